```python
import math
import jax, jax.numpy as jnp
from jax import lax
import numpy as np

D_MODEL = 2048
BATCH = 8
SEQ = 2048
DEPTH = 2
DEC_BATCH = 32
DEC_SEQ = 4
PAST_LEN = 8192
PAGE_SIZE = 128

N_EVEN = (DEPTH + 1) // 2
N_ODD = DEPTH // 2
W_A = D_MODEL // 2
H_A = 8
DH_A = W_A // H_A
MOBA_BLOCK = 256
MOBA_TOPK = 3
Q_CHUNK = 16
W_B = D_MODEL // 2
POOL_WINDOWS = (2, 4, 8, 16)
N_POOL = len(POOL_WINDOWS)
GW_B = W_B // N_POOL
POOL_HIST = max(POOL_WINDOWS) - 1
W_C = D_MODEL
H_C = 8
DH_C = W_C // H_C
MLSTM_CHUNK = 64
F_BIAS = 3.0
EPS = 1e-6

kernel_name = 'moba_pool_mlstm_hybrid_step'


def _rmsnorm(x, g):
    xf = x.astype(jnp.float32)
    r = xf * lax.rsqrt(jnp.mean(xf * xf, axis=-1, keepdims=True) + EPS)
    return (r * g.astype(jnp.float32)).astype(x.dtype)


def _alibi_slopes(n_heads):
    return 2.0 ** (-8.0 * jnp.arange(1, n_heads + 1, dtype=jnp.float32) / n_heads)


def _moba_attention(q, k, v, pos0):
    B, Tq, H, Dh = q.shape
    L = k.shape[1]
    nb = -(-L // MOBA_BLOCK)
    pad = nb * MOBA_BLOCK - L
    kb = jnp.pad(k, ((0, 0), (0, pad), (0, 0), (0, 0))).reshape(B, nb, MOBA_BLOCK, H, Dh).transpose(0, 3, 1, 2, 4)
    vb = jnp.pad(v, ((0, 0), (0, pad), (0, 0), (0, 0))).reshape(B, nb, MOBA_BLOCK, H, Dh).transpose(0, 3, 1, 2, 4)
    kmean = jnp.mean(kb.astype(jnp.float32), axis=3)
    q_pos = pos0 + jnp.arange(Tq, dtype=jnp.int32)
    n_past = q_pos // MOBA_BLOCK
    gate = jnp.einsum('bqhd,bhnd->bhqn', q.astype(jnp.float32), kmean)
    blk_ids = jnp.arange(nb, dtype=jnp.int32)
    gate = jnp.where(blk_ids[None, None, None, :] < n_past[None, None, :, None], gate, -jnp.inf)
    n_sel = min(MOBA_TOPK, nb)
    _, sel = lax.top_k(gate, n_sel)
    sel = sel.astype(jnp.int32)
    own = jnp.broadcast_to(n_past[None, None, :, None], (B, H, Tq, 1))
    blocks = jnp.concatenate([sel, own], axis=-1)
    valid = jnp.concatenate([jnp.arange(n_sel)[None, :] < n_past[:, None],
                             jnp.ones((Tq, 1), dtype=bool)], axis=-1)
    M = n_sel + 1
    qc = math.gcd(Tq, Q_CHUNK)
    nc = Tq // qc
    slopes = _alibi_slopes(H)
    scale = Dh ** -0.5
    bi = jnp.arange(B)[:, None, None, None]
    hi = jnp.arange(H)[None, :, None, None]
    offs = jnp.arange(MOBA_BLOCK, dtype=jnp.int32)

    def attend(args):
        q_c, blk_c, pos_c, val_c = args
        k_sel = kb[bi, hi, blk_c]
        v_sel = vb[bi, hi, blk_c]
        s = jnp.einsum('bhqd,bhqmpd->bhqmp', q_c, k_sel, preferred_element_type=jnp.float32) * scale
        k_pos = blk_c[..., None] * MOBA_BLOCK + offs
        dist = pos_c[None, None, :, None, None] - k_pos
        s = s - slopes[None, :, None, None, None] * dist.astype(jnp.float32)
        mask = (dist >= 0) & val_c[None, None, :, :, None]
        s = jnp.where(mask, s, -jnp.inf)
        p = jax.nn.softmax(s.reshape(B, H, qc, M * MOBA_BLOCK), axis=-1).reshape(s.shape)
        o = jnp.einsum('bhqmp,bhqmpd->bhqd', p, v_sel)
        return o.astype(q.dtype)

    qt = q.transpose(0, 2, 1, 3).reshape(B, H, nc, qc, Dh).transpose(2, 0, 1, 3, 4)
    bt = blocks.reshape(B, H, nc, qc, M).transpose(2, 0, 1, 3, 4)
    pt = q_pos.reshape(nc, qc)
    vt = valid.reshape(nc, qc, M)
    out = lax.map(attend, (qt, bt, pt, vt))
    return out.transpose(1, 0, 3, 2, 4).reshape(B, Tq, H * Dh)


def _multiscale_pool(u, prev, pos0, pool_w, pool_scale):
    B, T, W = u.shape
    P = POOL_HIST
    ext = jnp.concatenate([prev.astype(jnp.float32), u.astype(jnp.float32)], axis=1)
    cs = jnp.concatenate([jnp.zeros((B, 1, W), jnp.float32), jnp.cumsum(ext, axis=1)], axis=1)
    end = cs[:, P + 1:]
    pos = pos0 + jnp.arange(T, dtype=jnp.int32)
    means = []
    for g, w in enumerate(POOL_WINDOWS):
        sl = slice(g * GW_B, (g + 1) * GW_B)
        start = cs[:, P + 1 - w:P + 1 - w + T, sl]
        cnt = jnp.minimum(w, pos + 1).astype(jnp.float32)
        means.append((end[:, :, sl] - start) / cnt[None, :, None])
    d = jnp.concatenate(means, axis=-1) - ext[:, P:]
    d = jnp.einsum('btgc,gcd->btgd', d.reshape(B, T, N_POOL, GW_B), pool_w.astype(jnp.float32)).reshape(B, T, W)
    out = (d * pool_scale.astype(jnp.float32)).astype(u.dtype)
    return out, ext[:, -P:].astype(u.dtype)


def _mlstm(q, k, v, ig, lf, C0, n0, m0):
    B, T, H, DK = q.shape
    L = MLSTM_CHUNK if T % MLSTM_CHUNK == 0 else T
    nc = T // L
    f32 = jnp.float32

    def chunks(a):
        a = a.astype(f32).reshape((B, nc, L, H) + a.shape[3:])
        return jnp.moveaxis(jnp.moveaxis(a, 1, 0), 3, 2)

    causal = jnp.tril(jnp.ones((L, L), dtype=bool))

    def step(carry, xs):
        C, n, m = carry
        qc, kc, vc, ic, fc = xs
        b = jnp.cumsum(fc, axis=-1)
        dmat = jnp.where(causal, b[..., :, None] - b[..., None, :] + ic[..., None, :], -jnp.inf)
        carry_log = b + m[..., None]
        mt = jnp.maximum(carry_log, jnp.max(dmat, axis=-1))
        s = jnp.einsum('bhtd,bhsd->bhts', qc, kc) * jnp.exp(dmat - mt[..., None])
        inter = jnp.exp(carry_log - mt)
        num = jnp.einsum('bhts,bhsv->bhtv', s, vc) + inter[..., None] * jnp.einsum('bhvd,bhtd->bhtv', C, qc)
        den = jnp.sum(s, axis=-1) + inter * jnp.einsum('bhd,bhtd->bht', n, qc)
        h = num / jnp.maximum(jnp.abs(den), jnp.exp(-mt))[..., None]
        m_new = mt[..., -1]
        w_s = jnp.exp(b[..., -1:] - b + ic - m_new[..., None])
        decay = jnp.exp(b[..., -1] + m - m_new)
        C_new = decay[..., None, None] * C + jnp.einsum('bhsv,bhsd->bhvd', vc * w_s[..., None], kc)
        n_new = decay[..., None] * n + jnp.einsum('bhs,bhsd->bhd', w_s, kc)
        return (C_new, n_new, m_new), h

    (C, n, m), h = lax.scan(step, (C0.astype(f32), n0.astype(f32), m0.astype(f32)),
                            (chunks(q), chunks(k), chunks(v), chunks(ig), chunks(lf)))
    h = jnp.swapaxes(jnp.moveaxis(h, 0, 1), 2, 3).reshape(B, T, H, -1)
    return h, C, n, m


def _cond_in(x, c, ada_w, ada_b, g_pre):
    mod = jax.nn.silu(c) @ ada_w + ada_b
    shift, scale, gate = jnp.split(mod[:, None, :], 3, axis=-1)
    return _rmsnorm(x, g_pre) * (1 + scale) + shift, gate


def _even_mixer(h, k_past, v_past, pool_prev, pos0, w_in, pool_w, pool_scale, w_out):
    B, T, _ = h.shape
    z = h @ w_in
    q, k, v, ga, u, gb = jnp.split(z, [W_A, 2 * W_A, 3 * W_A, 4 * W_A, 4 * W_A + W_B], axis=-1)
    shp = (B, T, H_A, DH_A)
    q, k, v = q.reshape(shp), k.reshape(shp), v.reshape(shp)
    k_all = k if k_past is None else jnp.concatenate([k_past.astype(k.dtype), k], axis=1)
    v_all = v if v_past is None else jnp.concatenate([v_past.astype(v.dtype), v], axis=1)
    att = _moba_attention(q, k_all, v_all, pos0)
    pooled, pool_new = _multiscale_pool(u, pool_prev, pos0, pool_w, pool_scale)
    y = jnp.concatenate([att * jax.nn.silu(ga), pooled * jax.nn.silu(gb)], axis=-1) @ w_out
    return y, k, v, pool_new


def _odd_mixer(h, C0, n0, m0, w_in, b_i, b_f, hn_gain, w_out):
    B, T, _ = h.shape
    z = h @ w_in
    q, k, v, o, g, gi, gf = jnp.split(z, [W_C, 2 * W_C, 3 * W_C, 4 * W_C, 5 * W_C, 5 * W_C + H_C], axis=-1)
    shp = (B, T, H_C, DH_C)
    ig = (gi + b_i).astype(jnp.float32)
    lf = jax.nn.log_sigmoid((gf + b_f).astype(jnp.float32))
    hc, C, n, m = _mlstm(q.reshape(shp), k.reshape(shp) * DH_C ** -0.5, v.reshape(shp), ig, lf, C0, n0, m0)
    hc = hc * jax.nn.sigmoid(o.reshape(shp).astype(jnp.float32))
    hc = hc * lax.rsqrt(jnp.mean(hc * hc, axis=-1, keepdims=True) + EPS) * hn_gain.reshape(H_C, DH_C).astype(jnp.float32)
    y = (hc.reshape(B, T, W_C).astype(h.dtype) * jax.nn.silu(g)) @ w_out
    return y, C, n, m


def setup_inputs(seed: int = 0) -> dict:
    key = jax.random.key(seed)
    ks = jax.random.split(key, 24)
    f32 = jnp.float32
    n_pages = PAST_LEN // PAGE_SIZE
    n_used = DEC_BATCH * n_pages
    n_phys = n_used + max(1, n_used // 4)
    page_table = jax.random.permutation(ks[0], n_phys)[:n_used].reshape(DEC_BATCH, n_pages).astype(jnp.int32)
    nrm = lambda k, shp, s=1.0: jax.random.normal(k, shp, f32) * s
    return {
        'x_prompt': nrm(ks[1], (BATCH, SEQ, D_MODEL)),
        'x_sample': nrm(ks[2], (DEC_BATCH, DEC_SEQ, D_MODEL)),
        'cache_k': nrm(ks[3], (N_EVEN, n_phys, PAGE_SIZE, H_A, DH_A)),
        'cache_v': nrm(ks[4], (N_EVEN, n_phys, PAGE_SIZE, H_A, DH_A)),
        'state_pool': nrm(ks[5], (N_EVEN, DEC_BATCH, POOL_HIST, W_B)),
        'state_C': nrm(ks[6], (N_ODD, DEC_BATCH, H_C, DH_C, DH_C), DH_C ** -0.5),
        'state_n': nrm(ks[7], (N_ODD, DEC_BATCH, H_C, DH_C), 0.5),
        'state_m': nrm(ks[8], (N_ODD, DEC_BATCH, H_C), 0.5),
        'page_table': page_table,
        'c_prompt': nrm(ks[9], (BATCH, D_MODEL)),
        'c_sample': nrm(ks[10], (DEC_BATCH, D_MODEL)),
        'ada_w': nrm(ks[11], (DEPTH, D_MODEL, 3 * D_MODEL), 0.5 * D_MODEL ** -0.5),
        'ada_b': nrm(ks[12], (DEPTH, 3 * D_MODEL), 0.01),
        'norm_pre': 1.0 + nrm(ks[13], (DEPTH, D_MODEL), 0.05),
        'norm_post': 1.0 + nrm(ks[14], (DEPTH, D_MODEL), 0.05),
        'w_in_even': nrm(ks[15], (N_EVEN, D_MODEL, 4 * W_A + 2 * W_B), D_MODEL ** -0.5),
        'pool_w': nrm(ks[16], (N_EVEN, N_POOL, GW_B, GW_B), GW_B ** -0.5),
        'pool_scale': 1.0 + nrm(ks[17], (N_EVEN, W_B), 0.1),
        'w_out_even': nrm(ks[18], (N_EVEN, W_A + W_B, D_MODEL), (W_A + W_B) ** -0.5),
        'w_in_odd': nrm(ks[19], (N_ODD, D_MODEL, 5 * W_C + 2 * H_C), D_MODEL ** -0.5),
        'b_igate': nrm(ks[20], (N_ODD, H_C), 0.1),
        'b_fgate': F_BIAS + nrm(ks[21], (N_ODD, H_C), 0.5),
        'head_norm': 1.0 + nrm(ks[22], (N_ODD, W_C), 0.05),
        'w_out_odd': nrm(ks[23], (N_ODD, W_C, D_MODEL), W_C ** -0.5),
    }


def reference(x_prompt, x_sample, cache_k, cache_v, state_pool, state_C, state_n, state_m, page_table,
              c_prompt, c_sample, ada_w, ada_b, norm_pre, norm_post, w_in_even, pool_w, pool_scale,
              w_out_even, w_in_odd, b_igate, b_fgate, head_norm, w_out_odd):
    xp, xs = x_prompt, x_sample
    bp, bs = x_prompt.shape[0], x_sample.shape[0]
    kp_l, vp_l, ks_l, vs_l, pp_l, ps_l = [], [], [], [], [], []
    Cp_l, np_l, mp_l, Cs_l, ns_l, ms_l = [], [], [], [], [], []
    for l in range(DEPTH):
        hp, gp = _cond_in(xp, c_prompt, ada_w[l], ada_b[l], norm_pre[l])
        hs, gs = _cond_in(xs, c_sample, ada_w[l], ada_b[l], norm_pre[l])
        if l % 2 == 0:
            e = l // 2
            prev0 = jnp.zeros((bp, POOL_HIST, W_B), xp.dtype)
            yp, kp, vp, pp = _even_mixer(hp, None, None, prev0, 0,
                                         w_in_even[e], pool_w[e], pool_scale[e], w_out_even[e])
            k_past = cache_k[e][page_table].reshape(bs, -1, H_A, DH_A)
            v_past = cache_v[e][page_table].reshape(bs, -1, H_A, DH_A)
            ys, kn, vn, ps = _even_mixer(hs, k_past, v_past, state_pool[e], k_past.shape[1],
                                         w_in_even[e], pool_w[e], pool_scale[e], w_out_even[e])
            kp_l.append(kp); vp_l.append(vp); ks_l.append(kn); vs_l.append(vn)
            pp_l.append(pp); ps_l.append(ps.astype(state_pool.dtype))
        else:
            o = l // 2
            C0 = jnp.zeros((bp, H_C, DH_C, DH_C), jnp.float32)
            n0 = jnp.zeros((bp, H_C, DH_C), jnp.float32)
            m0 = jnp.zeros((bp, H_C), jnp.float32)
            yp, Cp, npr, mp = _odd_mixer(hp, C0, n0, m0, w_in_odd[o], b_igate[o], b_fgate[o], head_norm[o], w_out_odd[o])
            ys, Cs, ns, ms = _odd_mixer(hs, state_C[o], state_n[o], state_m[o],
                                        w_in_odd[o], b_igate[o], b_fgate[o], head_norm[o], w_out_odd[o])
            Cp_l.append(Cp); np_l.append(npr); mp_l.append(mp)
            Cs_l.append(Cs.astype(state_C.dtype)); ns_l.append(ns.astype(state_n.dtype)); ms_l.append(ms.astype(state_m.dtype))
        xp = xp + gp * _rmsnorm(yp, norm_post[l])
        xs = xs + gs * _rmsnorm(ys, norm_post[l])
    return (xp, xs,
            jnp.stack(kp_l), jnp.stack(vp_l), jnp.stack(ks_l), jnp.stack(vs_l),
            jnp.stack(pp_l), jnp.stack(ps_l),
            jnp.stack(Cp_l), jnp.stack(np_l), jnp.stack(mp_l),
            jnp.stack(Cs_l), jnp.stack(ns_l), jnp.stack(ms_l))
```

```python
import functools
import math

import jax
import jax.numpy as jnp
from jax import lax
from jax.experimental import pallas as pl
from jax.experimental.pallas import tpu as pltpu

F32 = jnp.float32
BF16 = jnp.bfloat16

H_A = 8
DH_A = 128
MOBA_BLOCK = 256
MOBA_TOPK = 3
POOL_WINDOWS = (2, 4, 8, 16)
GW_B = 256
POOL_HIST = max(POOL_WINDOWS) - 1
H_C = 8
DH_C = 256
EPS = 1e-6
PAGE_SIZE = 128

SEG = 1024
NEG_BIG = -1e30

VMEM_LIMIT_V7X = 52 * 1024 * 1024


def _silu(x):
    return x * (1.0 / (1.0 + jnp.exp(-x)))


def _sigmoid(x):
    return 1.0 / (1.0 + jnp.exp(-x))


def _log_sigmoid(x):
    return -(jnp.maximum(-x, 0.0) + jnp.log(1.0 + jnp.exp(-jnp.abs(x))))


def _cparams(sem):
    return pltpu.CompilerParams(dimension_semantics=sem, vmem_limit_bytes=VMEM_LIMIT_V7X)


def _adaln_kernel(c_ref, w_ref, b_ref, o_ref):
    s = _silu(c_ref[...]).astype(BF16)
    o_ref[...] = jnp.dot(s, w_ref[...].astype(BF16), preferred_element_type=F32) + b_ref[...]


def _adaln(c_all, ada_w, ada_b, tn=512):
    depth, d, n = ada_w.shape
    r = c_all.shape[0]
    return pl.pallas_call(
        _adaln_kernel,
        grid=(depth, n // tn),
        in_specs=[
            pl.BlockSpec((r, d), lambda l, j: (0, 0)),
            pl.BlockSpec((None, d, tn), lambda l, j: (l, 0, j)),
            pl.BlockSpec((None, 1, tn), lambda l, j: (l, 0, j)),
        ],
        out_specs=pl.BlockSpec((None, r, tn), lambda l, j: (l, 0, j)),
        out_shape=jax.ShapeDtypeStruct((depth, r, n), F32),
        compiler_params=_cparams(("parallel", "parallel")),
        name="adaln",
    )(c_all, ada_w, ada_b.reshape(depth, 1, n))


def _inproj_kernel(*refs, has_gate):
    if has_gate:
        x_ref, g_ref, sc_ref, sh_ref, w_ref, wg_ref, z_ref, gt_ref, h_scr = refs
    else:
        x_ref, g_ref, sc_ref, sh_ref, w_ref, z_ref, h_scr = refs

    @pl.when(pl.program_id(1) == 0)
    def _():
        x = x_ref[...]
        r = x * lax.rsqrt(jnp.mean(x * x, axis=-1, keepdims=True) + EPS)
        h = (r * g_ref[...]) * (1.0 + sc_ref[...]) + sh_ref[...]
        h_scr[...] = h.astype(BF16)
        if has_gate:
            gt_ref[...] = jnp.dot(h_scr[...], wg_ref[...], preferred_element_type=F32)

    z_ref[...] = jnp.dot(h_scr[...], w_ref[...], preferred_element_type=F32)


def _inproj(x2, g_pre, scale, shift, w_bf, wg_bf, *, rows_per_group, tm):
    r, d = x2.shape
    n = w_bf.shape[1]
    nseg = n // SEG
    has_gate = wg_bf is not None
    if scale.ndim == 3:
        per = rows_per_group // tm
        mod_spec = pl.BlockSpec((None, 1, d), lambda i, j: (i // per, 0, 0))
    else:
        mod_spec = pl.BlockSpec((tm, d), lambda i, j: (i, 0))
    in_specs = [
        pl.BlockSpec((tm, d), lambda i, j: (i, 0)),
        pl.BlockSpec((1, d), lambda i, j: (0, 0)),
        mod_spec,
        mod_spec,
        pl.BlockSpec((d, SEG), lambda i, j: (0, j)),
    ]
    args = [x2, g_pre.reshape(1, d), scale, shift, w_bf]
    out_specs = [pl.BlockSpec((None, tm, SEG), lambda i, j: (j, i, 0))]
    out_shape = [jax.ShapeDtypeStruct((nseg, r, SEG), F32)]
    if has_gate:
        in_specs.append(pl.BlockSpec((d, 128), lambda i, j: (0, 0)))
        args.append(wg_bf)
        out_specs.append(pl.BlockSpec((tm, 128), lambda i, j: (i, 0)))
        out_shape.append(jax.ShapeDtypeStruct((r, 128), F32))
    res = pl.pallas_call(
        functools.partial(_inproj_kernel, has_gate=has_gate),
        grid=(r // tm, nseg),
        in_specs=in_specs,
        out_specs=out_specs,
        out_shape=out_shape,
        scratch_shapes=[pltpu.VMEM((tm, d), BF16)],
        compiler_params=_cparams(("parallel", "arbitrary")),
        name="inproj_gate" if has_gate else "inproj",
    )(*args)
    return (res[0], res[1]) if has_gate else (res[0], None)


def _outproj_kernel(*refs, n_a):
    a_refs = refs[:n_a]
    w_refs = refs[n_a:2 * n_a]
    x_ref, gate_ref, gp_ref, o_ref = refs[2 * n_a:]
    y = None
    for a_ref, w_ref in zip(a_refs, w_refs):
        t = jnp.dot(a_ref[...].astype(BF16), w_ref[...], preferred_element_type=F32)
        y = t if y is None else y + t
    r = y * lax.rsqrt(jnp.mean(y * y, axis=-1, keepdims=True) + EPS)
    o_ref[...] = x_ref[...] + gate_ref[...] * (r * gp_ref[...])


def _outproj(a_list, w_list, x2, gate, g_post, *, rows_per_group, tm):
    r, d = x2.shape
    n_a = len(a_list)
    if gate.ndim == 3:
        per = rows_per_group // tm
        gate_spec = pl.BlockSpec((None, 1, d), lambda i: (i // per, 0, 0))
    else:
        gate_spec = pl.BlockSpec((tm, d), lambda i: (i, 0))
    in_specs = [pl.BlockSpec((tm, a.shape[1]), lambda i: (i, 0)) for a in a_list]
    in_specs += [pl.BlockSpec(w.shape, lambda i: (0, 0)) for w in w_list]
    in_specs += [pl.BlockSpec((tm, d), lambda i: (i, 0)), gate_spec, pl.BlockSpec((1, d), lambda i: (0, 0))]
    return pl.pallas_call(
        functools.partial(_outproj_kernel, n_a=n_a),
        grid=(r // tm,),
        in_specs=in_specs,
        out_specs=pl.BlockSpec((tm, d), lambda i: (i, 0)),
        out_shape=jax.ShapeDtypeStruct((r, d), F32),
        compiler_params=_cparams(("parallel",)),
        name="outproj",
    )(*a_list, *w_list, x2, gate, g_post.reshape(1, d))


def _moba_kernel(slopes_ref, q_ref, k_ref, v_ref, ga_ref, o_ref,
                 kb_scr, vb_scr, kmean_scr, m_scr, l_scr, acc_scr, *, nb):
    h = pl.program_id(1)
    i = pl.program_id(2)
    blk = MOBA_BLOCK

    @pl.when(i == 0)
    def _():
        kb_scr[...] = k_ref[...].astype(BF16)
        vb_scr[...] = v_ref[...].astype(BF16)
        for j in range(nb):
            kmean_scr[j:j + 1, :] = jnp.mean(k_ref[j * blk:(j + 1) * blk, :], axis=0, keepdims=True)

    q = q_ref[...]
    gate = lax.dot_general(q, kmean_scr[...], (((1,), (1,)), ((), ())),
                           precision=lax.Precision.HIGHEST, preferred_element_type=F32)
    col = lax.broadcasted_iota(jnp.int32, (blk, nb), 1)
    past = col < i
    gm = jnp.where(past, gate, -jnp.inf)
    cnt = jnp.zeros((blk, nb), F32)
    for kk in range(nb):
        gk = gm[:, kk:kk + 1]
        beats = (gk > gm) | ((gk == gm) & (col > kk))
        cnt = cnt + jnp.where(beats, 1.0, 0.0)
    selown = jnp.where((past & (cnt < MOBA_TOPK)) | (col == i), 1.0, 0.0)

    m_scr[...] = jnp.full((blk, 1), -jnp.inf, F32)
    l_scr[...] = jnp.zeros((blk, 1), F32)
    acc_scr[...] = jnp.zeros((blk, DH_A), F32)
    qb = q.astype(BF16)
    slope = slopes_ref[h]
    scale = DH_A ** -0.5
    qpos = i * blk + lax.broadcasted_iota(jnp.int32, (blk, 1), 0)

    for j in range(nb):
        @pl.when(j <= i)
        def _(j=j):
            kj = kb_scr[j * blk:(j + 1) * blk, :]
            s = lax.dot_general(qb, kj, (((1,), (1,)), ((), ())), preferred_element_type=F32) * scale
            kpos = j * blk + lax.broadcasted_iota(jnp.int32, (1, blk), 1)
            dist = qpos - kpos
            s = s - slope * dist.astype(F32)
            valid = (dist >= 0) & (selown[:, j:j + 1] > 0.5)
            s = jnp.where(valid, s, -jnp.inf)
            m_prev = m_scr[...]
            m_new = jnp.maximum(m_prev, jnp.max(s, axis=-1, keepdims=True))
            m_safe = jnp.where(m_new == -jnp.inf, 0.0, m_new)
            alpha = jnp.exp(m_prev - m_safe)
            p = jnp.exp(s - m_safe)
            l_scr[...] = alpha * l_scr[...] + jnp.sum(p, axis=-1, keepdims=True)
            acc_scr[...] = alpha * acc_scr[...] + jnp.dot(
                p.astype(BF16), vb_scr[j * blk:(j + 1) * blk, :], preferred_element_type=F32)
            m_scr[...] = m_new

    o = acc_scr[...] / l_scr[...]
    o_ref[...] = (o * _silu(ga_ref[...])).astype(o_ref.dtype)


def _moba_prompt(z4, slopes):
    _, b, t, w = z4.shape
    nb = t // MOBA_BLOCK
    blk = MOBA_BLOCK
    grid_spec = pltpu.PrefetchScalarGridSpec(
        num_scalar_prefetch=1,
        grid=(b, H_A, nb),
        in_specs=[
            pl.BlockSpec((None, None, blk, DH_A), lambda bi, h, i, s: (0, bi, i, h)),
            pl.BlockSpec((None, None, t, DH_A), lambda bi, h, i, s: (1, bi, 0, h)),
            pl.BlockSpec((None, None, t, DH_A), lambda bi, h, i, s: (2, bi, 0, h)),
            pl.BlockSpec((None, None, blk, DH_A), lambda bi, h, i, s: (3, bi, i, h)),
        ],
        out_specs=pl.BlockSpec((None, blk, DH_A), lambda bi, h, i, s: (bi, i, h)),
        scratch_shapes=[
            pltpu.VMEM((t, DH_A), BF16),
            pltpu.VMEM((t, DH_A), BF16),
            pltpu.VMEM((nb, DH_A), F32),
            pltpu.VMEM((blk, 1), F32),
            pltpu.VMEM((blk, 1), F32),
            pltpu.VMEM((blk, DH_A), F32),
        ],
    )
    return pl.pallas_call(
        functools.partial(_moba_kernel, nb=nb),
        grid_spec=grid_spec,
        out_shape=jax.ShapeDtypeStruct((b, t, w), BF16),
        compiler_params=_cparams(("parallel", "parallel", "arbitrary")),
        name="moba_prompt",
    )(slopes, z4, z4, z4, z4)


def _pool_kernel(u_ref, gb_ref, prev_ref, pw_ref, ps_ref, o_ref, pn_ref, ext_scr, *, tt, pos0, nt):
    t = pl.program_id(1)
    hist = POOL_HIST + 1

    @pl.when(t == 0)
    def _():
        ext_scr[0:1, :] = jnp.zeros((1, ext_scr.shape[1]), F32)
        ext_scr[1:hist, :] = prev_ref[...]

    ext_scr[hist:hist + tt, :] = u_ref[...]
    pos = pos0 + t * tt + lax.broadcasted_iota(jnp.int32, (tt, 1), 0)
    for g, w in enumerate(POOL_WINDOWS):
        sl = slice(g * GW_B, (g + 1) * GW_B)
        x = u_ref[:, sl]
        acc = x
        for jj in range(1, w):
            acc = acc + ext_scr[hist - jj:hist - jj + tt, sl]
        cnt = jnp.minimum(w, pos + 1).astype(F32)
        d = acc / cnt - x
        y = jnp.dot(d.astype(BF16), pw_ref[g], preferred_element_type=F32) * ps_ref[:, sl]
        o_ref[:, sl] = (y * _silu(gb_ref[:, sl])).astype(o_ref.dtype)

    tail = ext_scr[tt:tt + hist, :]
    ext_scr[0:hist, :] = tail

    @pl.when(t == nt - 1)
    def _():
        pn_ref[...] = ext_scr[1:hist, :]


def _pool(z4, prev, pool_w_bf, pool_scale, *, pos0, tt, out_dtype):
    _, b, t, w = z4.shape
    nt = t // tt
    return pl.pallas_call(
        functools.partial(_pool_kernel, tt=tt, pos0=pos0, nt=nt),
        grid=(b, nt),
        in_specs=[
            pl.BlockSpec((None, None, tt, w), lambda bi, ti: (4, bi, ti, 0)),
            pl.BlockSpec((None, None, tt, w), lambda bi, ti: (5, bi, ti, 0)),
            pl.BlockSpec((None, POOL_HIST, w), lambda bi, ti: (bi, 0, 0)),
            pl.BlockSpec(pool_w_bf.shape, lambda bi, ti: (0, 0, 0)),
            pl.BlockSpec((1, w), lambda bi, ti: (0, 0)),
        ],
        out_specs=[
            pl.BlockSpec((None, tt, w), lambda bi, ti: (bi, ti, 0)),
            pl.BlockSpec((None, POOL_HIST, w), lambda bi, ti: (bi, 0, 0)),
        ],
        out_shape=[
            jax.ShapeDtypeStruct((b, t, w), out_dtype),
            jax.ShapeDtypeStruct((b, POOL_HIST, w), F32),
        ],
        scratch_shapes=[pltpu.VMEM((POOL_HIST + 1 + tt, w), F32)],
        compiler_params=_cparams(("parallel", "arbitrary")),
        name="pool",
    )(z4, z4, prev, pool_w_bf, pool_scale.reshape(1, w))


def _mlstm_kernel(*refs, lv, lp, zero_state):
    if zero_state:
        (bias_ref, q_ref, k_ref, v_ref, og_ref, g_ref, gt_ref, hn_ref,
         h_out, c_out, n_out, m_out, c_scr, n_scr, m_scr, pad_scr) = refs
    else:
        (bias_ref, m0_ref, q_ref, k_ref, v_ref, og_ref, g_ref, gt_ref, hn_ref, c0_ref, n0_ref,
         h_out, c_out, n_out, m_out, c_scr, n_scr, m_scr, pad_scr) = refs
    bi = pl.program_id(0)
    h = pl.program_id(1)
    c = pl.program_id(2)
    nc = pl.num_programs(2)

    @pl.when(c == 0)
    def _():
        if zero_state:
            c_scr[...] = jnp.zeros(c_scr.shape, F32)
            n_scr[...] = jnp.zeros(n_scr.shape, F32)
            m_scr[...] = jnp.zeros(m_scr.shape, F32)
        else:
            c_scr[...] = c0_ref[...]
            n_scr[...] = n0_ref[...]
            m_scr[...] = jnp.full(m_scr.shape, m0_ref[bi, h], F32)

    def load(ref, slot):
        if lv == lp:
            return ref[...]
        pad_scr[slot] = jnp.zeros(pad_scr.shape[1:], F32)
        pad_scr[slot, 0:lv, 0:ref.shape[1]] = ref[...]
        return pad_scr[slot][:, 0:ref.shape[1]]

    q = load(q_ref, 0)
    k = load(k_ref, 1) * (DH_C ** -0.5)
    v = load(v_ref, 2)
    og = load(og_ref, 3)
    g = load(g_ref, 4)
    gts = load(gt_ref, 5)

    lane = lax.broadcasted_iota(jnp.int32, gts.shape, 1)
    row1 = lax.broadcasted_iota(jnp.int32, (lp, 1), 0)
    ig_col = jnp.sum(jnp.where(lane == h, gts, 0.0), axis=1, keepdims=True) + bias_ref[0, h]
    gf_col = jnp.sum(jnp.where(lane == H_C + h, gts, 0.0), axis=1, keepdims=True) + bias_ref[1, h]
    lf_col = _log_sigmoid(gf_col)
    if lv != lp:
        ig_col = jnp.where(row1 < lv, ig_col, NEG_BIG)
        lf_col = jnp.where(row1 < lv, lf_col, 0.0)

    rr = lax.broadcasted_iota(jnp.int32, (lp, lp), 0)
    cc = lax.broadcasted_iota(jnp.int32, (lp, lp), 1)
    eye = rr == cc
    tril = rr >= cc

    def to_row(col):
        return jnp.sum(jnp.where(eye, col, 0.0), axis=0, keepdims=True)

    lf_row = to_row(lf_col)
    ig_row = to_row(ig_col)
    b_col = jnp.sum(jnp.where(tril, lf_row, 0.0), axis=1, keepdims=True)
    b_row = to_row(b_col)
    dmat = jnp.where(tril, b_col - b_row + ig_row, -jnp.inf)
    m_prev = m_scr[...]
    carry = b_col + m_prev
    mt = jnp.maximum(carry, jnp.max(dmat, axis=1, keepdims=True))

    qb = q.astype(BF16)
    kb = k.astype(BF16)
    vb = v.astype(BF16)
    s = lax.dot_general(qb, kb, (((1,), (1,)), ((), ())), preferred_element_type=F32) * jnp.exp(dmat - mt)
    inter = jnp.exp(carry - mt)
    cq = lax.dot_general(qb, c_scr[...].astype(BF16), (((1,), (1,)), ((), ())), preferred_element_type=F32)
    num = jnp.dot(s.astype(BF16), vb, preferred_element_type=F32) + inter * cq
    nq = jnp.sum(q * n_scr[...], axis=1, keepdims=True)
    den = jnp.sum(s, axis=1, keepdims=True) + inter * nq
    hc = num / jnp.maximum(jnp.abs(den), jnp.exp(-mt))

    m_new = mt[lp - 1:lp, :]
    b_last = b_col[lp - 1:lp, :]
    w_col = jnp.exp(b_last - b_col + ig_col - m_new)
    decay = jnp.exp(b_last + m_prev - m_new)
    vw = (v * w_col).astype(BF16)
    c_scr[...] = decay * c_scr[...] + lax.dot_general(
        vw, kb, (((0,), (0,)), ((), ())), preferred_element_type=F32)
    n_scr[...] = decay * n_scr[...] + jnp.sum(k * w_col, axis=0, keepdims=True)
    m_scr[...] = m_new

    hc = hc * _sigmoid(og)
    hc = hc * lax.rsqrt(jnp.mean(hc * hc, axis=-1, keepdims=True) + EPS) * hn_ref[...]
    res = (hc * _silu(g)).astype(h_out.dtype)
    h_out[...] = res if lv == lp else res[0:lv, :]

    @pl.when(c == nc - 1)
    def _():
        c_out[...] = c_scr[...]
        n_out[...] = n_scr[...]
        m_out[...] = jnp.broadcast_to(m_scr[...], m_out.shape)


def _mlstm(z4, gates3, bias, head_norm, state, *, chunk, lp):
    _, b, t, _ = z4.shape
    nc = t // chunk
    per = SEG // DH_C
    zero_state = state is None
    n_pref = 1 if zero_state else 2

    def seg_spec(base):
        return pl.BlockSpec((None, None, chunk, DH_C),
                            lambda bi, h, c, *_: (base + h // per, bi, c, h % per))

    in_specs = [seg_spec(0), seg_spec(2), seg_spec(4), seg_spec(6), seg_spec(8),
                pl.BlockSpec((None, chunk, 128), lambda bi, h, c, *_: (bi, c, 0)),
                pl.BlockSpec((1, DH_C), lambda bi, h, c, *_: (0, h))]
    args = [z4, z4, z4, z4, z4, gates3, head_norm.reshape(1, H_C * DH_C)]
    prefetch = [bias]
    if not zero_state:
        c0, n0, m0 = state
        prefetch.append(m0)
        in_specs += [pl.BlockSpec((None, None, DH_C, DH_C), lambda bi, h, c, *_: (bi, h, 0, 0)),
                     pl.BlockSpec((None, None, 1, DH_C), lambda bi, h, c, *_: (bi, h, 0, 0))]
        args += [c0, n0]
    grid_spec = pltpu.PrefetchScalarGridSpec(
        num_scalar_prefetch=n_pref,
        grid=(b, H_C, nc),
        in_specs=in_specs,
        out_specs=[
            pl.BlockSpec((None, chunk, DH_C), lambda bi, h, c, *_: (bi, c, h)),
            pl.BlockSpec((None, None, DH_C, DH_C), lambda bi, h, c, *_: (bi, h, 0, 0)),
            pl.BlockSpec((None, None, 1, DH_C), lambda bi, h, c, *_: (bi, h, 0, 0)),
            pl.BlockSpec((None, None, 1, 128), lambda bi, h, c, *_: (bi, h, 0, 0)),
        ],
        scratch_shapes=[
            pltpu.VMEM((DH_C, DH_C), F32),
            pltpu.VMEM((1, DH_C), F32),
            pltpu.VMEM((1, 1), F32),
            pltpu.VMEM((6, lp, DH_C), F32),
        ],
    )
    return pl.pallas_call(
        functools.partial(_mlstm_kernel, lv=chunk, lp=lp, zero_state=zero_state),
        grid_spec=grid_spec,
        out_shape=[
            jax.ShapeDtypeStruct((b, t, H_C * DH_C), BF16),
            jax.ShapeDtypeStruct((b, H_C, DH_C, DH_C), F32),
            jax.ShapeDtypeStruct((b, H_C, 1, DH_C), F32),
            jax.ShapeDtypeStruct((b, H_C, 1, 128), F32),
        ],
        compiler_params=_cparams(("parallel", "parallel", "arbitrary")),
        name="mlstm",
    )(*prefetch, *args)


PAGES_PER_STEP = 16


def _kmean_topk_kernel(pt_ref, *refs, n_steps, tq):
    page_refs = refs[:PAGES_PER_STEP]
    q_ref, sel_ref, kmean_scr = refs[PAGES_PER_STEP:]
    s = pl.program_id(1)
    ppb = MOBA_BLOCK // PAGE_SIZE
    for jj in range(PAGES_PER_STEP // ppb):
        tot = None
        for p in range(ppb):
            part = jnp.sum(page_refs[jj * ppb + p][...], axis=0, keepdims=True)
            tot = part if tot is None else tot + part
        kmean_scr[pl.ds(s * (PAGES_PER_STEP // ppb) + jj, 1), :] = tot * (1.0 / MOBA_BLOCK)

    @pl.when(s == n_steps - 1)
    def _():
        nbp = kmean_scr.shape[0]
        w = kmean_scr.shape[1]
        q = q_ref[...]
        rows = H_A * tq
        lane_head = lax.broadcasted_iota(jnp.int32, (H_A, w), 1) // DH_A
        row_head = lax.broadcasted_iota(jnp.int32, (H_A, w), 0)
        pieces = [jnp.where(lane_head == row_head, jnp.broadcast_to(q[t:t + 1, :], (H_A, w)), 0.0)
                  for t in range(tq)]
        qbd = jnp.concatenate(pieces, axis=0)
        gate = lax.dot_general(qbd, kmean_scr[...], (((1,), (1,)), ((), ())),
                               precision=lax.Precision.HIGHEST, preferred_element_type=F32)
        lane = lax.broadcasted_iota(jnp.int32, (rows, nbp), 1)
        out_lane = lax.broadcasted_iota(jnp.int32, (rows, 128), 1)
        sel = jnp.zeros((rows, 128), jnp.int32)
        for t in range(MOBA_TOPK):
            mx = jnp.max(gate, axis=1, keepdims=True)
            idx = jnp.min(jnp.where(gate == mx, lane, nbp), axis=1, keepdims=True)
            sel = jnp.where(out_lane == t, idx, sel)
            gate = jnp.where(lane == idx, -jnp.inf, gate)
        sel_ref[...] = sel


def _kmean_topk(cache_k3, page_table, z4s):
    b, n_pages = page_table.shape
    _, _, tq, w = z4s.shape
    n_steps = n_pages // PAGES_PER_STEP
    nbp = n_pages * PAGE_SIZE // MOBA_BLOCK

    def page_spec(p):
        return pl.BlockSpec((None, PAGE_SIZE, w), lambda bi, s, pt: (pt[bi, s * PAGES_PER_STEP + p], 0, 0))

    grid_spec = pltpu.PrefetchScalarGridSpec(
        num_scalar_prefetch=1,
        grid=(b, n_steps),
        in_specs=[page_spec(p) for p in range(PAGES_PER_STEP)]
        + [pl.BlockSpec((None, None, tq, w), lambda bi, s, pt: (0, bi, 0, 0))],
        out_specs=pl.BlockSpec((None, H_A * tq, 128), lambda bi, s, pt: (bi, 0, 0)),
        scratch_shapes=[pltpu.VMEM((nbp, w), F32)],
    )
    return pl.pallas_call(
        functools.partial(_kmean_topk_kernel, n_steps=n_steps, tq=tq),
        grid_spec=grid_spec,
        out_shape=jax.ShapeDtypeStruct((b, H_A * tq, 128), jnp.int32),
        compiler_params=_cparams(("parallel", "arbitrary")),
        name="kmean_topk",
    )(page_table, *([cache_k3] * PAGES_PER_STEP), z4s)


def _sample_attn_kernel(pt_ref, sel_ref, slopes_ref, *refs, tq, past_len, n_pages):
    ppb = MOBA_BLOCK // PAGE_SIZE
    n_pg = MOBA_TOPK * ppb
    k_refs = refs[:n_pg]
    v_refs = refs[n_pg:2 * n_pg]
    q_ref, kn_ref, vn_ref, ga_ref, o_ref = refs[2 * n_pg:]
    bi = pl.program_id(0)
    h = pl.program_id(1)
    qi = pl.program_id(2)
    slope = slopes_ref[h]
    scale = DH_A ** -0.5
    qpos = past_len + qi

    qrow = q_ref[pl.ds(qi, 1), :]
    q8 = jnp.broadcast_to(qrow, (8, DH_A)).astype(BF16)
    lane = lax.broadcasted_iota(jnp.int32, (1, PAGE_SIZE), 1)

    s_list = []
    for t in range(MOBA_TOPK):
        blk = sel_ref[((bi * tq + qi) * H_A + h) * MOBA_TOPK + t]
        for pg in range(ppb):
            kp = k_refs[t * ppb + pg][...].astype(BF16)
            s = lax.dot_general(q8, kp, (((1,), (1,)), ((), ())), preferred_element_type=F32)[0:1, :] * scale
            kpos = blk * MOBA_BLOCK + pg * PAGE_SIZE + lane
            s_list.append(s - slope * (qpos - kpos).astype(F32))
    trow = lax.broadcasted_iota(jnp.int32, (tq, 1), 0)
    s_own = jnp.sum(kn_ref[...] * qrow, axis=1, keepdims=True) * scale
    s_own = s_own - slope * (qi - trow).astype(F32)
    s_own = jnp.where(trow <= qi, s_own, -jnp.inf)

    m = jnp.max(s_own, axis=0, keepdims=True)
    for s in s_list:
        m = jnp.maximum(m, jnp.max(s, axis=1, keepdims=True))
    p_own = jnp.exp(s_own - m)
    l = jnp.sum(p_own, axis=0, keepdims=True)
    acc = jnp.sum(p_own * vn_ref[...], axis=0, keepdims=True)
    for idx, s in enumerate(s_list):
        p = jnp.exp(s - m)
        l = l + jnp.sum(p, axis=1, keepdims=True)
        p8 = jnp.broadcast_to(p, (8, PAGE_SIZE)).astype(BF16)
        acc = acc + jnp.dot(p8, v_refs[idx][...].astype(BF16), preferred_element_type=F32)[0:1, :]
    o = acc / l
    o_ref[pl.ds(qi, 1), :] = o * _silu(ga_ref[pl.ds(qi, 1), :])


def _sample_attn(cache_k3, cache_v3, pt_flat, sel_flat, slopes, z4s, *, past_len, n_pages):
    _, b, tq, w = z4s.shape
    ppb = MOBA_BLOCK // PAGE_SIZE

    def page_spec(t, pg):
        def imap(bi, h, qi, pt, sel, sl):
            blk = sel[((bi * tq + qi) * H_A + h) * MOBA_TOPK + t]
            return (pt[bi * n_pages + blk * ppb + pg], 0, h)
        return pl.BlockSpec((None, PAGE_SIZE, DH_A), imap)

    page_specs = [page_spec(t, pg) for t in range(MOBA_TOPK) for pg in range(ppb)]

    def seg_spec(sidx):
        return pl.BlockSpec((None, None, tq, DH_A), lambda bi, h, qi, pt, sel, sl: (sidx, bi, 0, h))

    grid_spec = pltpu.PrefetchScalarGridSpec(
        num_scalar_prefetch=3,
        grid=(b, H_A, tq),
        in_specs=page_specs + page_specs + [seg_spec(0), seg_spec(1), seg_spec(2), seg_spec(3)],
        out_specs=pl.BlockSpec((None, tq, DH_A), lambda bi, h, qi, pt, sel, sl: (bi, 0, h)),
    )
    n_pg = len(page_specs)
    return pl.pallas_call(
        functools.partial(_sample_attn_kernel, tq=tq, past_len=past_len, n_pages=n_pages),
        grid_spec=grid_spec,
        out_shape=jax.ShapeDtypeStruct((b, tq, w), F32),
        compiler_params=_cparams(("parallel", "parallel", "arbitrary")),
        name="sample_attn",
    )(pt_flat, sel_flat, slopes, *([cache_k3] * n_pg), *([cache_v3] * n_pg), z4s, z4s, z4s, z4s)


def _alibi_slopes(n_heads):
    return 2.0 ** (-8.0 * jnp.arange(1, n_heads + 1, dtype=F32) / n_heads)


def kernel(x_prompt, x_sample, cache_k, cache_v, state_pool, state_C, state_n, state_m, page_table,
           c_prompt, c_sample, ada_w, ada_b, norm_pre, norm_post, w_in_even, pool_w, pool_scale,
           w_out_even, w_in_odd, b_igate, b_fgate, head_norm, w_out_odd):
    bp, tp, d = x_prompt.shape
    bs, ts, _ = x_sample.shape
    n_pages = page_table.shape[1]
    past_len = n_pages * PAGE_SIZE
    w_a = H_A * DH_A
    w_c = H_C * DH_C
    rp, rs = bp * tp, bs * ts
    tm_p = 1024

    mods = _adaln(jnp.concatenate([c_prompt, c_sample], axis=0), ada_w, ada_b)

    def mod_parts(l):
        m = mods[l]
        shift, scale, gate = m[:, :d], m[:, d:2 * d], m[:, 2 * d:]
        prompt = tuple(a[:bp].reshape(bp, 1, d) for a in (shift, scale, gate))
        sample = tuple(jnp.repeat(a[bp:], ts, axis=0) for a in (shift, scale, gate))
        return prompt, sample

    slopes = _alibi_slopes(H_A)
    xp = x_prompt.reshape(rp, d)
    xs = x_sample.reshape(rs, d)

    (sh_p, sc_p, gt_p), (sh_s, sc_s, gt_s) = mod_parts(0)
    w_in0 = w_in_even[0].astype(BF16)
    zp, _ = _inproj(xp, norm_pre[0], sc_p, sh_p, w_in0, None, rows_per_group=tp, tm=tm_p)
    zs, _ = _inproj(xs, norm_pre[0], sc_s, sh_s, w_in0, None, rows_per_group=ts, tm=rs)
    z4p = zp.reshape(zp.shape[0], bp, tp, SEG)
    z4s = zs.reshape(zs.shape[0], bs, ts, SEG)

    pool_w_bf = pool_w[0].astype(BF16)
    att_p = _moba_prompt(z4p, slopes)
    pool_p, pstate_p = _pool(z4p, jnp.zeros((bp, POOL_HIST, SEG), F32), pool_w_bf, pool_scale[0],
                             pos0=0, tt=512, out_dtype=BF16)

    cache_k3 = cache_k[0].reshape(cache_k.shape[1], PAGE_SIZE, w_a)
    cache_v3 = cache_v[0].reshape(cache_v.shape[1], PAGE_SIZE, w_a)
    sel = _kmean_topk(cache_k3, page_table, z4s)
    sel_flat = sel[:, :, :MOBA_TOPK].reshape(bs, ts, H_A, MOBA_TOPK).reshape(-1)
    att_s = _sample_attn(cache_k3, cache_v3, page_table.reshape(-1), sel_flat, slopes, z4s,
                         past_len=past_len, n_pages=n_pages)
    pool_s, pstate_s = _pool(z4s, state_pool[0], pool_w_bf, pool_scale[0],
                             pos0=past_len, tt=ts, out_dtype=F32)

    w_out0 = w_out_even[0].astype(BF16)
    w_list0 = [w_out0[:w_a], w_out0[w_a:]]
    xp1 = _outproj([att_p.reshape(rp, w_a), pool_p.reshape(rp, SEG)], w_list0, xp, gt_p, norm_post[0],
                   rows_per_group=tp, tm=512)
    xs1 = _outproj([att_s.reshape(rs, w_a), pool_s.reshape(rs, SEG)], w_list0, xs, gt_s, norm_post[0],
                   rows_per_group=ts, tm=rs)

    (sh_p, sc_p, gt_p), (sh_s, sc_s, gt_s) = mod_parts(1)
    n_main = 5 * w_c
    w_in1 = w_in_odd[0][:, :n_main].astype(BF16)
    wg1 = jnp.pad(w_in_odd[0][:, n_main:], ((0, 0), (0, 128 - 2 * H_C))).astype(BF16)
    zp, gates_p = _inproj(xp1, norm_pre[1], sc_p, sh_p, w_in1, wg1, rows_per_group=tp, tm=tm_p)
    zs, gates_s = _inproj(xs1, norm_pre[1], sc_s, sh_s, w_in1, wg1, rows_per_group=ts, tm=rs)
    bias = jnp.stack([b_igate[0], b_fgate[0]]).astype(F32)
    hc_p, c_p, n_p, m_p = _mlstm(zp.reshape(zp.shape[0], bp, tp, SEG), gates_p.reshape(bp, tp, 128),
                                 bias, head_norm[0], None, chunk=128, lp=128)
    state = (state_C[0], state_n[0].reshape(bs, H_C, 1, DH_C), state_m[0])
    hc_s, c_s, n_s, m_s = _mlstm(zs.reshape(zs.shape[0], bs, ts, SEG), gates_s.reshape(bs, ts, 128),
                                 bias, head_norm[0], state, chunk=ts, lp=8)
    w_out1 = w_out_odd[0].astype(BF16)
    xp2 = _outproj([hc_p.reshape(rp, w_c)], [w_out1], xp1, gt_p, norm_post[1], rows_per_group=tp, tm=512)
    xs2 = _outproj([hc_s.reshape(rs, w_c)], [w_out1], xs1, gt_s, norm_post[1], rows_per_group=ts, tm=rs)

    kv_shape_p = (1, bp, tp, H_A, DH_A)
    kv_shape_s = (1, bs, ts, H_A, DH_A)
    return (xp2.reshape(bp, tp, d), xs2.reshape(bs, ts, d),
            z4p[1].reshape(kv_shape_p), z4p[2].reshape(kv_shape_p),
            z4s[1].reshape(kv_shape_s), z4s[2].reshape(kv_shape_s),
            pstate_p[None], pstate_s[None],
            c_p[None], n_p.reshape(1, bp, H_C, DH_C), m_p[:, :, 0, 0][None],
            c_s[None], n_s.reshape(1, bs, H_C, DH_C), m_s[:, :, 0, 0][None])
```

```python
import functools
import math

import jax
import jax.numpy as jnp
from jax import lax
from jax.experimental import pallas as pl
from jax.experimental.pallas import tpu as pltpu

F32 = jnp.float32
BF16 = jnp.bfloat16

H_A = 8
DH_A = 128
MOBA_BLOCK = 256
MOBA_TOPK = 3
POOL_WINDOWS = (2, 4, 8, 16)
GW_B = 256
POOL_HIST = max(POOL_WINDOWS) - 1
H_C = 8
DH_C = 256
EPS = 1e-6
PAGE_SIZE = 128

SEG = 1024
NEG_BIG = -1e30

VMEM_LIMIT_V7X = 52 * 1024 * 1024


def _silu(x):
    return x * (1.0 / (1.0 + jnp.exp(-x)))


def _sigmoid(x):
    return 1.0 / (1.0 + jnp.exp(-x))


def _log_sigmoid(x):
    return -(jnp.maximum(-x, 0.0) + jnp.log(1.0 + jnp.exp(-jnp.abs(x))))


def _cparams(sem):
    return pltpu.CompilerParams(dimension_semantics=sem, vmem_limit_bytes=VMEM_LIMIT_V7X)


def _adaln_kernel(c_ref, w_ref, b_ref, o_ref):
    s = _silu(c_ref[...]).astype(BF16)
    o_ref[...] = jnp.dot(s, w_ref[...].astype(BF16), preferred_element_type=F32) + b_ref[...]


def _adaln(c_all, ada_w, ada_b, tn=512):
    depth, d, n = ada_w.shape
    r = c_all.shape[0]
    return pl.pallas_call(
        _adaln_kernel,
        grid=(depth, n // tn),
        in_specs=[
            pl.BlockSpec((r, d), lambda l, j: (0, 0)),
            pl.BlockSpec((None, d, tn), lambda l, j: (l, 0, j)),
            pl.BlockSpec((None, 1, tn), lambda l, j: (l, 0, j)),
        ],
        out_specs=pl.BlockSpec((None, r, tn), lambda l, j: (l, 0, j)),
        out_shape=jax.ShapeDtypeStruct((depth, r, n), F32),
        compiler_params=_cparams(("parallel", "parallel")),
        name="adaln",
    )(c_all, ada_w, ada_b.reshape(depth, 1, n))


def _inproj_kernel(*refs, has_gate, kv_heads):
    kv_ref = gt_ref = wg_ref = None
    if has_gate:
        x_ref, g_ref, sc_ref, sh_ref, w_ref, wg_ref, z_ref, gt_ref, h_scr = refs
    elif kv_heads:
        x_ref, g_ref, sc_ref, sh_ref, w_ref, z_ref, kv_ref, h_scr = refs
    else:
        x_ref, g_ref, sc_ref, sh_ref, w_ref, z_ref, h_scr = refs
    j = pl.program_id(1)

    @pl.when(pl.program_id(1) == 0)
    def _():
        x = x_ref[...]
        r = x * lax.rsqrt(jnp.mean(x * x, axis=-1, keepdims=True) + EPS)
        h = (r * g_ref[...]) * (1.0 + sc_ref[...]) + sh_ref[...]
        h_scr[...] = h.astype(BF16)
        if has_gate:
            gt_ref[...] = jnp.dot(h_scr[...], wg_ref[...], preferred_element_type=F32)

    res = jnp.dot(h_scr[...], w_ref[...], preferred_element_type=F32)
    z_ref[...] = res

    if kv_heads:
        @pl.when((j == 1) | (j == 2))
        def _():
            dh = SEG // kv_heads
            for hh in range(kv_heads):
                kv_ref[:, hh, :] = res[:, hh * dh:(hh + 1) * dh]


def _inproj(x2, g_pre, scale, shift, w_bf, wg_bf, *, rows_per_group, tm, kv_heads=0):
    r, d = x2.shape
    n = w_bf.shape[1]
    nseg = n // SEG
    has_gate = wg_bf is not None
    if scale.ndim == 3:
        per = rows_per_group // tm
        mod_spec = pl.BlockSpec((None, 1, d), lambda i, j: (i // per, 0, 0))
    else:
        mod_spec = pl.BlockSpec((tm, d), lambda i, j: (i, 0))
    in_specs = [
        pl.BlockSpec((tm, d), lambda i, j: (i, 0), pipeline_mode=pl.Buffered(1)),
        pl.BlockSpec((1, d), lambda i, j: (0, 0)),
        mod_spec,
        mod_spec,
        pl.BlockSpec((d, SEG), lambda i, j: (0, j)),
    ]
    args = [x2, g_pre.reshape(1, d), scale, shift, w_bf]
    out_specs = [pl.BlockSpec((None, tm, SEG), lambda i, j: (j, i, 0))]
    out_shape = [jax.ShapeDtypeStruct((nseg, r, SEG), F32)]
    if has_gate:
        in_specs.append(pl.BlockSpec((d, 128), lambda i, j: (0, 0)))
        args.append(wg_bf)
        out_specs.append(pl.BlockSpec((tm, 128), lambda i, j: (i, 0)))
        out_shape.append(jax.ShapeDtypeStruct((r, 128), F32))
    if kv_heads:
        out_specs.append(pl.BlockSpec((None, tm, kv_heads, SEG // kv_heads),
                                      lambda i, j: (jnp.clip(j - 1, 0, 1), i, 0, 0)))
        out_shape.append(jax.ShapeDtypeStruct((2, r, kv_heads, SEG // kv_heads), F32))
    res = pl.pallas_call(
        functools.partial(_inproj_kernel, has_gate=has_gate, kv_heads=kv_heads),
        grid=(r // tm, nseg),
        in_specs=in_specs,
        out_specs=out_specs,
        out_shape=out_shape,
        scratch_shapes=[pltpu.VMEM((tm, d), BF16)],
        compiler_params=_cparams(("parallel", "arbitrary")),
        name="inproj_gate" if has_gate else "inproj",
    )(*args)
    return (res[0], res[1]) if (has_gate or kv_heads) else (res[0], None)


def _outproj_kernel(*refs, n_a):
    a_refs = refs[:n_a]
    w_refs = refs[n_a:2 * n_a]
    x_ref, gate_ref, gp_ref, o_ref = refs[2 * n_a:]
    y = None
    for a_ref, w_ref in zip(a_refs, w_refs):
        t = jnp.dot(a_ref[...].astype(BF16), w_ref[...], preferred_element_type=F32)
        y = t if y is None else y + t
    r = y * lax.rsqrt(jnp.mean(y * y, axis=-1, keepdims=True) + EPS)
    o_ref[...] = x_ref[...] + gate_ref[...] * (r * gp_ref[...])


def _outproj(a_list, w_list, x2, gate, g_post, *, rows_per_group, tm):
    r, d = x2.shape
    n_a = len(a_list)
    if gate.ndim == 3:
        per = rows_per_group // tm
        gate_spec = pl.BlockSpec((None, 1, d), lambda i: (i // per, 0, 0))
    else:
        gate_spec = pl.BlockSpec((tm, d), lambda i: (i, 0))
    in_specs = [pl.BlockSpec((tm, a.shape[1]), lambda i: (i, 0)) for a in a_list]
    in_specs += [pl.BlockSpec(w.shape, lambda i: (0, 0)) for w in w_list]
    in_specs += [pl.BlockSpec((tm, d), lambda i: (i, 0)), gate_spec, pl.BlockSpec((1, d), lambda i: (0, 0))]
    return pl.pallas_call(
        functools.partial(_outproj_kernel, n_a=n_a),
        grid=(r // tm,),
        in_specs=in_specs,
        out_specs=pl.BlockSpec((tm, d), lambda i: (i, 0)),
        out_shape=jax.ShapeDtypeStruct((r, d), F32),
        compiler_params=_cparams(("parallel",)),
        name="outproj",
    )(*a_list, *w_list, x2, gate, g_post.reshape(1, d))


def _moba_kernel(slopes_ref, q_ref, k_ref, v_ref, ga_ref, o_ref,
                 kb_scr, vb_scr, kmean_scr, m_scr, l_scr, acc_scr, *, nb):
    h = pl.program_id(1)
    i = pl.program_id(2)
    blk = MOBA_BLOCK

    @pl.when(i == 0)
    def _():
        kb_scr[...] = k_ref[...].astype(BF16)
        vb_scr[...] = v_ref[...].astype(BF16)
        for j in range(nb):
            kmean_scr[j:j + 1, :] = jnp.mean(k_ref[j * blk:(j + 1) * blk, :], axis=0, keepdims=True)

    q = q_ref[...]
    qb = q.astype(BF16)
    gate = lax.dot_general(qb, kmean_scr[...].astype(BF16), (((1,), (1,)), ((), ())),
                           preferred_element_type=F32)
    col = lax.broadcasted_iota(jnp.int32, (blk, nb), 1)
    past = col < i
    gm = jnp.where(past, gate, -jnp.inf)
    cnt = jnp.zeros((blk, nb), F32)
    for kk in range(nb):
        gk = gm[:, kk:kk + 1]
        beats = (gk > gm) | ((gk == gm) & (col > kk))
        cnt = cnt + jnp.where(beats, 1.0, 0.0)
    selown = jnp.where((past & (cnt < MOBA_TOPK)) | (col == i), 1.0, 0.0)

    m_scr[...] = jnp.full((blk, 1), -jnp.inf, F32)
    l_scr[...] = jnp.zeros((blk, 1), F32)
    acc_scr[...] = jnp.zeros((blk, DH_A), F32)
    slope = slopes_ref[h]
    scale = DH_A ** -0.5
    qpos = i * blk + lax.broadcasted_iota(jnp.int32, (blk, 1), 0)

    for j in range(nb):
        @pl.when(j <= i)
        def _(j=j):
            kj = kb_scr[j * blk:(j + 1) * blk, :]
            s = lax.dot_general(qb, kj, (((1,), (1,)), ((), ())), preferred_element_type=F32) * scale
            kpos = j * blk + lax.broadcasted_iota(jnp.int32, (1, blk), 1)
            dist = qpos - kpos
            s = s - slope * dist.astype(F32)
            valid = (dist >= 0) & (selown[:, j:j + 1] > 0.5)
            s = jnp.where(valid, s, -jnp.inf)
            m_prev = m_scr[...]
            m_new = jnp.maximum(m_prev, jnp.max(s, axis=-1, keepdims=True))
            m_safe = jnp.where(m_new == -jnp.inf, 0.0, m_new)
            alpha = jnp.exp(m_prev - m_safe)
            p = jnp.exp(s - m_safe)
            l_scr[...] = alpha * l_scr[...] + jnp.sum(p, axis=-1, keepdims=True)
            acc_scr[...] = alpha * acc_scr[...] + jnp.dot(
                p.astype(BF16), vb_scr[j * blk:(j + 1) * blk, :], preferred_element_type=F32)
            m_scr[...] = m_new

    o = acc_scr[...] / l_scr[...]
    o_ref[...] = (o * _silu(ga_ref[...])).astype(o_ref.dtype)


def _moba_prompt(z4, slopes):
    _, b, t, w = z4.shape
    nb = t // MOBA_BLOCK
    blk = MOBA_BLOCK
    grid_spec = pltpu.PrefetchScalarGridSpec(
        num_scalar_prefetch=1,
        grid=(b, H_A, nb),
        in_specs=[
            pl.BlockSpec((None, None, blk, DH_A), lambda bi, h, i, s: (0, bi, i, h)),
            pl.BlockSpec((None, None, t, DH_A), lambda bi, h, i, s: (1, bi, 0, h)),
            pl.BlockSpec((None, None, t, DH_A), lambda bi, h, i, s: (2, bi, 0, h)),
            pl.BlockSpec((None, None, blk, DH_A), lambda bi, h, i, s: (3, bi, i, h)),
        ],
        out_specs=pl.BlockSpec((None, blk, DH_A), lambda bi, h, i, s: (bi, i, h)),
        scratch_shapes=[
            pltpu.VMEM((t, DH_A), BF16),
            pltpu.VMEM((t, DH_A), BF16),
            pltpu.VMEM((nb, DH_A), F32),
            pltpu.VMEM((blk, 1), F32),
            pltpu.VMEM((blk, 1), F32),
            pltpu.VMEM((blk, DH_A), F32),
        ],
    )
    return pl.pallas_call(
        functools.partial(_moba_kernel, nb=nb),
        grid_spec=grid_spec,
        out_shape=jax.ShapeDtypeStruct((b, t, w), BF16),
        compiler_params=_cparams(("parallel", "parallel", "arbitrary")),
        name="moba_prompt",
    )(slopes, z4, z4, z4, z4)


def _pool_kernel(u_ref, gb_ref, prev_ref, pw_ref, ps_ref, o_ref, pn_ref, ext_scr, *, tt, pos0, nt):
    t = pl.program_id(1)
    hist = POOL_HIST + 1

    @pl.when(t == 0)
    def _():
        ext_scr[0:1, :] = jnp.zeros((1, ext_scr.shape[1]), F32)
        ext_scr[1:hist, :] = prev_ref[...]

    ext_scr[hist:hist + tt, :] = u_ref[...]
    pos = pos0 + t * tt + lax.broadcasted_iota(jnp.int32, (tt, 1), 0)
    for g, w in enumerate(POOL_WINDOWS):
        sl = slice(g * GW_B, (g + 1) * GW_B)
        x = u_ref[:, sl]
        acc = x
        for jj in range(1, w):
            acc = acc + ext_scr[hist - jj:hist - jj + tt, sl]
        cnt = jnp.minimum(w, pos + 1).astype(F32)
        d = acc / cnt - x
        y = jnp.dot(d.astype(BF16), pw_ref[g], preferred_element_type=F32) * ps_ref[:, sl]
        o_ref[:, sl] = (y * _silu(gb_ref[:, sl])).astype(o_ref.dtype)

    tail = ext_scr[tt:tt + hist, :]
    ext_scr[0:hist, :] = tail

    @pl.when(t == nt - 1)
    def _():
        pn_ref[...] = ext_scr[1:hist, :]


def _pool(z4, prev, pool_w_bf, pool_scale, *, pos0, tt, out_dtype):
    _, b, t, w = z4.shape
    nt = t // tt
    return pl.pallas_call(
        functools.partial(_pool_kernel, tt=tt, pos0=pos0, nt=nt),
        grid=(b, nt),
        in_specs=[
            pl.BlockSpec((None, None, tt, w), lambda bi, ti: (4, bi, ti, 0)),
            pl.BlockSpec((None, None, tt, w), lambda bi, ti: (5, bi, ti, 0)),
            pl.BlockSpec((None, POOL_HIST, w), lambda bi, ti: (bi, 0, 0)),
            pl.BlockSpec(pool_w_bf.shape, lambda bi, ti: (0, 0, 0)),
            pl.BlockSpec((1, w), lambda bi, ti: (0, 0)),
        ],
        out_specs=[
            pl.BlockSpec((None, tt, w), lambda bi, ti: (bi, ti, 0)),
            pl.BlockSpec((None, POOL_HIST, w), lambda bi, ti: (bi, 0, 0)),
        ],
        out_shape=[
            jax.ShapeDtypeStruct((b, t, w), out_dtype),
            jax.ShapeDtypeStruct((b, POOL_HIST, w), F32),
        ],
        scratch_shapes=[pltpu.VMEM((POOL_HIST + 1 + tt, w), F32)],
        compiler_params=_cparams(("parallel", "arbitrary")),
        name="pool",
    )(z4, z4, prev, pool_w_bf, pool_scale.reshape(1, w))


def _mlstm_kernel(*refs, lv, lp, zero_state):
    if zero_state:
        (bias_ref, q_ref, k_ref, v_ref, og_ref, g_ref, gt_ref, hn_ref,
         h_out, c_out, n_out, m_out, c_scr, n_scr, m_scr, pad_scr) = refs
    else:
        (bias_ref, m0_ref, q_ref, k_ref, v_ref, og_ref, g_ref, gt_ref, hn_ref, c0_ref, n0_ref,
         h_out, c_out, n_out, m_out, c_scr, n_scr, m_scr, pad_scr) = refs
    bi = pl.program_id(0)
    h = pl.program_id(1)
    c = pl.program_id(2)
    nc = pl.num_programs(2)

    @pl.when(c == 0)
    def _():
        if zero_state:
            c_scr[...] = jnp.zeros(c_scr.shape, F32)
            n_scr[...] = jnp.zeros(n_scr.shape, F32)
            m_scr[...] = jnp.zeros(m_scr.shape, F32)
        else:
            c_scr[...] = c0_ref[...]
            n_scr[...] = n0_ref[...]
            m_scr[...] = jnp.full(m_scr.shape, m0_ref[bi, h], F32)

    def load(ref, slot):
        if lv == lp:
            return ref[...]
        pad_scr[slot] = jnp.zeros(pad_scr.shape[1:], F32)
        pad_scr[slot, 0:lv, 0:ref.shape[1]] = ref[...]
        return pad_scr[slot][:, 0:ref.shape[1]]

    q = load(q_ref, 0)
    k = load(k_ref, 1) * (DH_C ** -0.5)
    v = load(v_ref, 2)
    og = load(og_ref, 3)
    g = load(g_ref, 4)
    gts = load(gt_ref, 5)

    lane = lax.broadcasted_iota(jnp.int32, gts.shape, 1)
    row1 = lax.broadcasted_iota(jnp.int32, (lp, 1), 0)
    ig_col = jnp.sum(jnp.where(lane == h, gts, 0.0), axis=1, keepdims=True) + bias_ref[0, h]
    gf_col = jnp.sum(jnp.where(lane == H_C + h, gts, 0.0), axis=1, keepdims=True) + bias_ref[1, h]
    lf_col = _log_sigmoid(gf_col)
    if lv != lp:
        ig_col = jnp.where(row1 < lv, ig_col, NEG_BIG)
        lf_col = jnp.where(row1 < lv, lf_col, 0.0)

    rr = lax.broadcasted_iota(jnp.int32, (lp, lp), 0)
    cc = lax.broadcasted_iota(jnp.int32, (lp, lp), 1)
    eye = rr == cc
    tril = rr >= cc

    def to_row(col):
        return jnp.sum(jnp.where(eye, col, 0.0), axis=0, keepdims=True)

    lf_row = to_row(lf_col)
    ig_row = to_row(ig_col)
    b_col = jnp.sum(jnp.where(tril, lf_row, 0.0), axis=1, keepdims=True)
    b_row = to_row(b_col)
    dmat = jnp.where(tril, b_col - b_row + ig_row, -jnp.inf)
    m_prev = m_scr[...]
    carry = b_col + m_prev
    mt = jnp.maximum(carry, jnp.max(dmat, axis=1, keepdims=True))

    qb = q.astype(BF16)
    kb = k.astype(BF16)
    vb = v.astype(BF16)
    s = lax.dot_general(qb, kb, (((1,), (1,)), ((), ())), preferred_element_type=F32) * jnp.exp(dmat - mt)
    inter = jnp.exp(carry - mt)
    cq = lax.dot_general(qb, c_scr[...].astype(BF16), (((1,), (1,)), ((), ())), preferred_element_type=F32)
    num = jnp.dot(s.astype(BF16), vb, preferred_element_type=F32) + inter * cq
    nq = jnp.sum(q * n_scr[...], axis=1, keepdims=True)
    den = jnp.sum(s, axis=1, keepdims=True) + inter * nq
    hc = num / jnp.maximum(jnp.abs(den), jnp.exp(-mt))

    m_new = mt[lp - 1:lp, :]
    b_last = b_col[lp - 1:lp, :]
    w_col = jnp.exp(b_last - b_col + ig_col - m_new)
    decay = jnp.exp(b_last + m_prev - m_new)
    vw = (v * w_col).astype(BF16)
    c_scr[...] = decay * c_scr[...] + lax.dot_general(
        vw, kb, (((0,), (0,)), ((), ())), preferred_element_type=F32)
    n_scr[...] = decay * n_scr[...] + jnp.sum(k * w_col, axis=0, keepdims=True)
    m_scr[...] = m_new

    hc = hc * _sigmoid(og)
    hc = hc * lax.rsqrt(jnp.mean(hc * hc, axis=-1, keepdims=True) + EPS) * hn_ref[...]
    res = (hc * _silu(g)).astype(h_out.dtype)
    h_out[...] = res if lv == lp else res[0:lv, :]

    @pl.when(c == nc - 1)
    def _():
        c_out[...] = c_scr[...]
        n_out[...] = n_scr[...]
        m_out[...] = jnp.broadcast_to(m_scr[...], m_out.shape)


def _mlstm(z4, gates3, bias, head_norm, state, *, chunk, lp):
    _, b, t, _ = z4.shape
    nc = t // chunk
    per = SEG // DH_C
    zero_state = state is None
    n_pref = 1 if zero_state else 2

    def seg_spec(base):
        return pl.BlockSpec((None, None, chunk, DH_C),
                            lambda bi, h, c, *_: (base + h // per, bi, c, h % per))

    in_specs = [seg_spec(0), seg_spec(2), seg_spec(4), seg_spec(6), seg_spec(8),
                pl.BlockSpec((None, chunk, 128), lambda bi, h, c, *_: (bi, c, 0)),
                pl.BlockSpec((1, DH_C), lambda bi, h, c, *_: (0, h))]
    args = [z4, z4, z4, z4, z4, gates3, head_norm.reshape(1, H_C * DH_C)]
    prefetch = [bias]
    if not zero_state:
        c0, n0, m0 = state
        prefetch.append(m0)
        in_specs += [pl.BlockSpec((None, None, DH_C, DH_C), lambda bi, h, c, *_: (bi, h, 0, 0)),
                     pl.BlockSpec((None, None, 1, DH_C), lambda bi, h, c, *_: (bi, h, 0, 0))]
        args += [c0, n0]
    grid_spec = pltpu.PrefetchScalarGridSpec(
        num_scalar_prefetch=n_pref,
        grid=(b, H_C, nc),
        in_specs=in_specs,
        out_specs=[
            pl.BlockSpec((None, chunk, DH_C), lambda bi, h, c, *_: (bi, c, h)),
            pl.BlockSpec((None, None, DH_C, DH_C), lambda bi, h, c, *_: (bi, h, 0, 0)),
            pl.BlockSpec((None, None, 1, DH_C), lambda bi, h, c, *_: (bi, h, 0, 0)),
            pl.BlockSpec((None, None, 1, 128), lambda bi, h, c, *_: (bi, h, 0, 0)),
        ],
        scratch_shapes=[
            pltpu.VMEM((DH_C, DH_C), F32),
            pltpu.VMEM((1, DH_C), F32),
            pltpu.VMEM((1, 1), F32),
            pltpu.VMEM((6, lp, DH_C), F32),
        ],
    )
    return pl.pallas_call(
        functools.partial(_mlstm_kernel, lv=chunk, lp=lp, zero_state=zero_state),
        grid_spec=grid_spec,
        out_shape=[
            jax.ShapeDtypeStruct((b, t, H_C * DH_C), BF16),
            jax.ShapeDtypeStruct((b, H_C, DH_C, DH_C), F32),
            jax.ShapeDtypeStruct((b, H_C, 1, DH_C), F32),
            jax.ShapeDtypeStruct((b, H_C, 1, 128), F32),
        ],
        compiler_params=_cparams(("parallel", "parallel", "arbitrary")),
        name="mlstm",
    )(*prefetch, *args)


PAGES_PER_STEP = 16


def _kmean_topk_kernel(pt_ref, *refs, n_steps, tq):
    page_refs = refs[:PAGES_PER_STEP]
    q_ref, sel_ref, kmean_scr = refs[PAGES_PER_STEP:]
    s = pl.program_id(1)
    ppb = MOBA_BLOCK // PAGE_SIZE
    for jj in range(PAGES_PER_STEP // ppb):
        tot = None
        for p in range(ppb):
            part = jnp.sum(page_refs[jj * ppb + p][...], axis=0, keepdims=True)
            tot = part if tot is None else tot + part
        kmean_scr[pl.ds(s * (PAGES_PER_STEP // ppb) + jj, 1), :] = tot * (1.0 / MOBA_BLOCK)

    @pl.when(s == n_steps - 1)
    def _():
        nbp = kmean_scr.shape[0]
        w = kmean_scr.shape[1]
        q = q_ref[...]
        rows = H_A * tq
        lane_head = lax.broadcasted_iota(jnp.int32, (H_A, w), 1) // DH_A
        row_head = lax.broadcasted_iota(jnp.int32, (H_A, w), 0)
        pieces = [jnp.where(lane_head == row_head, jnp.broadcast_to(q[t:t + 1, :], (H_A, w)), 0.0)
                  for t in range(tq)]
        qbd = jnp.concatenate(pieces, axis=0)
        gate = lax.dot_general(qbd.astype(BF16), kmean_scr[...].astype(BF16), (((1,), (1,)), ((), ())),
                               preferred_element_type=F32)
        lane = lax.broadcasted_iota(jnp.int32, (rows, nbp), 1)
        out_lane = lax.broadcasted_iota(jnp.int32, (rows, 128), 1)
        sel = jnp.zeros((rows, 128), jnp.int32)
        for t in range(MOBA_TOPK):
            mx = jnp.max(gate, axis=1, keepdims=True)
            idx = jnp.min(jnp.where(gate == mx, lane, nbp), axis=1, keepdims=True)
            sel = jnp.where(out_lane == t, idx, sel)
            gate = jnp.where(lane == idx, -jnp.inf, gate)
        sel_ref[...] = sel


def _kmean_topk(cache_k3, page_table, z4s):
    b, n_pages = page_table.shape
    _, _, tq, w = z4s.shape
    n_steps = n_pages // PAGES_PER_STEP
    nbp = n_pages * PAGE_SIZE // MOBA_BLOCK

    def page_spec(p):
        return pl.BlockSpec((None, PAGE_SIZE, w), lambda bi, s, pt: (pt[bi, s * PAGES_PER_STEP + p], 0, 0))

    grid_spec = pltpu.PrefetchScalarGridSpec(
        num_scalar_prefetch=1,
        grid=(b, n_steps),
        in_specs=[page_spec(p) for p in range(PAGES_PER_STEP)]
        + [pl.BlockSpec((None, None, tq, w), lambda bi, s, pt: (0, bi, 0, 0))],
        out_specs=pl.BlockSpec((None, H_A * tq, 128), lambda bi, s, pt: (bi, 0, 0)),
        scratch_shapes=[pltpu.VMEM((nbp, w), F32)],
    )
    return pl.pallas_call(
        functools.partial(_kmean_topk_kernel, n_steps=n_steps, tq=tq),
        grid_spec=grid_spec,
        out_shape=jax.ShapeDtypeStruct((b, H_A * tq, 128), jnp.int32),
        compiler_params=_cparams(("parallel", "arbitrary")),
        name="kmean_topk",
    )(page_table, *([cache_k3] * PAGES_PER_STEP), z4s)


def _sample_attn_kernel(pt_ref, sel_ref, slopes_ref, *refs, tq, past_len, n_pages):
    ppb = MOBA_BLOCK // PAGE_SIZE
    n_pg = MOBA_TOPK * ppb
    k_refs = refs[:n_pg]
    v_refs = refs[n_pg:2 * n_pg]
    q_ref, kn_ref, vn_ref, ga_ref, o_ref = refs[2 * n_pg:]
    bi = pl.program_id(0)
    h = pl.program_id(1)
    qi = pl.program_id(2)
    slope = slopes_ref[h]
    scale = DH_A ** -0.5
    qpos = past_len + qi

    qrow = q_ref[pl.ds(qi, 1), :]
    q8 = jnp.broadcast_to(qrow, (8, DH_A)).astype(BF16)
    lane = lax.broadcasted_iota(jnp.int32, (1, PAGE_SIZE), 1)

    s_list = []
    for t in range(MOBA_TOPK):
        blk = sel_ref[((bi * tq + qi) * H_A + h) * MOBA_TOPK + t]
        for pg in range(ppb):
            kp = k_refs[t * ppb + pg][...].astype(BF16)
            s = lax.dot_general(q8, kp, (((1,), (1,)), ((), ())), preferred_element_type=F32)[0:1, :] * scale
            kpos = blk * MOBA_BLOCK + pg * PAGE_SIZE + lane
            s_list.append(s - slope * (qpos - kpos).astype(F32))
    trow = lax.broadcasted_iota(jnp.int32, (tq, 1), 0)
    s_own = jnp.sum(kn_ref[...] * qrow, axis=1, keepdims=True) * scale
    s_own = s_own - slope * (qi - trow).astype(F32)
    s_own = jnp.where(trow <= qi, s_own, -jnp.inf)

    m = jnp.max(s_own, axis=0, keepdims=True)
    for s in s_list:
        m = jnp.maximum(m, jnp.max(s, axis=1, keepdims=True))
    p_own = jnp.exp(s_own - m)
    l = jnp.sum(p_own, axis=0, keepdims=True)
    acc = jnp.sum(p_own * vn_ref[...], axis=0, keepdims=True)
    for idx, s in enumerate(s_list):
        p = jnp.exp(s - m)
        l = l + jnp.sum(p, axis=1, keepdims=True)
        p8 = jnp.broadcast_to(p, (8, PAGE_SIZE)).astype(BF16)
        acc = acc + jnp.dot(p8, v_refs[idx][...].astype(BF16), preferred_element_type=F32)[0:1, :]
    o = acc / l
    o_ref[pl.ds(qi, 1), :] = o * _silu(ga_ref[pl.ds(qi, 1), :])


def _sample_attn(cache_k3, cache_v3, pt_flat, sel_flat, slopes, z4s, *, past_len, n_pages):
    _, b, tq, w = z4s.shape
    ppb = MOBA_BLOCK // PAGE_SIZE

    def page_spec(t, pg):
        def imap(bi, h, qi, pt, sel, sl):
            blk = sel[((bi * tq + qi) * H_A + h) * MOBA_TOPK + t]
            return (pt[bi * n_pages + blk * ppb + pg], 0, h)
        return pl.BlockSpec((None, PAGE_SIZE, DH_A), imap)

    page_specs = [page_spec(t, pg) for t in range(MOBA_TOPK) for pg in range(ppb)]

    def seg_spec(sidx):
        return pl.BlockSpec((None, None, tq, DH_A), lambda bi, h, qi, pt, sel, sl: (sidx, bi, 0, h))

    grid_spec = pltpu.PrefetchScalarGridSpec(
        num_scalar_prefetch=3,
        grid=(b, H_A, tq),
        in_specs=page_specs + page_specs + [seg_spec(0), seg_spec(1), seg_spec(2), seg_spec(3)],
        out_specs=pl.BlockSpec((None, tq, DH_A), lambda bi, h, qi, pt, sel, sl: (bi, 0, h)),
    )
    n_pg = len(page_specs)
    return pl.pallas_call(
        functools.partial(_sample_attn_kernel, tq=tq, past_len=past_len, n_pages=n_pages),
        grid_spec=grid_spec,
        out_shape=jax.ShapeDtypeStruct((b, tq, w), F32),
        compiler_params=_cparams(("parallel", "parallel", "arbitrary")),
        name="sample_attn",
    )(pt_flat, sel_flat, slopes, *([cache_k3] * n_pg), *([cache_v3] * n_pg), z4s, z4s, z4s, z4s)


def _alibi_slopes(n_heads):
    return 2.0 ** (-8.0 * jnp.arange(1, n_heads + 1, dtype=F32) / n_heads)


def kernel(x_prompt, x_sample, cache_k, cache_v, state_pool, state_C, state_n, state_m, page_table,
           c_prompt, c_sample, ada_w, ada_b, norm_pre, norm_post, w_in_even, pool_w, pool_scale,
           w_out_even, w_in_odd, b_igate, b_fgate, head_norm, w_out_odd):
    bp, tp, d = x_prompt.shape
    bs, ts, _ = x_sample.shape
    n_pages = page_table.shape[1]
    past_len = n_pages * PAGE_SIZE
    w_a = H_A * DH_A
    w_c = H_C * DH_C
    rp, rs = bp * tp, bs * ts
    tm_p = 1024

    mods = _adaln(jnp.concatenate([c_prompt, c_sample], axis=0), ada_w, ada_b)

    def mod_parts(l):
        m = mods[l]
        shift, scale, gate = m[:, :d], m[:, d:2 * d], m[:, 2 * d:]
        prompt = tuple(a[:bp].reshape(bp, 1, d) for a in (shift, scale, gate))
        sample = tuple(jnp.repeat(a[bp:], ts, axis=0) for a in (shift, scale, gate))
        return prompt, sample

    slopes = _alibi_slopes(H_A)
    xp = x_prompt.reshape(rp, d)
    xs = x_sample.reshape(rs, d)

    (sh_p, sc_p, gt_p), (sh_s, sc_s, gt_s) = mod_parts(0)
    w_in0 = w_in_even[0].astype(BF16)
    zp, kv_p = _inproj(xp, norm_pre[0], sc_p, sh_p, w_in0, None, rows_per_group=tp, tm=tm_p, kv_heads=H_A)
    zs, kv_s = _inproj(xs, norm_pre[0], sc_s, sh_s, w_in0, None, rows_per_group=ts, tm=rs, kv_heads=H_A)
    z4p = zp.reshape(zp.shape[0], bp, tp, SEG)
    z4s = zs.reshape(zs.shape[0], bs, ts, SEG)

    pool_w_bf = pool_w[0].astype(BF16)
    att_p = _moba_prompt(z4p, slopes)
    pool_p, pstate_p = _pool(z4p, jnp.zeros((bp, POOL_HIST, SEG), F32), pool_w_bf, pool_scale[0],
                             pos0=0, tt=512, out_dtype=BF16)

    cache_k3 = cache_k[0].reshape(cache_k.shape[1], PAGE_SIZE, w_a)
    cache_v3 = cache_v[0].reshape(cache_v.shape[1], PAGE_SIZE, w_a)
    sel = _kmean_topk(cache_k3, page_table, z4s)
    sel_flat = sel[:, :, :MOBA_TOPK].reshape(bs, ts, H_A, MOBA_TOPK).reshape(-1)
    att_s = _sample_attn(cache_k3, cache_v3, page_table.reshape(-1), sel_flat, slopes, z4s,
                         past_len=past_len, n_pages=n_pages)
    pool_s, pstate_s = _pool(z4s, state_pool[0], pool_w_bf, pool_scale[0],
                             pos0=past_len, tt=ts, out_dtype=F32)

    w_out0 = w_out_even[0].astype(BF16)
    w_list0 = [w_out0[:w_a], w_out0[w_a:]]
    xp1 = _outproj([att_p.reshape(rp, w_a), pool_p.reshape(rp, SEG)], w_list0, xp, gt_p, norm_post[0],
                   rows_per_group=tp, tm=512)
    xs1 = _outproj([att_s.reshape(rs, w_a), pool_s.reshape(rs, SEG)], w_list0, xs, gt_s, norm_post[0],
                   rows_per_group=ts, tm=rs)

    (sh_p, sc_p, gt_p), (sh_s, sc_s, gt_s) = mod_parts(1)
    n_main = 5 * w_c
    w_in1 = w_in_odd[0][:, :n_main].astype(BF16)
    wg1 = jnp.pad(w_in_odd[0][:, n_main:], ((0, 0), (0, 128 - 2 * H_C))).astype(BF16)
    zp, gates_p = _inproj(xp1, norm_pre[1], sc_p, sh_p, w_in1, wg1, rows_per_group=tp, tm=tm_p)
    zs, gates_s = _inproj(xs1, norm_pre[1], sc_s, sh_s, w_in1, wg1, rows_per_group=ts, tm=rs)
    bias = jnp.stack([b_igate[0], b_fgate[0]]).astype(F32)
    hc_p, c_p, n_p, m_p = _mlstm(zp.reshape(zp.shape[0], bp, tp, SEG), gates_p.reshape(bp, tp, 128),
                                 bias, head_norm[0], None, chunk=128, lp=128)
    state = (state_C[0], state_n[0].reshape(bs, H_C, 1, DH_C), state_m[0])
    hc_s, c_s, n_s, m_s = _mlstm(zs.reshape(zs.shape[0], bs, ts, SEG), gates_s.reshape(bs, ts, 128),
                                 bias, head_norm[0], state, chunk=ts, lp=8)
    w_out1 = w_out_odd[0].astype(BF16)
    xp2 = _outproj([hc_p.reshape(rp, w_c)], [w_out1], xp1, gt_p, norm_post[1], rows_per_group=tp, tm=512)
    xs2 = _outproj([hc_s.reshape(rs, w_c)], [w_out1], xs1, gt_s, norm_post[1], rows_per_group=ts, tm=rs)

    kv_shape_p = (1, bp, tp, H_A, DH_A)
    kv_shape_s = (1, bs, ts, H_A, DH_A)
    return (xp2.reshape(bp, tp, d), xs2.reshape(bs, ts, d),
            kv_p[0].reshape(kv_shape_p), kv_p[1].reshape(kv_shape_p),
            kv_s[0].reshape(kv_shape_s), kv_s[1].reshape(kv_shape_s),
            pstate_p[None], pstate_s[None],
            c_p[None], n_p.reshape(1, bp, H_C, DH_C), m_p[:, :, 0, 0][None],
            c_s[None], n_s.reshape(1, bs, H_C, DH_C), m_s[:, :, 0, 0][None])
```

```python
import functools
import math

import jax
import jax.numpy as jnp
from jax import lax
from jax.experimental import pallas as pl
from jax.experimental.pallas import tpu as pltpu

F32 = jnp.float32
BF16 = jnp.bfloat16

H_A = 8
DH_A = 128
MOBA_BLOCK = 256
MOBA_TOPK = 3
POOL_WINDOWS = (2, 4, 8, 16)
GW_B = 256
POOL_HIST = max(POOL_WINDOWS) - 1
H_C = 8
DH_C = 256
EPS = 1e-6
PAGE_SIZE = 128

SEG = 1024
NEG_BIG = -1e30

VMEM_LIMIT_V7X = 52 * 1024 * 1024


def _silu(x):
    return x * (1.0 / (1.0 + jnp.exp(-x)))


def _sigmoid(x):
    return 1.0 / (1.0 + jnp.exp(-x))


def _log_sigmoid(x):
    return -(jnp.maximum(-x, 0.0) + jnp.log(1.0 + jnp.exp(-jnp.abs(x))))


def _cparams(sem):
    return pltpu.CompilerParams(dimension_semantics=sem, vmem_limit_bytes=VMEM_LIMIT_V7X)


def _adaln_kernel(c_ref, w_ref, b_ref, o_ref):
    s = _silu(c_ref[...]).astype(BF16)
    o_ref[...] = jnp.dot(s, w_ref[...].astype(BF16), preferred_element_type=F32) + b_ref[...]


def _adaln(c_all, ada_w, ada_b, tn=512):
    depth, d, n = ada_w.shape
    r = c_all.shape[0]
    return pl.pallas_call(
        _adaln_kernel,
        grid=(depth, n // tn),
        in_specs=[
            pl.BlockSpec((r, d), lambda l, j: (0, 0)),
            pl.BlockSpec((None, d, tn), lambda l, j: (l, 0, j)),
            pl.BlockSpec((None, 1, tn), lambda l, j: (l, 0, j)),
        ],
        out_specs=pl.BlockSpec((None, r, tn), lambda l, j: (l, 0, j)),
        out_shape=jax.ShapeDtypeStruct((depth, r, n), F32),
        compiler_params=_cparams(("parallel", "parallel")),
        name="adaln",
    )(c_all, ada_w, ada_b.reshape(depth, 1, n))


def _inproj_kernel(*refs, has_gate, hm):
    hm_ref = gt_ref = wg_ref = None
    if has_gate:
        x_ref, g_ref, sc_ref, sh_ref, w_ref, wg_ref, z_ref, gt_ref, h_scr = refs
    elif hm:
        x_ref, g_ref, sc_ref, sh_ref, w_ref, z_ref, hm_ref, h_scr = refs
    else:
        x_ref, g_ref, sc_ref, sh_ref, w_ref, z_ref, h_scr = refs
    j = pl.program_id(1)

    @pl.when(pl.program_id(1) == 0)
    def _():
        x = x_ref[...]
        r = x * lax.rsqrt(jnp.mean(x * x, axis=-1, keepdims=True) + EPS)
        h = (r * g_ref[...]) * (1.0 + sc_ref[...]) + sh_ref[...]
        h_scr[...] = h.astype(BF16)
        if has_gate:
            gt_ref[...] = jnp.dot(h_scr[...], wg_ref[...], preferred_element_type=F32)

    res = jnp.dot(h_scr[...], w_ref[...], preferred_element_type=F32)
    z_ref[...] = res

    if hm:
        first, count, heads = hm

        @pl.when((j >= first) & (j < first + count))
        def _():
            dh = SEG // heads
            for hh in range(heads):
                hm_ref[:, hh, :] = res[:, hh * dh:(hh + 1) * dh]


def _inproj(x2, g_pre, scale, shift, w_bf, wg_bf, *, rows_per_group, tm, hm=None):
    r, d = x2.shape
    n = w_bf.shape[1]
    nseg = n // SEG
    has_gate = wg_bf is not None
    if scale.ndim == 3:
        per = rows_per_group // tm
        mod_spec = pl.BlockSpec((None, 1, d), lambda i, j: (i // per, 0, 0))
    else:
        mod_spec = pl.BlockSpec((tm, d), lambda i, j: (i, 0))
    in_specs = [
        pl.BlockSpec((tm, d), lambda i, j: (i, 0), pipeline_mode=pl.Buffered(1)),
        pl.BlockSpec((1, d), lambda i, j: (0, 0)),
        mod_spec,
        mod_spec,
        pl.BlockSpec((d, SEG), lambda i, j: (0, j)),
    ]
    args = [x2, g_pre.reshape(1, d), scale, shift, w_bf]
    out_specs = [pl.BlockSpec((None, tm, SEG), lambda i, j: (j, i, 0))]
    out_shape = [jax.ShapeDtypeStruct((nseg, r, SEG), F32)]
    if has_gate:
        in_specs.append(pl.BlockSpec((d, 128), lambda i, j: (0, 0)))
        args.append(wg_bf)
        out_specs.append(pl.BlockSpec((tm, 128), lambda i, j: (i, 0)))
        out_shape.append(jax.ShapeDtypeStruct((r, 128), F32))
    if hm:
        first, count, heads = hm
        out_specs.append(pl.BlockSpec((None, tm, heads, SEG // heads),
                                      lambda i, j: (jnp.clip(j - first, 0, count - 1), i, 0, 0)))
        out_shape.append(jax.ShapeDtypeStruct((count, r, heads, SEG // heads), F32))
    res = pl.pallas_call(
        functools.partial(_inproj_kernel, has_gate=has_gate, hm=hm),
        grid=(r // tm, nseg),
        in_specs=in_specs,
        out_specs=out_specs,
        out_shape=out_shape,
        scratch_shapes=[pltpu.VMEM((tm, d), BF16)],
        compiler_params=_cparams(("parallel", "arbitrary")),
        name="inproj_gate" if has_gate else "inproj",
    )(*args)
    return (res[0], res[1]) if (has_gate or hm) else (res[0], None)


def _outproj_kernel(*refs, n_a):
    a_refs = refs[:n_a]
    w_refs = refs[n_a:2 * n_a]
    x_ref, gate_ref, gp_ref, o_ref = refs[2 * n_a:]
    y = None
    for a_ref, w_ref in zip(a_refs, w_refs):
        t = jnp.dot(a_ref[...].astype(BF16), w_ref[...], preferred_element_type=F32)
        y = t if y is None else y + t
    r = y * lax.rsqrt(jnp.mean(y * y, axis=-1, keepdims=True) + EPS)
    o_ref[...] = x_ref[...] + gate_ref[...] * (r * gp_ref[...])


def _outproj(a_list, w_list, x2, gate, g_post, *, rows_per_group, tm):
    r, d = x2.shape
    n_a = len(a_list)
    if gate.ndim == 3:
        per = rows_per_group // tm
        gate_spec = pl.BlockSpec((None, 1, d), lambda i: (i // per, 0, 0))
    else:
        gate_spec = pl.BlockSpec((tm, d), lambda i: (i, 0))
    in_specs = [pl.BlockSpec((tm, a.shape[1]), lambda i: (i, 0)) for a in a_list]
    in_specs += [pl.BlockSpec(w.shape, lambda i: (0, 0)) for w in w_list]
    in_specs += [pl.BlockSpec((tm, d), lambda i: (i, 0)), gate_spec, pl.BlockSpec((1, d), lambda i: (0, 0))]
    return pl.pallas_call(
        functools.partial(_outproj_kernel, n_a=n_a),
        grid=(r // tm,),
        in_specs=in_specs,
        out_specs=pl.BlockSpec((tm, d), lambda i: (i, 0)),
        out_shape=jax.ShapeDtypeStruct((r, d), F32),
        compiler_params=_cparams(("parallel",)),
        name="outproj",
    )(*a_list, *w_list, x2, gate, g_post.reshape(1, d))


def _moba_kernel(slopes_ref, q_ref, k_ref, v_ref, ga_ref, o_ref,
                 kb_scr, vb_scr, kmean_scr, m_scr, l_scr, acc_scr, *, nb):
    h = pl.program_id(1)
    i = pl.program_id(2)
    blk = MOBA_BLOCK

    @pl.when(i == 0)
    def _():
        kb_scr[...] = k_ref[...].astype(BF16)
        vb_scr[...] = v_ref[...].astype(BF16)
        for j in range(nb):
            kmean_scr[j:j + 1, :] = jnp.mean(k_ref[j * blk:(j + 1) * blk, :], axis=0, keepdims=True)

    q = q_ref[...]
    qb = q.astype(BF16)
    gate = lax.dot_general(qb, kmean_scr[...].astype(BF16), (((1,), (1,)), ((), ())),
                           preferred_element_type=F32)
    col = lax.broadcasted_iota(jnp.int32, (blk, nb), 1)
    past = col < i
    gm = jnp.where(past, gate, -jnp.inf)
    cnt = jnp.zeros((blk, nb), F32)
    for kk in range(nb):
        gk = gm[:, kk:kk + 1]
        beats = (gk > gm) | ((gk == gm) & (col > kk))
        cnt = cnt + jnp.where(beats, 1.0, 0.0)
    selown = jnp.where((past & (cnt < MOBA_TOPK)) | (col == i), 1.0, 0.0)

    m_scr[...] = jnp.full((blk, 1), -jnp.inf, F32)
    l_scr[...] = jnp.zeros((blk, 1), F32)
    acc_scr[...] = jnp.zeros((blk, DH_A), F32)
    slope = slopes_ref[h]
    scale = DH_A ** -0.5
    qpos = i * blk + lax.broadcasted_iota(jnp.int32, (blk, 1), 0)

    for j in range(nb):
        @pl.when(j <= i)
        def _(j=j):
            kj = kb_scr[j * blk:(j + 1) * blk, :]
            s = lax.dot_general(qb, kj, (((1,), (1,)), ((), ())), preferred_element_type=F32) * scale
            kpos = j * blk + lax.broadcasted_iota(jnp.int32, (1, blk), 1)
            dist = qpos - kpos
            s = s - slope * dist.astype(F32)
            valid = (dist >= 0) & (selown[:, j:j + 1] > 0.5)
            s = jnp.where(valid, s, -jnp.inf)
            m_prev = m_scr[...]
            m_new = jnp.maximum(m_prev, jnp.max(s, axis=-1, keepdims=True))
            m_safe = jnp.where(m_new == -jnp.inf, 0.0, m_new)
            alpha = jnp.exp(m_prev - m_safe)
            p = jnp.exp(s - m_safe)
            l_scr[...] = alpha * l_scr[...] + jnp.sum(p, axis=-1, keepdims=True)
            acc_scr[...] = alpha * acc_scr[...] + jnp.dot(
                p.astype(BF16), vb_scr[j * blk:(j + 1) * blk, :], preferred_element_type=F32)
            m_scr[...] = m_new

    o = acc_scr[...] / l_scr[...]
    o_ref[...] = (o * _silu(ga_ref[...])).astype(o_ref.dtype)


def _moba_prompt(z4, slopes):
    _, b, t, w = z4.shape
    nb = t // MOBA_BLOCK
    blk = MOBA_BLOCK
    grid_spec = pltpu.PrefetchScalarGridSpec(
        num_scalar_prefetch=1,
        grid=(b, H_A, nb),
        in_specs=[
            pl.BlockSpec((None, None, blk, DH_A), lambda bi, h, i, s: (0, bi, i, h)),
            pl.BlockSpec((None, None, t, DH_A), lambda bi, h, i, s: (1, bi, 0, h)),
            pl.BlockSpec((None, None, t, DH_A), lambda bi, h, i, s: (2, bi, 0, h)),
            pl.BlockSpec((None, None, blk, DH_A), lambda bi, h, i, s: (3, bi, i, h)),
        ],
        out_specs=pl.BlockSpec((None, blk, DH_A), lambda bi, h, i, s: (bi, i, h)),
        scratch_shapes=[
            pltpu.VMEM((t, DH_A), BF16),
            pltpu.VMEM((t, DH_A), BF16),
            pltpu.VMEM((nb, DH_A), F32),
            pltpu.VMEM((blk, 1), F32),
            pltpu.VMEM((blk, 1), F32),
            pltpu.VMEM((blk, DH_A), F32),
        ],
    )
    return pl.pallas_call(
        functools.partial(_moba_kernel, nb=nb),
        grid_spec=grid_spec,
        out_shape=jax.ShapeDtypeStruct((b, t, w), BF16),
        compiler_params=_cparams(("parallel", "parallel", "arbitrary")),
        name="moba_prompt",
    )(slopes, z4, z4, z4, z4)


def _pool_kernel(u_ref, gb_ref, prev_ref, pw_ref, ps_ref, o_ref, pn_ref, ext_scr, *, tt, pos0, nt):
    t = pl.program_id(1)
    hist = POOL_HIST + 1

    @pl.when(t == 0)
    def _():
        ext_scr[0:1, :] = jnp.zeros((1, ext_scr.shape[1]), F32)
        ext_scr[1:hist, :] = prev_ref[...]

    ext_scr[hist:hist + tt, :] = u_ref[...]
    pos = pos0 + t * tt + lax.broadcasted_iota(jnp.int32, (tt, 1), 0)
    for g, w in enumerate(POOL_WINDOWS):
        sl = slice(g * GW_B, (g + 1) * GW_B)
        x = u_ref[:, sl]
        acc = x
        for jj in range(1, w):
            acc = acc + ext_scr[hist - jj:hist - jj + tt, sl]
        cnt = jnp.minimum(w, pos + 1).astype(F32)
        d = acc / cnt - x
        y = jnp.dot(d.astype(BF16), pw_ref[g], preferred_element_type=F32) * ps_ref[:, sl]
        o_ref[:, sl] = (y * _silu(gb_ref[:, sl])).astype(o_ref.dtype)

    tail = ext_scr[tt:tt + hist, :]
    ext_scr[0:hist, :] = tail

    @pl.when(t == nt - 1)
    def _():
        pn_ref[...] = ext_scr[1:hist, :]


def _pool(z4, prev, pool_w_bf, pool_scale, *, pos0, tt, out_dtype):
    _, b, t, w = z4.shape
    nt = t // tt
    return pl.pallas_call(
        functools.partial(_pool_kernel, tt=tt, pos0=pos0, nt=nt),
        grid=(b, nt),
        in_specs=[
            pl.BlockSpec((None, None, tt, w), lambda bi, ti: (4, bi, ti, 0)),
            pl.BlockSpec((None, None, tt, w), lambda bi, ti: (5, bi, ti, 0)),
            pl.BlockSpec((None, POOL_HIST, w), lambda bi, ti: (bi, 0, 0)),
            pl.BlockSpec(pool_w_bf.shape, lambda bi, ti: (0, 0, 0)),
            pl.BlockSpec((1, w), lambda bi, ti: (0, 0)),
        ],
        out_specs=[
            pl.BlockSpec((None, tt, w), lambda bi, ti: (bi, ti, 0)),
            pl.BlockSpec((None, POOL_HIST, w), lambda bi, ti: (bi, 0, 0)),
        ],
        out_shape=[
            jax.ShapeDtypeStruct((b, t, w), out_dtype),
            jax.ShapeDtypeStruct((b, POOL_HIST, w), F32),
        ],
        scratch_shapes=[pltpu.VMEM((POOL_HIST + 1 + tt, w), F32)],
        compiler_params=_cparams(("parallel", "arbitrary")),
        name="pool",
    )(z4, z4, prev, pool_w_bf, pool_scale.reshape(1, w))


def _mlstm_kernel(*refs, lv, lp, zero_state):
    if zero_state:
        (bias_ref, q_ref, k_ref, v_ref, og_ref, g_ref, gt_ref, hn_ref,
         h_out, c_out, n_out, m_out, c_scr, n_scr, m_scr, pad_scr) = refs
    else:
        (bias_ref, m0_ref, q_ref, k_ref, v_ref, og_ref, g_ref, gt_ref, hn_ref, c0_ref, n0_ref,
         h_out, c_out, n_out, m_out, c_scr, n_scr, m_scr, pad_scr) = refs
    bi = pl.program_id(0)
    h = pl.program_id(1)
    c = pl.program_id(2)
    nc = pl.num_programs(2)

    @pl.when(c == 0)
    def _():
        if zero_state:
            c_scr[...] = jnp.zeros(c_scr.shape, F32)
            n_scr[...] = jnp.zeros(n_scr.shape, F32)
            m_scr[...] = jnp.zeros(m_scr.shape, F32)
        else:
            c_scr[...] = c0_ref[...]
            n_scr[...] = n0_ref[...]
            m_scr[...] = jnp.full(m_scr.shape, m0_ref[bi, h], F32)

    def load(ref, slot):
        if lv == lp:
            return ref[...]
        pad_scr[slot] = jnp.zeros(pad_scr.shape[1:], F32)
        pad_scr[slot, 0:lv, 0:ref.shape[1]] = ref[...]
        return pad_scr[slot][:, 0:ref.shape[1]]

    q = load(q_ref, 0)
    k = load(k_ref, 1) * (DH_C ** -0.5)
    v = load(v_ref, 2)
    og = load(og_ref, 3)
    g = load(g_ref, 4)
    gts = load(gt_ref, 5)

    lane = lax.broadcasted_iota(jnp.int32, gts.shape, 1)
    row1 = lax.broadcasted_iota(jnp.int32, (lp, 1), 0)
    ig_col = jnp.sum(jnp.where(lane == h, gts, 0.0), axis=1, keepdims=True) + bias_ref[0, h]
    gf_col = jnp.sum(jnp.where(lane == H_C + h, gts, 0.0), axis=1, keepdims=True) + bias_ref[1, h]
    lf_col = _log_sigmoid(gf_col)
    if lv != lp:
        ig_col = jnp.where(row1 < lv, ig_col, NEG_BIG)
        lf_col = jnp.where(row1 < lv, lf_col, 0.0)

    rr = lax.broadcasted_iota(jnp.int32, (lp, lp), 0)
    cc = lax.broadcasted_iota(jnp.int32, (lp, lp), 1)
    eye = rr == cc
    tril = rr >= cc

    def to_row(col):
        return jnp.sum(jnp.where(eye, col, 0.0), axis=0, keepdims=True)

    lf_row = to_row(lf_col)
    ig_row = to_row(ig_col)
    b_col = jnp.sum(jnp.where(tril, lf_row, 0.0), axis=1, keepdims=True)
    b_row = to_row(b_col)
    dmat = jnp.where(tril, b_col - b_row + ig_row, -jnp.inf)
    m_prev = m_scr[...]
    carry = b_col + m_prev
    mt = jnp.maximum(carry, jnp.max(dmat, axis=1, keepdims=True))

    qb = q.astype(BF16)
    kb = k.astype(BF16)
    vb = v.astype(BF16)
    s = lax.dot_general(qb, kb, (((1,), (1,)), ((), ())), preferred_element_type=F32) * jnp.exp(dmat - mt)
    inter = jnp.exp(carry - mt)
    cq = lax.dot_general(qb, c_scr[...].astype(BF16), (((1,), (1,)), ((), ())), preferred_element_type=F32)
    num = jnp.dot(s.astype(BF16), vb, preferred_element_type=F32) + inter * cq
    nq = jnp.sum(q * n_scr[...], axis=1, keepdims=True)
    den = jnp.sum(s, axis=1, keepdims=True) + inter * nq
    hc = num / jnp.maximum(jnp.abs(den), jnp.exp(-mt))

    m_new = mt[lp - 1:lp, :]
    b_last = b_col[lp - 1:lp, :]
    w_col = jnp.exp(b_last - b_col + ig_col - m_new)
    decay = jnp.exp(b_last + m_prev - m_new)
    vw = (v * w_col).astype(BF16)
    c_scr[...] = decay * c_scr[...] + lax.dot_general(
        vw, kb, (((0,), (0,)), ((), ())), preferred_element_type=F32)
    n_scr[...] = decay * n_scr[...] + jnp.sum(k * w_col, axis=0, keepdims=True)
    m_scr[...] = m_new

    hc = hc * _sigmoid(og)
    hc = hc * lax.rsqrt(jnp.mean(hc * hc, axis=-1, keepdims=True) + EPS) * hn_ref[...]
    res = (hc * _silu(g)).astype(h_out.dtype)
    h_out[...] = res if lv == lp else res[0:lv, :]

    @pl.when(c == nc - 1)
    def _():
        c_out[...] = c_scr[...]
        n_out[...] = n_scr[...]
        m_out[...] = jnp.broadcast_to(m_scr[...], m_out.shape)


def _mlstm(z4, gates3, bias, head_norm, state, *, chunk, lp):
    _, b, t, _ = z4.shape
    nc = t // chunk
    per = SEG // DH_C
    zero_state = state is None
    n_pref = 1 if zero_state else 2

    def seg_spec(base):
        return pl.BlockSpec((None, None, chunk, DH_C),
                            lambda bi, h, c, *_: (base + h // per, bi, c, h % per))

    in_specs = [seg_spec(0), seg_spec(2), seg_spec(4), seg_spec(6), seg_spec(8),
                pl.BlockSpec((None, chunk, 128), lambda bi, h, c, *_: (bi, c, 0)),
                pl.BlockSpec((1, DH_C), lambda bi, h, c, *_: (0, h))]
    args = [z4, z4, z4, z4, z4, gates3, head_norm.reshape(1, H_C * DH_C)]
    prefetch = [bias]
    if not zero_state:
        c0, n0, m0 = state
        prefetch.append(m0)
        in_specs += [pl.BlockSpec((None, None, DH_C, DH_C), lambda bi, h, c, *_: (bi, h, 0, 0)),
                     pl.BlockSpec((None, None, 1, DH_C), lambda bi, h, c, *_: (bi, h, 0, 0))]
        args += [c0, n0]
    grid_spec = pltpu.PrefetchScalarGridSpec(
        num_scalar_prefetch=n_pref,
        grid=(b, H_C, nc),
        in_specs=in_specs,
        out_specs=[
            pl.BlockSpec((None, chunk, DH_C), lambda bi, h, c, *_: (bi, c, h)),
            pl.BlockSpec((None, None, DH_C, DH_C), lambda bi, h, c, *_: (bi, h, 0, 0)),
            pl.BlockSpec((None, None, 1, DH_C), lambda bi, h, c, *_: (bi, h, 0, 0)),
            pl.BlockSpec((None, None, 1, 128), lambda bi, h, c, *_: (bi, h, 0, 0)),
        ],
        scratch_shapes=[
            pltpu.VMEM((DH_C, DH_C), F32),
            pltpu.VMEM((1, DH_C), F32),
            pltpu.VMEM((1, 1), F32),
            pltpu.VMEM((6, lp, DH_C), F32),
        ],
    )
    return pl.pallas_call(
        functools.partial(_mlstm_kernel, lv=chunk, lp=lp, zero_state=zero_state),
        grid_spec=grid_spec,
        out_shape=[
            jax.ShapeDtypeStruct((b, t, H_C * DH_C), BF16),
            jax.ShapeDtypeStruct((b, H_C, DH_C, DH_C), F32),
            jax.ShapeDtypeStruct((b, H_C, 1, DH_C), F32),
            jax.ShapeDtypeStruct((b, H_C, 1, 128), F32),
        ],
        compiler_params=_cparams(("parallel", "parallel", "arbitrary")),
        name="mlstm",
    )(*prefetch, *args)


PAGES_PER_STEP = 8

def _sample_attn_kernel(pt_ref, slopes_ref, *refs, n_steps, tq, past_len):
    pp = PAGES_PER_STEP
    k_refs = refs[:pp]
    v_refs = refs[pp:2 * pp]
    qkvg_refs = refs[2 * pp:2 * pp + 4]
    o_ref, s_scr, p_scr, kmean_scr, acc_scr, l_scr = refs[2 * pp + 4:]
    q_ref, kn_ref, vn_ref, ga_ref = qkvg_refs
    s = pl.program_id(1)
    rows = tq * H_A
    cols = PAGE_SIZE * H_A
    ppb = MOBA_BLOCK // PAGE_SIZE
    n_pages = n_steps * pp
    nt = (((1,), (1,)), ((), ()))
    log2e = math.log2(math.e)
    scale2 = DH_A ** -0.5 * log2e

    row = lax.broadcasted_iota(jnp.int32, (rows, 1), 0)
    row_h = row % H_A
    qpos = past_len + row // H_A
    slope2 = jnp.zeros((rows, 1), F32)
    for hh in range(H_A):
        slope2 = jnp.where(row_h == hh, slopes_ref[hh] * log2e, slope2)
    lane = lax.broadcasted_iota(jnp.int32, (1, cols), 1)

    def chunk(g):
        return pl.ds(pl.multiple_of(g * cols, cols), cols)

    @pl.when(s < n_steps)
    def _():
        qb = q_ref[...].reshape(rows, DH_A).astype(BF16)
        for jj in range(pp // ppb):
            tot = None
            for p in range(ppb):
                part = jnp.sum(k_refs[jj * ppb + p][...], axis=0)
                tot = part if tot is None else tot + part
            kmean_scr[pl.ds(pl.multiple_of((s * (pp // ppb) + jj) * H_A, H_A), H_A), :] = tot * (1.0 / MOBA_BLOCK)
        head_ok = (lane % H_A) == row_h
        qoff = slope2 * qpos.astype(F32)
        for p in range(pp):
            g = s * pp + p
            k2 = k_refs[p][...].reshape(cols, DH_A).astype(BF16)
            st = lax.dot_general(qb, k2, nt, preferred_element_type=F32)
            kpos = (g * PAGE_SIZE + lane // H_A).astype(F32)
            sv = st * scale2 + (slope2 * kpos - qoff)
            s_scr[:, chunk(g)] = jnp.where(head_ok, sv, -jnp.inf)

    @pl.when(s == n_steps)
    def _():
        qb = q_ref[...].reshape(rows, DH_A).astype(BF16)
        nl = kmean_scr.shape[0]
        gate = lax.dot_general(qb, kmean_scr[...].astype(BF16), nt, preferred_element_type=F32)
        glane = lax.broadcasted_iota(jnp.int32, (rows, nl), 1)
        gate = jnp.where(glane % H_A == row_h, gate, -jnp.inf)
        sel = []
        for t in range(MOBA_TOPK):
            mx = jnp.max(gate, axis=1, keepdims=True)
            idx = jnp.min(jnp.where(gate == mx, glane, nl), axis=1, keepdims=True)
            sel.append(idx // H_A)
            gate = jnp.where(glane == idx, -jnp.inf, gate)

        def picked(g):
            blk = g // ppb
            return (sel[0] == blk) | (sel[1] == blk) | (sel[2] == blk)

        kn2 = kn_ref[...].reshape(rows, DH_A).astype(BF16)
        olane = lax.broadcasted_iota(jnp.int32, (1, rows), 1)
        own_pos = past_len + olane // H_A
        s_own = lax.dot_general(qb, kn2, nt, preferred_element_type=F32) * scale2
        s_own = s_own - slope2 * (qpos - own_pos).astype(F32)
        s_own = jnp.where(((olane % H_A) == row_h) & (own_pos <= qpos), s_own, -jnp.inf)

        def fold_lanes(x, op):
            acc = x[:, 0:128]
            for kk in range(1, cols // 128):
                acc = op(acc, x[:, kk * 128:(kk + 1) * 128])
            return acc

        def max_body(g, macc):
            pm = fold_lanes(s_scr[:, chunk(g)], jnp.maximum)
            return jnp.maximum(macc, jnp.where(picked(g), pm, -jnp.inf))

        macc = lax.fori_loop(0, n_pages, max_body, jnp.full((rows, 128), -jnp.inf, F32), unroll=2)
        m = jnp.maximum(jnp.max(macc, axis=1, keepdims=True), jnp.max(s_own, axis=1, keepdims=True))

        def exp_body(g, lacc):
            p = jnp.exp2(s_scr[:, chunk(g)] - jnp.where(picked(g), m, jnp.inf))
            p_scr[:, chunk(g)] = p.astype(BF16)
            return lacc + fold_lanes(p, jnp.add)

        p_own = jnp.exp2(s_own - m)
        lacc = lax.fori_loop(0, n_pages, exp_body, jnp.zeros((rows, 128), F32), unroll=2)
        l_scr[...] = jnp.sum(lacc, axis=1, keepdims=True) + jnp.sum(p_own, axis=1, keepdims=True)
        vn2 = vn_ref[...].reshape(rows, DH_A).astype(BF16)
        acc_scr[...] = jnp.dot(p_own.astype(BF16), vn2, preferred_element_type=F32)

    @pl.when(s >= n_steps)
    def _():
        acc = acc_scr[...]
        for p in range(pp):
            g = (s - n_steps) * pp + p
            v2 = v_refs[p][...].reshape(cols, DH_A).astype(BF16)
            acc = acc + jnp.dot(p_scr[:, chunk(g)], v2, preferred_element_type=F32)
        acc_scr[...] = acc

    @pl.when(s == 2 * n_steps - 1)
    def _():
        o = acc_scr[...] / l_scr[...]
        o = o * _silu(ga_ref[...].reshape(rows, DH_A))
        o_ref[...] = o.reshape(tq, H_A, DH_A)


def _sample_attn(cache_k4, cache_v4, page_table, slopes, qkvg5):
    b, n_pages = page_table.shape
    _, _, tq, _, _ = qkvg5.shape
    pp = PAGES_PER_STEP
    n_steps = n_pages // pp
    past_len = n_pages * PAGE_SIZE
    nbp = past_len // MOBA_BLOCK
    rows = tq * H_A

    def k_spec(p):
        return pl.BlockSpec((None, PAGE_SIZE, H_A, DH_A),
                            lambda bi, s, pt, sl: (pt[bi, jnp.minimum(s, n_steps - 1) * pp + p], 0, 0, 0))

    def v_spec(p):
        return pl.BlockSpec((None, PAGE_SIZE, H_A, DH_A),
                            lambda bi, s, pt, sl: (pt[bi, jnp.maximum(s - n_steps, 0) * pp + p], 0, 0, 0))

    def new_spec(sidx):
        return pl.BlockSpec((None, None, tq, H_A, DH_A), lambda bi, s, pt, sl: (sidx, bi, 0, 0, 0))

    grid_spec = pltpu.PrefetchScalarGridSpec(
        num_scalar_prefetch=2,
        grid=(b, 2 * n_steps),
        in_specs=[k_spec(p) for p in range(pp)] + [v_spec(p) for p in range(pp)]
        + [new_spec(0), new_spec(1), new_spec(2), new_spec(3)],
        out_specs=pl.BlockSpec((None, tq, H_A, DH_A), lambda bi, s, pt, sl: (bi, 0, 0, 0)),
        scratch_shapes=[
            pltpu.VMEM((rows, n_pages * PAGE_SIZE * H_A), F32),
            pltpu.VMEM((rows, n_pages * PAGE_SIZE * H_A), BF16),
            pltpu.VMEM((nbp * H_A, DH_A), F32),
            pltpu.VMEM((rows, DH_A), F32),
            pltpu.VMEM((rows, 1), F32),
        ],
    )
    return pl.pallas_call(
        functools.partial(_sample_attn_kernel, n_steps=n_steps, tq=tq, past_len=past_len),
        grid_spec=grid_spec,
        out_shape=jax.ShapeDtypeStruct((b, tq, H_A, DH_A), F32),
        compiler_params=_cparams(("parallel", "arbitrary")),
        name="sample_attn",
    )(page_table, slopes, *([cache_k4] * pp), *([cache_v4] * pp), qkvg5, qkvg5, qkvg5, qkvg5)


def _alibi_slopes(n_heads):
    return 2.0 ** (-8.0 * jnp.arange(1, n_heads + 1, dtype=F32) / n_heads)


def kernel(x_prompt, x_sample, cache_k, cache_v, state_pool, state_C, state_n, state_m, page_table,
           c_prompt, c_sample, ada_w, ada_b, norm_pre, norm_post, w_in_even, pool_w, pool_scale,
           w_out_even, w_in_odd, b_igate, b_fgate, head_norm, w_out_odd):
    bp, tp, d = x_prompt.shape
    bs, ts, _ = x_sample.shape
    n_pages = page_table.shape[1]
    past_len = n_pages * PAGE_SIZE
    w_a = H_A * DH_A
    w_c = H_C * DH_C
    rp, rs = bp * tp, bs * ts
    tm_p = 1024

    mods = _adaln(jnp.concatenate([c_prompt, c_sample], axis=0), ada_w, ada_b)

    def mod_parts(l):
        m = mods[l]
        shift, scale, gate = m[:, :d], m[:, d:2 * d], m[:, 2 * d:]
        prompt = tuple(a[:bp].reshape(bp, 1, d) for a in (shift, scale, gate))
        sample = tuple(jnp.repeat(a[bp:], ts, axis=0) for a in (shift, scale, gate))
        return prompt, sample

    slopes = _alibi_slopes(H_A)
    xp = x_prompt.reshape(rp, d)
    xs = x_sample.reshape(rs, d)

    (sh_p, sc_p, gt_p), (sh_s, sc_s, gt_s) = mod_parts(0)
    w_in0 = w_in_even[0].astype(BF16)
    zp, kv_p = _inproj(xp, norm_pre[0], sc_p, sh_p, w_in0, None, rows_per_group=tp, tm=tm_p, hm=(1, 2, H_A))
    zs, qkvg_s = _inproj(xs, norm_pre[0], sc_s, sh_s, w_in0, None, rows_per_group=ts, tm=rs, hm=(0, 4, H_A))
    kv_s = qkvg_s[1:3]
    z4p = zp.reshape(zp.shape[0], bp, tp, SEG)
    z4s = zs.reshape(zs.shape[0], bs, ts, SEG)

    pool_w_bf = pool_w[0].astype(BF16)
    att_p = _moba_prompt(z4p, slopes)
    pool_p, pstate_p = _pool(z4p, jnp.zeros((bp, POOL_HIST, SEG), F32), pool_w_bf, pool_scale[0],
                             pos0=0, tt=512, out_dtype=BF16)

    att_s = _sample_attn(cache_k[0], cache_v[0], page_table, slopes, qkvg_s.reshape(4, bs, ts, H_A, DH_A))
    pool_s, pstate_s = _pool(z4s, state_pool[0], pool_w_bf, pool_scale[0],
                             pos0=past_len, tt=ts, out_dtype=F32)

    w_out0 = w_out_even[0].astype(BF16)
    w_list0 = [w_out0[:w_a], w_out0[w_a:]]
    xp1 = _outproj([att_p.reshape(rp, w_a), pool_p.reshape(rp, SEG)], w_list0, xp, gt_p, norm_post[0],
                   rows_per_group=tp, tm=512)
    xs1 = _outproj([att_s.reshape(rs, w_a), pool_s.reshape(rs, SEG)], w_list0, xs, gt_s, norm_post[0],
                   rows_per_group=ts, tm=rs)

    (sh_p, sc_p, gt_p), (sh_s, sc_s, gt_s) = mod_parts(1)
    n_main = 5 * w_c
    w_in1 = w_in_odd[0][:, :n_main].astype(BF16)
    wg1 = jnp.pad(w_in_odd[0][:, n_main:], ((0, 0), (0, 128 - 2 * H_C))).astype(BF16)
    zp, gates_p = _inproj(xp1, norm_pre[1], sc_p, sh_p, w_in1, wg1, rows_per_group=tp, tm=tm_p)
    zs, gates_s = _inproj(xs1, norm_pre[1], sc_s, sh_s, w_in1, wg1, rows_per_group=ts, tm=rs)
    bias = jnp.stack([b_igate[0], b_fgate[0]]).astype(F32)
    hc_p, c_p, n_p, m_p = _mlstm(zp.reshape(zp.shape[0], bp, tp, SEG), gates_p.reshape(bp, tp, 128),
                                 bias, head_norm[0], None, chunk=128, lp=128)
    state = (state_C[0], state_n[0].reshape(bs, H_C, 1, DH_C), state_m[0])
    hc_s, c_s, n_s, m_s = _mlstm(zs.reshape(zs.shape[0], bs, ts, SEG), gates_s.reshape(bs, ts, 128),
                                 bias, head_norm[0], state, chunk=ts, lp=8)
    w_out1 = w_out_odd[0].astype(BF16)
    xp2 = _outproj([hc_p.reshape(rp, w_c)], [w_out1], xp1, gt_p, norm_post[1], rows_per_group=tp, tm=512)
    xs2 = _outproj([hc_s.reshape(rs, w_c)], [w_out1], xs1, gt_s, norm_post[1], rows_per_group=ts, tm=rs)

    kv_shape_p = (1, bp, tp, H_A, DH_A)
    kv_shape_s = (1, bs, ts, H_A, DH_A)
    return (xp2.reshape(bp, tp, d), xs2.reshape(bs, ts, d),
            kv_p[0].reshape(kv_shape_p), kv_p[1].reshape(kv_shape_p),
            kv_s[0].reshape(kv_shape_s), kv_s[1].reshape(kv_shape_s),
            pstate_p[None], pstate_s[None],
            c_p[None], n_p.reshape(1, bp, H_C, DH_C), m_p[:, :, 0, 0][None],
            c_s[None], n_s.reshape(1, bs, H_C, DH_C), m_s[:, :, 0, 0][None])
```

```python
import functools
import math

import jax
import jax.numpy as jnp
from jax import lax
from jax.experimental import pallas as pl
from jax.experimental.pallas import tpu as pltpu

F32 = jnp.float32
BF16 = jnp.bfloat16

H_A = 8
DH_A = 128
MOBA_BLOCK = 256
MOBA_TOPK = 3
POOL_WINDOWS = (2, 4, 8, 16)
GW_B = 256
POOL_HIST = max(POOL_WINDOWS) - 1
H_C = 8
DH_C = 256
EPS = 1e-6
PAGE_SIZE = 128

SEG = 1024
NEG_BIG = -1e30

VMEM_LIMIT_V7X = 52 * 1024 * 1024


def _silu(x):
    return x * (1.0 / (1.0 + jnp.exp(-x)))


def _sigmoid(x):
    return 1.0 / (1.0 + jnp.exp(-x))


def _log_sigmoid(x):
    return -(jnp.maximum(-x, 0.0) + jnp.log(1.0 + jnp.exp(-jnp.abs(x))))


def _cparams(sem):
    return pltpu.CompilerParams(dimension_semantics=sem, vmem_limit_bytes=VMEM_LIMIT_V7X)


def _adaln_kernel(c_ref, w_ref, b_ref, o_ref):
    s = _silu(c_ref[...]).astype(BF16)
    o_ref[...] = jnp.dot(s, w_ref[...].astype(BF16), preferred_element_type=F32) + b_ref[...]


def _adaln(c_all, ada_w, ada_b, tn=512):
    depth, d, n = ada_w.shape
    r = c_all.shape[0]
    return pl.pallas_call(
        _adaln_kernel,
        grid=(depth, n // tn),
        in_specs=[
            pl.BlockSpec((r, d), lambda l, j: (0, 0)),
            pl.BlockSpec((None, d, tn), lambda l, j: (l, 0, j)),
            pl.BlockSpec((None, 1, tn), lambda l, j: (l, 0, j)),
        ],
        out_specs=pl.BlockSpec((None, r, tn), lambda l, j: (l, 0, j)),
        out_shape=jax.ShapeDtypeStruct((depth, r, n), F32),
        compiler_params=_cparams(("parallel", "parallel")),
        name="adaln",
    )(c_all, ada_w, ada_b.reshape(depth, 1, n))


def _inproj_kernel(*refs, has_gate, hm):
    hm_ref = gt_ref = wg_ref = None
    if has_gate:
        x_ref, g_ref, sc_ref, sh_ref, w_ref, wg_ref, z_ref, gt_ref, h_scr = refs
    elif hm:
        x_ref, g_ref, sc_ref, sh_ref, w_ref, z_ref, hm_ref, h_scr = refs
    else:
        x_ref, g_ref, sc_ref, sh_ref, w_ref, z_ref, h_scr = refs
    j = pl.program_id(1)

    @pl.when(pl.program_id(1) == 0)
    def _():
        x = x_ref[...]
        r = x * lax.rsqrt(jnp.mean(x * x, axis=-1, keepdims=True) + EPS)
        h = (r * g_ref[...]) * (1.0 + sc_ref[...]) + sh_ref[...]
        h_scr[...] = h.astype(BF16)
        if has_gate:
            gt_ref[...] = jnp.dot(h_scr[...], wg_ref[...], preferred_element_type=F32)

    res = jnp.dot(h_scr[...], w_ref[...], preferred_element_type=F32)
    z_ref[...] = res

    if hm:
        first, count, heads = hm

        @pl.when((j >= first) & (j < first + count))
        def _():
            dh = SEG // heads
            for hh in range(heads):
                hm_ref[:, hh, :] = res[:, hh * dh:(hh + 1) * dh]


def _inproj(x2, g_pre, scale, shift, w_bf, wg_bf, *, rows_per_group, tm, hm=None):
    r, d = x2.shape
    n = w_bf.shape[1]
    nseg = n // SEG
    has_gate = wg_bf is not None
    if scale.ndim == 3:
        per = rows_per_group // tm
        mod_spec = pl.BlockSpec((None, 1, d), lambda i, j: (i // per, 0, 0))
    else:
        mod_spec = pl.BlockSpec((tm, d), lambda i, j: (i, 0))
    in_specs = [
        pl.BlockSpec((tm, d), lambda i, j: (i, 0), pipeline_mode=pl.Buffered(1)),
        pl.BlockSpec((1, d), lambda i, j: (0, 0)),
        mod_spec,
        mod_spec,
        pl.BlockSpec((d, SEG), lambda i, j: (0, j)),
    ]
    args = [x2, g_pre.reshape(1, d), scale, shift, w_bf]
    out_specs = [pl.BlockSpec((None, tm, SEG), lambda i, j: (j, i, 0))]
    out_shape = [jax.ShapeDtypeStruct((nseg, r, SEG), F32)]
    if has_gate:
        in_specs.append(pl.BlockSpec((d, 128), lambda i, j: (0, 0)))
        args.append(wg_bf)
        out_specs.append(pl.BlockSpec((tm, 128), lambda i, j: (i, 0)))
        out_shape.append(jax.ShapeDtypeStruct((r, 128), F32))
    if hm:
        first, count, heads = hm
        out_specs.append(pl.BlockSpec((None, tm, heads, SEG // heads),
                                      lambda i, j: (jnp.clip(j - first, 0, count - 1), i, 0, 0)))
        out_shape.append(jax.ShapeDtypeStruct((count, r, heads, SEG // heads), F32))
    res = pl.pallas_call(
        functools.partial(_inproj_kernel, has_gate=has_gate, hm=hm),
        grid=(r // tm, nseg),
        in_specs=in_specs,
        out_specs=out_specs,
        out_shape=out_shape,
        scratch_shapes=[pltpu.VMEM((tm, d), BF16)],
        compiler_params=_cparams(("parallel", "arbitrary")),
        name="inproj_gate" if has_gate else "inproj",
    )(*args)
    return (res[0], res[1]) if (has_gate or hm) else (res[0], None)


def _outproj_kernel(*refs, n_a):
    a_refs = refs[:n_a]
    w_refs = refs[n_a:2 * n_a]
    x_ref, gate_ref, gp_ref, o_ref = refs[2 * n_a:]
    y = None
    for a_ref, w_ref in zip(a_refs, w_refs):
        t = jnp.dot(a_ref[...].astype(BF16), w_ref[...], preferred_element_type=F32)
        y = t if y is None else y + t
    r = y * lax.rsqrt(jnp.mean(y * y, axis=-1, keepdims=True) + EPS)
    o_ref[...] = x_ref[...] + gate_ref[...] * (r * gp_ref[...])


def _outproj(a_list, w_list, x2, gate, g_post, *, rows_per_group, tm):
    r, d = x2.shape
    n_a = len(a_list)
    if gate.ndim == 3:
        per = rows_per_group // tm
        gate_spec = pl.BlockSpec((None, 1, d), lambda i: (i // per, 0, 0))
    else:
        gate_spec = pl.BlockSpec((tm, d), lambda i: (i, 0))
    in_specs = [pl.BlockSpec((tm, a.shape[1]), lambda i: (i, 0)) for a in a_list]
    in_specs += [pl.BlockSpec(w.shape, lambda i: (0, 0)) for w in w_list]
    in_specs += [pl.BlockSpec((tm, d), lambda i: (i, 0)), gate_spec, pl.BlockSpec((1, d), lambda i: (0, 0))]
    return pl.pallas_call(
        functools.partial(_outproj_kernel, n_a=n_a),
        grid=(r // tm,),
        in_specs=in_specs,
        out_specs=pl.BlockSpec((tm, d), lambda i: (i, 0)),
        out_shape=jax.ShapeDtypeStruct((r, d), F32),
        compiler_params=_cparams(("parallel",)),
        name="outproj",
    )(*a_list, *w_list, x2, gate, g_post.reshape(1, d))


MOBA_AUX = 128
ALIBI_COL = 8


def _moba_kernel(slopes_ref, q_ref, k_ref, v_ref, ga_ref, o_ref, kaug_scr, vt_scr, *, nb):
    h = pl.program_id(1)
    blk = MOBA_BLOCK
    t = nb * blk
    log2e = math.log2(math.e)
    scale2 = DH_A ** -0.5 * log2e
    nt = (((1,), (1,)), ((), ()))

    k = k_ref[...]
    kmean_b = jnp.concatenate(
        [jnp.mean(k[j * blk:(j + 1) * blk, :], axis=0, keepdims=True) for j in range(nb)], axis=0).astype(BF16)
    pos = lax.broadcasted_iota(jnp.int32, (t, MOBA_AUX), 0)
    lane = lax.broadcasted_iota(jnp.int32, (t, MOBA_AUX), 1)
    a = (slopes_ref[h] * log2e) * pos.astype(F32)
    a_hi = a.astype(BF16).astype(F32)
    a_mid = (a - a_hi).astype(BF16).astype(F32)
    a_lo = a - a_hi - a_mid
    aux = jnp.where(lane == ALIBI_COL, a_hi,
                    jnp.where(lane == ALIBI_COL + 1, a_mid,
                              jnp.where(lane == ALIBI_COL + 2, a_lo,
                                        jnp.where(lane == pos // blk, 1.0, 0.0))))
    kaug_scr[...] = jnp.concatenate([k.astype(BF16), aux.astype(BF16)], axis=1)
    vt_scr[...] = jnp.transpose(v_ref[...]).astype(BF16)

    ones_rows = jnp.where(lax.broadcasted_iota(jnp.int32, (8, blk), 0) < 3, 1.0, 0.0)
    rowb = lax.broadcasted_iota(jnp.int32, (nb, blk), 0)

    def attend(n):
        nk = (n + 1) * blk
        q = q_ref[n * blk:nk, :]
        if n > MOBA_TOPK:
            gate = lax.dot_general(kmean_b, q.astype(BF16), nt, preferred_element_type=F32)
            gm = jnp.where(rowb < n, gate, -jnp.inf)
            cnt = jnp.zeros((nb, blk), F32)
            for kk in range(n):
                gk = gm[kk:kk + 1, :]
                beats = (gk > gm) | ((gk == gm) & (rowb > kk))
                cnt = cnt + jnp.where(beats, 1.0, 0.0)
            keep = ((rowb < n) & (cnt < MOBA_TOPK)) | (rowb == n)
            bias = jnp.where(keep, 0.0, NEG_BIG)
        else:
            bias = jnp.zeros((nb, blk), F32)
        pieces = [bias, ones_rows]
        if nb < ALIBI_COL:
            pieces.insert(1, jnp.zeros((ALIBI_COL - nb, blk), F32))
        pieces.append(jnp.zeros((MOBA_AUX - ALIBI_COL - 8, blk), F32))
        rhs = jnp.concatenate([(jnp.transpose(q) * scale2).astype(BF16),
                               jnp.concatenate(pieces, axis=0).astype(BF16)], axis=0)
        s = jnp.dot(kaug_scr[0:nk, :], rhs, preferred_element_type=F32)
        krow = lax.broadcasted_iota(jnp.int32, (blk, blk), 0)
        qcol = lax.broadcasted_iota(jnp.int32, (blk, blk), 1)
        s_own = jnp.where(krow <= qcol, s[n * blk:nk, :], -jnp.inf)
        m = jnp.max(s_own, axis=0, keepdims=True)
        if n > 0:
            m = jnp.maximum(m, jnp.max(s[0:n * blk, :], axis=0, keepdims=True))
        p_own = jnp.exp2(s_own - m)
        l = jnp.sum(p_own, axis=0, keepdims=True)
        ot = jnp.dot(vt_scr[:, n * blk:nk], p_own.astype(BF16), preferred_element_type=F32)
        if n > 0:
            p = jnp.exp2(s[0:n * blk, :] - m)
            l = l + jnp.sum(p, axis=0, keepdims=True)
            ot = ot + jnp.dot(vt_scr[:, 0:n * blk], p.astype(BF16), preferred_element_type=F32)
        o = jnp.transpose(ot / l)
        o_ref[n * blk:nk, :] = (o * _silu(ga_ref[n * blk:nk, :])).astype(o_ref.dtype)

    for n in range(nb):
        attend(n)


def _moba_prompt(z4, slopes):
    _, b, t, w = z4.shape
    nb = t // MOBA_BLOCK
    assert nb <= ALIBI_COL

    def seg_spec(sidx):
        return pl.BlockSpec((None, None, t, DH_A), lambda bi, h, s: (sidx, bi, 0, h))

    grid_spec = pltpu.PrefetchScalarGridSpec(
        num_scalar_prefetch=1,
        grid=(b, H_A),
        in_specs=[seg_spec(0), seg_spec(1), seg_spec(2), seg_spec(3)],
        out_specs=pl.BlockSpec((None, t, DH_A), lambda bi, h, s: (bi, 0, h)),
        scratch_shapes=[
            pltpu.VMEM((t, DH_A + MOBA_AUX), BF16),
            pltpu.VMEM((DH_A, t), BF16),
        ],
    )
    return pl.pallas_call(
        functools.partial(_moba_kernel, nb=nb),
        grid_spec=grid_spec,
        out_shape=jax.ShapeDtypeStruct((b, t, w), BF16),
        compiler_params=_cparams(("parallel", "parallel")),
        name="moba_prompt",
    )(slopes, z4, z4, z4, z4)


def _pool_kernel(u_ref, gb_ref, prev_ref, pw_ref, ps_ref, o_ref, pn_ref, ext_scr, *, tt, pos0, nt):
    t = pl.program_id(1)
    hist = POOL_HIST + 1

    @pl.when(t == 0)
    def _():
        ext_scr[0:1, :] = jnp.zeros((1, ext_scr.shape[1]), F32)
        ext_scr[1:hist, :] = prev_ref[...]

    ext_scr[hist:hist + tt, :] = u_ref[...]
    pos = pos0 + t * tt + lax.broadcasted_iota(jnp.int32, (tt, 1), 0)
    for g, w in enumerate(POOL_WINDOWS):
        sl = slice(g * GW_B, (g + 1) * GW_B)
        x = u_ref[:, sl]
        acc = x
        for jj in range(1, w):
            acc = acc + ext_scr[hist - jj:hist - jj + tt, sl]
        cnt = jnp.minimum(w, pos + 1).astype(F32)
        d = acc / cnt - x
        y = jnp.dot(d.astype(BF16), pw_ref[g], preferred_element_type=F32) * ps_ref[:, sl]
        o_ref[:, sl] = (y * _silu(gb_ref[:, sl])).astype(o_ref.dtype)

    tail = ext_scr[tt:tt + hist, :]
    ext_scr[0:hist, :] = tail

    @pl.when(t == nt - 1)
    def _():
        pn_ref[...] = ext_scr[1:hist, :]


def _pool(z4, prev, pool_w_bf, pool_scale, *, pos0, tt, out_dtype):
    _, b, t, w = z4.shape
    nt = t // tt
    return pl.pallas_call(
        functools.partial(_pool_kernel, tt=tt, pos0=pos0, nt=nt),
        grid=(b, nt),
        in_specs=[
            pl.BlockSpec((None, None, tt, w), lambda bi, ti: (4, bi, ti, 0)),
            pl.BlockSpec((None, None, tt, w), lambda bi, ti: (5, bi, ti, 0)),
            pl.BlockSpec((None, POOL_HIST, w), lambda bi, ti: (bi, 0, 0)),
            pl.BlockSpec(pool_w_bf.shape, lambda bi, ti: (0, 0, 0)),
            pl.BlockSpec((1, w), lambda bi, ti: (0, 0)),
        ],
        out_specs=[
            pl.BlockSpec((None, tt, w), lambda bi, ti: (bi, ti, 0)),
            pl.BlockSpec((None, POOL_HIST, w), lambda bi, ti: (bi, 0, 0)),
        ],
        out_shape=[
            jax.ShapeDtypeStruct((b, t, w), out_dtype),
            jax.ShapeDtypeStruct((b, POOL_HIST, w), F32),
        ],
        scratch_shapes=[pltpu.VMEM((POOL_HIST + 1 + tt, w), F32)],
        compiler_params=_cparams(("parallel", "arbitrary")),
        name="pool",
    )(z4, z4, prev, pool_w_bf, pool_scale.reshape(1, w))


def _mlstm_kernel(*refs, lv, lp, hps, zero_state):
    if zero_state:
        (bias_ref, q_ref, k_ref, v_ref, og_ref, g_ref, gt_ref, hn_ref,
         h_out, c_out, n_out, m_out, c_scr, n_scr, m_scr, pad_scr) = refs
    else:
        (bias_ref, m0_ref, q_ref, k_ref, v_ref, og_ref, g_ref, gt_ref, hn_ref, c0_ref, n0_ref,
         h_out, c_out, n_out, m_out, c_scr, n_scr, m_scr, pad_scr) = refs
    bi = pl.program_id(0)
    hg = pl.program_id(1)
    c = pl.program_id(2)
    nc = pl.num_programs(2)

    @pl.when(c == 0)
    def _():
        if zero_state:
            c_scr[...] = jnp.zeros(c_scr.shape, F32)
            n_scr[...] = jnp.zeros(n_scr.shape, F32)
            m_scr[...] = jnp.zeros(m_scr.shape, F32)
        else:
            c_scr[...] = c0_ref[...]
            n_scr[...] = n0_ref[...]
            for hh in range(hps):
                m_scr[hh] = jnp.full(m_scr.shape[1:], m0_ref[bi, hg * hps + hh], F32)

    def load(ref, slot):
        if lv == lp:
            return ref
        pad_scr[slot] = jnp.zeros(pad_scr.shape[1:], F32)
        pad_scr[slot, 0:lv, 0:ref.shape[1]] = ref[...]
        return pad_scr.at[slot]

    q_src, k_src, v_src, og_src, g_src, gt_src = (load(r, i) for i, r in
                                                  enumerate((q_ref, k_ref, v_ref, og_ref, g_ref, gt_ref)))
    gts = gt_src[:, 0:128]
    lane = lax.broadcasted_iota(jnp.int32, gts.shape, 1)
    row1 = lax.broadcasted_iota(jnp.int32, (lp, 1), 0)
    rr = lax.broadcasted_iota(jnp.int32, (lp, lp), 0)
    cc = lax.broadcasted_iota(jnp.int32, (lp, lp), 1)
    eye = rr == cc
    tril = rr >= cc
    nt = (((1,), (1,)), ((), ()))

    def to_row(col):
        return jnp.sum(jnp.where(eye, col, 0.0), axis=0, keepdims=True)

    for hh in range(hps):
        h = hg * hps + hh
        sl = slice(hh * DH_C, (hh + 1) * DH_C)
        q = q_src[:, sl]
        k = k_src[:, sl] * (DH_C ** -0.5)
        v = v_src[:, sl]

        ig_col = jnp.sum(jnp.where(lane == h, gts, 0.0), axis=1, keepdims=True) + bias_ref[0, h]
        gf_col = jnp.sum(jnp.where(lane == H_C + h, gts, 0.0), axis=1, keepdims=True) + bias_ref[1, h]
        lf_col = _log_sigmoid(gf_col)
        if lv != lp:
            ig_col = jnp.where(row1 < lv, ig_col, NEG_BIG)
            lf_col = jnp.where(row1 < lv, lf_col, 0.0)

        lf_row = to_row(lf_col)
        ig_row = to_row(ig_col)
        b_col = jnp.sum(jnp.where(tril, lf_row, 0.0), axis=1, keepdims=True)
        b_row = to_row(b_col)
        dmat = jnp.where(tril, b_col - b_row + ig_row, -jnp.inf)
        m_prev = m_scr[hh]
        carry = b_col + m_prev
        mt = jnp.maximum(carry, jnp.max(dmat, axis=1, keepdims=True))

        qb = q.astype(BF16)
        kb = k.astype(BF16)
        vb = v.astype(BF16)
        s = lax.dot_general(qb, kb, nt, preferred_element_type=F32) * jnp.exp(dmat - mt)
        inter = jnp.exp(carry - mt)
        cq = lax.dot_general(qb, c_scr[hh].astype(BF16), nt, preferred_element_type=F32)
        num = jnp.dot(s.astype(BF16), vb, preferred_element_type=F32) + inter * cq
        nq = jnp.sum(q * n_scr[hh], axis=1, keepdims=True)
        den = jnp.sum(s, axis=1, keepdims=True) + inter * nq
        hc = num / jnp.maximum(jnp.abs(den), jnp.exp(-mt))

        m_new = mt[lp - 1:lp, :]
        b_last = b_col[lp - 1:lp, :]
        w_col = jnp.exp(b_last - b_col + ig_col - m_new)
        decay = jnp.exp(b_last + m_prev - m_new)
        vw = (v * w_col).astype(BF16)
        c_scr[hh] = decay * c_scr[hh] + lax.dot_general(
            vw, kb, (((0,), (0,)), ((), ())), preferred_element_type=F32)
        n_scr[hh] = decay * n_scr[hh] + jnp.sum(k * w_col, axis=0, keepdims=True)
        m_scr[hh] = m_new

        hc = hc * _sigmoid(og_src[:, sl])
        hc = hc * lax.rsqrt(jnp.mean(hc * hc, axis=-1, keepdims=True) + EPS) * hn_ref[:, sl]
        res = (hc * _silu(g_src[:, sl])).astype(h_out.dtype)
        h_out[:, sl] = res if lv == lp else res[0:lv, :]

    @pl.when(c == nc - 1)
    def _():
        c_out[...] = c_scr[...]
        n_out[...] = n_scr[...]
        m_out[...] = jnp.broadcast_to(m_scr[...], m_out.shape)


def _mlstm(z4, gates3, bias, head_norm, state, *, chunk, lp):
    _, b, t, _ = z4.shape
    nc = t // chunk
    hps = SEG // DH_C
    ngrp = H_C // hps
    zero_state = state is None
    n_pref = 1 if zero_state else 2

    def seg_spec(base):
        return pl.BlockSpec((None, None, chunk, SEG), lambda bi, hg, c, *_: (base + hg, bi, c, 0))

    def state_spec(*tail):
        return pl.BlockSpec((None, hps) + tail, lambda bi, hg, c, *_: (bi, hg) + (0,) * len(tail))

    in_specs = [seg_spec(0), seg_spec(2), seg_spec(4), seg_spec(6), seg_spec(8),
                pl.BlockSpec((None, chunk, 128), lambda bi, hg, c, *_: (bi, c, 0)),
                pl.BlockSpec((1, SEG), lambda bi, hg, c, *_: (0, hg))]
    args = [z4, z4, z4, z4, z4, gates3, head_norm.reshape(1, H_C * DH_C)]
    prefetch = [bias]
    if not zero_state:
        c0, n0, m0 = state
        prefetch.append(m0)
        in_specs += [state_spec(DH_C, DH_C), state_spec(1, DH_C)]
        args += [c0, n0]
    grid_spec = pltpu.PrefetchScalarGridSpec(
        num_scalar_prefetch=n_pref,
        grid=(b, ngrp, nc),
        in_specs=in_specs,
        out_specs=[
            pl.BlockSpec((None, chunk, SEG), lambda bi, hg, c, *_: (bi, c, hg)),
            state_spec(DH_C, DH_C),
            state_spec(1, DH_C),
            state_spec(1, 128),
        ],
        scratch_shapes=[
            pltpu.VMEM((hps, DH_C, DH_C), F32),
            pltpu.VMEM((hps, 1, DH_C), F32),
            pltpu.VMEM((hps, 1, 1), F32),
            pltpu.VMEM((6, lp, SEG), F32),
        ],
    )
    return pl.pallas_call(
        functools.partial(_mlstm_kernel, lv=chunk, lp=lp, hps=hps, zero_state=zero_state),
        grid_spec=grid_spec,
        out_shape=[
            jax.ShapeDtypeStruct((b, t, H_C * DH_C), BF16),
            jax.ShapeDtypeStruct((b, H_C, DH_C, DH_C), F32),
            jax.ShapeDtypeStruct((b, H_C, 1, DH_C), F32),
            jax.ShapeDtypeStruct((b, H_C, 1, 128), F32),
        ],
        compiler_params=_cparams(("parallel", "parallel", "arbitrary")),
        name="mlstm",
    )(*prefetch, *args)


PAGES_PER_STEP = 8

def _sample_attn_kernel(pt_ref, slopes_ref, *refs, n_steps, tq, past_len):
    pp = PAGES_PER_STEP
    k_refs = refs[:pp]
    v_refs = refs[pp:2 * pp]
    qkvg_refs = refs[2 * pp:2 * pp + 4]
    o_ref, s_scr, p_scr, kmean_scr, acc_scr, l_scr = refs[2 * pp + 4:]
    q_ref, kn_ref, vn_ref, ga_ref = qkvg_refs
    s = pl.program_id(1)
    rows = tq * H_A
    cols = PAGE_SIZE * H_A
    ppb = MOBA_BLOCK // PAGE_SIZE
    n_pages = n_steps * pp
    nt = (((1,), (1,)), ((), ()))
    log2e = math.log2(math.e)
    scale2 = DH_A ** -0.5 * log2e

    row = lax.broadcasted_iota(jnp.int32, (rows, 1), 0)
    row_h = row % H_A
    qpos = past_len + row // H_A
    slope2 = jnp.zeros((rows, 1), F32)
    for hh in range(H_A):
        slope2 = jnp.where(row_h == hh, slopes_ref[hh] * log2e, slope2)
    lane = lax.broadcasted_iota(jnp.int32, (1, cols), 1)

    def chunk(g):
        return pl.ds(pl.multiple_of(g * cols, cols), cols)

    @pl.when(s < n_steps)
    def _():
        qb = q_ref[...].reshape(rows, DH_A).astype(BF16)
        for jj in range(pp // ppb):
            tot = None
            for p in range(ppb):
                part = jnp.sum(k_refs[jj * ppb + p][...], axis=0)
                tot = part if tot is None else tot + part
            kmean_scr[pl.ds(pl.multiple_of((s * (pp // ppb) + jj) * H_A, H_A), H_A), :] = tot * (1.0 / MOBA_BLOCK)
        head_ok = (lane % H_A) == row_h
        qoff = slope2 * qpos.astype(F32)
        for p in range(pp):
            g = s * pp + p
            k2 = k_refs[p][...].reshape(cols, DH_A).astype(BF16)
            st = lax.dot_general(qb, k2, nt, preferred_element_type=F32)
            kpos = (g * PAGE_SIZE + lane // H_A).astype(F32)
            sv = st * scale2 + (slope2 * kpos - qoff)
            s_scr[:, chunk(g)] = jnp.where(head_ok, sv, -jnp.inf)

    @pl.when(s == n_steps)
    def _():
        qb = q_ref[...].reshape(rows, DH_A).astype(BF16)
        nl = kmean_scr.shape[0]
        gate = lax.dot_general(qb, kmean_scr[...].astype(BF16), nt, preferred_element_type=F32)
        glane = lax.broadcasted_iota(jnp.int32, (rows, nl), 1)
        gate = jnp.where(glane % H_A == row_h, gate, -jnp.inf)
        sel = []
        for t in range(MOBA_TOPK):
            mx = jnp.max(gate, axis=1, keepdims=True)
            idx = jnp.min(jnp.where(gate == mx, glane, nl), axis=1, keepdims=True)
            sel.append(idx // H_A)
            gate = jnp.where(glane == idx, -jnp.inf, gate)

        def picked(g):
            blk = g // ppb
            return (sel[0] == blk) | (sel[1] == blk) | (sel[2] == blk)

        kn2 = kn_ref[...].reshape(rows, DH_A).astype(BF16)
        olane = lax.broadcasted_iota(jnp.int32, (1, rows), 1)
        own_pos = past_len + olane // H_A
        s_own = lax.dot_general(qb, kn2, nt, preferred_element_type=F32) * scale2
        s_own = s_own - slope2 * (qpos - own_pos).astype(F32)
        s_own = jnp.where(((olane % H_A) == row_h) & (own_pos <= qpos), s_own, -jnp.inf)

        def fold_lanes(x, op):
            acc = x[:, 0:128]
            for kk in range(1, cols // 128):
                acc = op(acc, x[:, kk * 128:(kk + 1) * 128])
            return acc

        def max_body(g, macc):
            pm = fold_lanes(s_scr[:, chunk(g)], jnp.maximum)
            return jnp.maximum(macc, jnp.where(picked(g), pm, -jnp.inf))

        macc = lax.fori_loop(0, n_pages, max_body, jnp.full((rows, 128), -jnp.inf, F32), unroll=2)
        m = jnp.maximum(jnp.max(macc, axis=1, keepdims=True), jnp.max(s_own, axis=1, keepdims=True))

        def exp_body(g, lacc):
            p = jnp.exp2(s_scr[:, chunk(g)] - jnp.where(picked(g), m, jnp.inf))
            p_scr[:, chunk(g)] = p.astype(BF16)
            return lacc + fold_lanes(p, jnp.add)

        p_own = jnp.exp2(s_own - m)
        lacc = lax.fori_loop(0, n_pages, exp_body, jnp.zeros((rows, 128), F32), unroll=2)
        l_scr[...] = jnp.sum(lacc, axis=1, keepdims=True) + jnp.sum(p_own, axis=1, keepdims=True)
        vn2 = vn_ref[...].reshape(rows, DH_A).astype(BF16)
        acc_scr[...] = jnp.dot(p_own.astype(BF16), vn2, preferred_element_type=F32)

    @pl.when(s >= n_steps)
    def _():
        acc = acc_scr[...]
        for p in range(pp):
            g = (s - n_steps) * pp + p
            v2 = v_refs[p][...].reshape(cols, DH_A).astype(BF16)
            acc = acc + jnp.dot(p_scr[:, chunk(g)], v2, preferred_element_type=F32)
        acc_scr[...] = acc

    @pl.when(s == 2 * n_steps - 1)
    def _():
        o = acc_scr[...] / l_scr[...]
        o = o * _silu(ga_ref[...].reshape(rows, DH_A))
        o_ref[...] = o.reshape(tq, H_A, DH_A)


def _sample_attn(cache_k4, cache_v4, page_table, slopes, qkvg5):
    b, n_pages = page_table.shape
    _, _, tq, _, _ = qkvg5.shape
    pp = PAGES_PER_STEP
    n_steps = n_pages // pp
    past_len = n_pages * PAGE_SIZE
    nbp = past_len // MOBA_BLOCK
    rows = tq * H_A

    def k_spec(p):
        return pl.BlockSpec((None, PAGE_SIZE, H_A, DH_A),
                            lambda bi, s, pt, sl: (pt[bi, jnp.minimum(s, n_steps - 1) * pp + p], 0, 0, 0))

    def v_spec(p):
        return pl.BlockSpec((None, PAGE_SIZE, H_A, DH_A),
                            lambda bi, s, pt, sl: (pt[bi, jnp.maximum(s - n_steps, 0) * pp + p], 0, 0, 0))

    def new_spec(sidx):
        return pl.BlockSpec((None, None, tq, H_A, DH_A), lambda bi, s, pt, sl: (sidx, bi, 0, 0, 0))

    grid_spec = pltpu.PrefetchScalarGridSpec(
        num_scalar_prefetch=2,
        grid=(b, 2 * n_steps),
        in_specs=[k_spec(p) for p in range(pp)] + [v_spec(p) for p in range(pp)]
        + [new_spec(0), new_spec(1), new_spec(2), new_spec(3)],
        out_specs=pl.BlockSpec((None, tq, H_A, DH_A), lambda bi, s, pt, sl: (bi, 0, 0, 0)),
        scratch_shapes=[
            pltpu.VMEM((rows, n_pages * PAGE_SIZE * H_A), F32),
            pltpu.VMEM((rows, n_pages * PAGE_SIZE * H_A), BF16),
            pltpu.VMEM((nbp * H_A, DH_A), F32),
            pltpu.VMEM((rows, DH_A), F32),
            pltpu.VMEM((rows, 1), F32),
        ],
    )
    return pl.pallas_call(
        functools.partial(_sample_attn_kernel, n_steps=n_steps, tq=tq, past_len=past_len),
        grid_spec=grid_spec,
        out_shape=jax.ShapeDtypeStruct((b, tq, H_A, DH_A), F32),
        compiler_params=_cparams(("parallel", "arbitrary")),
        name="sample_attn",
    )(page_table, slopes, *([cache_k4] * pp), *([cache_v4] * pp), qkvg5, qkvg5, qkvg5, qkvg5)


def _alibi_slopes(n_heads):
    return 2.0 ** (-8.0 * jnp.arange(1, n_heads + 1, dtype=F32) / n_heads)


def kernel(x_prompt, x_sample, cache_k, cache_v, state_pool, state_C, state_n, state_m, page_table,
           c_prompt, c_sample, ada_w, ada_b, norm_pre, norm_post, w_in_even, pool_w, pool_scale,
           w_out_even, w_in_odd, b_igate, b_fgate, head_norm, w_out_odd):
    bp, tp, d = x_prompt.shape
    bs, ts, _ = x_sample.shape
    n_pages = page_table.shape[1]
    past_len = n_pages * PAGE_SIZE
    w_a = H_A * DH_A
    w_c = H_C * DH_C
    rp, rs = bp * tp, bs * ts
    tm_p = 1024

    mods = _adaln(jnp.concatenate([c_prompt, c_sample], axis=0), ada_w, ada_b)

    def mod_parts(l):
        m = mods[l]
        shift, scale, gate = m[:, :d], m[:, d:2 * d], m[:, 2 * d:]
        prompt = tuple(a[:bp].reshape(bp, 1, d) for a in (shift, scale, gate))
        sample = tuple(jnp.repeat(a[bp:], ts, axis=0) for a in (shift, scale, gate))
        return prompt, sample

    slopes = _alibi_slopes(H_A)
    xp = x_prompt.reshape(rp, d)
    xs = x_sample.reshape(rs, d)

    (sh_p, sc_p, gt_p), (sh_s, sc_s, gt_s) = mod_parts(0)
    w_in0 = w_in_even[0].astype(BF16)
    zp, kv_p = _inproj(xp, norm_pre[0], sc_p, sh_p, w_in0, None, rows_per_group=tp, tm=tm_p, hm=(1, 2, H_A))
    zs, qkvg_s = _inproj(xs, norm_pre[0], sc_s, sh_s, w_in0, None, rows_per_group=ts, tm=rs, hm=(0, 4, H_A))
    kv_s = qkvg_s[1:3]
    z4p = zp.reshape(zp.shape[0], bp, tp, SEG)
    z4s = zs.reshape(zs.shape[0], bs, ts, SEG)

    pool_w_bf = pool_w[0].astype(BF16)
    att_p = _moba_prompt(z4p, slopes)
    pool_p, pstate_p = _pool(z4p, jnp.zeros((bp, POOL_HIST, SEG), F32), pool_w_bf, pool_scale[0],
                             pos0=0, tt=512, out_dtype=BF16)

    att_s = _sample_attn(cache_k[0], cache_v[0], page_table, slopes, qkvg_s.reshape(4, bs, ts, H_A, DH_A))
    pool_s, pstate_s = _pool(z4s, state_pool[0], pool_w_bf, pool_scale[0],
                             pos0=past_len, tt=ts, out_dtype=F32)

    w_out0 = w_out_even[0].astype(BF16)
    w_list0 = [w_out0[:w_a], w_out0[w_a:]]
    xp1 = _outproj([att_p.reshape(rp, w_a), pool_p.reshape(rp, SEG)], w_list0, xp, gt_p, norm_post[0],
                   rows_per_group=tp, tm=512)
    xs1 = _outproj([att_s.reshape(rs, w_a), pool_s.reshape(rs, SEG)], w_list0, xs, gt_s, norm_post[0],
                   rows_per_group=ts, tm=rs)

    (sh_p, sc_p, gt_p), (sh_s, sc_s, gt_s) = mod_parts(1)
    n_main = 5 * w_c
    w_in1 = w_in_odd[0][:, :n_main].astype(BF16)
    wg1 = jnp.pad(w_in_odd[0][:, n_main:], ((0, 0), (0, 128 - 2 * H_C))).astype(BF16)
    zp, gates_p = _inproj(xp1, norm_pre[1], sc_p, sh_p, w_in1, wg1, rows_per_group=tp, tm=tm_p)
    zs, gates_s = _inproj(xs1, norm_pre[1], sc_s, sh_s, w_in1, wg1, rows_per_group=ts, tm=rs)
    bias = jnp.stack([b_igate[0], b_fgate[0]]).astype(F32)
    hc_p, c_p, n_p, m_p = _mlstm(zp.reshape(zp.shape[0], bp, tp, SEG), gates_p.reshape(bp, tp, 128),
                                 bias, head_norm[0], None, chunk=256, lp=256)
    state = (state_C[0], state_n[0].reshape(bs, H_C, 1, DH_C), state_m[0])
    hc_s, c_s, n_s, m_s = _mlstm(zs.reshape(zs.shape[0], bs, ts, SEG), gates_s.reshape(bs, ts, 128),
                                 bias, head_norm[0], state, chunk=ts, lp=8)
    w_out1 = w_out_odd[0].astype(BF16)
    xp2 = _outproj([hc_p.reshape(rp, w_c)], [w_out1], xp1, gt_p, norm_post[1], rows_per_group=tp, tm=512)
    xs2 = _outproj([hc_s.reshape(rs, w_c)], [w_out1], xs1, gt_s, norm_post[1], rows_per_group=ts, tm=rs)

    kv_shape_p = (1, bp, tp, H_A, DH_A)
    kv_shape_s = (1, bs, ts, H_A, DH_A)
    return (xp2.reshape(bp, tp, d), xs2.reshape(bs, ts, d),
            kv_p[0].reshape(kv_shape_p), kv_p[1].reshape(kv_shape_p),
            kv_s[0].reshape(kv_shape_s), kv_s[1].reshape(kv_shape_s),
            pstate_p[None], pstate_s[None],
            c_p[None], n_p.reshape(1, bp, H_C, DH_C), m_p[:, :, 0, 0][None],
            c_s[None], n_s.reshape(1, bs, H_C, DH_C), m_s[:, :, 0, 0][None])
```

```python
import functools
import math

import jax
import jax.numpy as jnp
from jax import lax
from jax.experimental import pallas as pl
from jax.experimental.pallas import tpu as pltpu

F32 = jnp.float32
BF16 = jnp.bfloat16

H_A = 8
DH_A = 128
MOBA_BLOCK = 256
MOBA_TOPK = 3
POOL_WINDOWS = (2, 4, 8, 16)
GW_B = 256
POOL_HIST = max(POOL_WINDOWS) - 1
H_C = 8
DH_C = 256
EPS = 1e-6
PAGE_SIZE = 128

SEG = 1024
NEG_BIG = -1e30

VMEM_LIMIT_V7X = 52 * 1024 * 1024


def _sigmoid(x):
    return 0.5 * jnp.tanh(0.5 * x) + 0.5


def _silu(x):
    return x * _sigmoid(x)


def _log_sigmoid(x):
    return -(jnp.maximum(-x, 0.0) + jnp.log(1.0 + jnp.exp(-jnp.abs(x))))


def _cparams(sem):
    return pltpu.CompilerParams(dimension_semantics=sem, vmem_limit_bytes=VMEM_LIMIT_V7X)


def _adaln_kernel(c_ref, w_ref, b_ref, o_ref):
    s = _silu(c_ref[...]).astype(BF16)
    o_ref[...] = jnp.dot(s, w_ref[...].astype(BF16), preferred_element_type=F32) + b_ref[...]


def _adaln(c_all, ada_w, ada_b, tn=512):
    depth, d, n = ada_w.shape
    r = c_all.shape[0]
    return pl.pallas_call(
        _adaln_kernel,
        grid=(depth, n // tn),
        in_specs=[
            pl.BlockSpec((r, d), lambda l, j: (0, 0)),
            pl.BlockSpec((None, d, tn), lambda l, j: (l, 0, j)),
            pl.BlockSpec((None, 1, tn), lambda l, j: (l, 0, j)),
        ],
        out_specs=pl.BlockSpec((None, r, tn), lambda l, j: (l, 0, j)),
        out_shape=jax.ShapeDtypeStruct((depth, r, n), F32),
        compiler_params=_cparams(("parallel", "parallel")),
        name="adaln",
    )(c_all, ada_w, ada_b.reshape(depth, 1, n))


def _inproj_kernel(*refs, has_gate, hm):
    hm_ref = gt_ref = wg_ref = None
    if has_gate:
        x_ref, g_ref, sc_ref, sh_ref, w_ref, wg_ref, z_ref, gt_ref, h_scr = refs
    elif hm:
        x_ref, g_ref, sc_ref, sh_ref, w_ref, z_ref, hm_ref, h_scr = refs
    else:
        x_ref, g_ref, sc_ref, sh_ref, w_ref, z_ref, h_scr = refs
    j = pl.program_id(1)

    @pl.when(pl.program_id(1) == 0)
    def _():
        x = x_ref[...]
        r = x * lax.rsqrt(jnp.mean(x * x, axis=-1, keepdims=True) + EPS)
        h = (r * g_ref[...]) * (1.0 + sc_ref[...]) + sh_ref[...]
        h_scr[...] = h.astype(BF16)
        if has_gate:
            gt_ref[...] = jnp.dot(h_scr[...], wg_ref[...], preferred_element_type=F32)

    res = jnp.dot(h_scr[...], w_ref[...], preferred_element_type=F32)
    z_ref[...] = res

    if hm:
        first, count, heads = hm

        @pl.when((j >= first) & (j < first + count))
        def _():
            dh = SEG // heads
            for hh in range(heads):
                hm_ref[:, hh, :] = res[:, hh * dh:(hh + 1) * dh]


def _inproj(x2, g_pre, scale, shift, w_bf, wg_bf, *, rows_per_group, tm, hm=None):
    r, d = x2.shape
    n = w_bf.shape[1]
    nseg = n // SEG
    has_gate = wg_bf is not None
    if scale.ndim == 3:
        per = rows_per_group // tm
        mod_spec = pl.BlockSpec((None, 1, d), lambda i, j: (i // per, 0, 0))
    else:
        mod_spec = pl.BlockSpec((tm, d), lambda i, j: (i, 0))
    in_specs = [
        pl.BlockSpec((tm, d), lambda i, j: (i, 0), pipeline_mode=pl.Buffered(1)),
        pl.BlockSpec((1, d), lambda i, j: (0, 0)),
        mod_spec,
        mod_spec,
        pl.BlockSpec((d, SEG), lambda i, j: (0, j)),
    ]
    args = [x2, g_pre.reshape(1, d), scale, shift, w_bf]
    out_specs = [pl.BlockSpec((None, tm, SEG), lambda i, j: (j, i, 0))]
    out_shape = [jax.ShapeDtypeStruct((nseg, r, SEG), F32)]
    if has_gate:
        in_specs.append(pl.BlockSpec((d, 128), lambda i, j: (0, 0)))
        args.append(wg_bf)
        out_specs.append(pl.BlockSpec((tm, 128), lambda i, j: (i, 0)))
        out_shape.append(jax.ShapeDtypeStruct((r, 128), F32))
    if hm:
        first, count, heads = hm
        out_specs.append(pl.BlockSpec((None, tm, heads, SEG // heads),
                                      lambda i, j: (jnp.clip(j - first, 0, count - 1), i, 0, 0)))
        out_shape.append(jax.ShapeDtypeStruct((count, r, heads, SEG // heads), F32))
    res = pl.pallas_call(
        functools.partial(_inproj_kernel, has_gate=has_gate, hm=hm),
        grid=(r // tm, nseg),
        in_specs=in_specs,
        out_specs=out_specs,
        out_shape=out_shape,
        scratch_shapes=[pltpu.VMEM((tm, d), BF16)],
        compiler_params=_cparams(("parallel", "arbitrary")),
        name="inproj_gate" if has_gate else "inproj",
    )(*args)
    return (res[0], res[1]) if (has_gate or hm) else (res[0], None)


def _outproj_kernel(*refs, n_a):
    a_refs = refs[:n_a]
    w_refs = refs[n_a:2 * n_a]
    x_ref, gate_ref, gp_ref, o_ref = refs[2 * n_a:]
    y = None
    for a_ref, w_ref in zip(a_refs, w_refs):
        t = jnp.dot(a_ref[...].astype(BF16), w_ref[...], preferred_element_type=F32)
        y = t if y is None else y + t
    r = y * lax.rsqrt(jnp.mean(y * y, axis=-1, keepdims=True) + EPS)
    o_ref[...] = x_ref[...] + gate_ref[...] * (r * gp_ref[...])


def _outproj(a_list, w_list, x2, gate, g_post, *, rows_per_group, tm):
    r, d = x2.shape
    n_a = len(a_list)
    if gate.ndim == 3:
        per = rows_per_group // tm
        gate_spec = pl.BlockSpec((None, 1, d), lambda i: (i // per, 0, 0))
    else:
        gate_spec = pl.BlockSpec((tm, d), lambda i: (i, 0))
    in_specs = [pl.BlockSpec((tm, a.shape[1]), lambda i: (i, 0)) for a in a_list]
    in_specs += [pl.BlockSpec(w.shape, lambda i: (0, 0)) for w in w_list]
    in_specs += [pl.BlockSpec((tm, d), lambda i: (i, 0)), gate_spec, pl.BlockSpec((1, d), lambda i: (0, 0))]
    return pl.pallas_call(
        functools.partial(_outproj_kernel, n_a=n_a),
        grid=(r // tm,),
        in_specs=in_specs,
        out_specs=pl.BlockSpec((tm, d), lambda i: (i, 0)),
        out_shape=jax.ShapeDtypeStruct((r, d), F32),
        compiler_params=_cparams(("parallel",)),
        name="outproj",
    )(*a_list, *w_list, x2, gate, g_post.reshape(1, d))


MOBA_AUX = 128
ALIBI_COL = 8


def _moba_kernel(slopes_ref, q_ref, k_ref, v_ref, ga_ref, o_ref, kaug_scr, vt_scr, *, nb):
    h = pl.program_id(1)
    blk = MOBA_BLOCK
    t = nb * blk
    log2e = math.log2(math.e)
    scale2 = DH_A ** -0.5 * log2e
    nt = (((1,), (1,)), ((), ()))

    k = k_ref[...]
    kmean_b = jnp.concatenate(
        [jnp.mean(k[j * blk:(j + 1) * blk, :], axis=0, keepdims=True) for j in range(nb)], axis=0).astype(BF16)
    pos = lax.broadcasted_iota(jnp.int32, (t, MOBA_AUX), 0)
    lane = lax.broadcasted_iota(jnp.int32, (t, MOBA_AUX), 1)
    a = (slopes_ref[h] * log2e) * pos.astype(F32)
    a_hi = a.astype(BF16).astype(F32)
    a_mid = (a - a_hi).astype(BF16).astype(F32)
    a_lo = a - a_hi - a_mid
    aux = jnp.where(lane == ALIBI_COL, a_hi,
                    jnp.where(lane == ALIBI_COL + 1, a_mid,
                              jnp.where(lane == ALIBI_COL + 2, a_lo,
                                        jnp.where(lane == pos // blk, 1.0, 0.0))))
    kaug_scr[...] = jnp.concatenate([k.astype(BF16), aux.astype(BF16)], axis=1)
    vt_scr[...] = jnp.transpose(v_ref[...]).astype(BF16)

    ones_rows = jnp.where(lax.broadcasted_iota(jnp.int32, (8, blk), 0) < 3, 1.0, 0.0)
    rowb = lax.broadcasted_iota(jnp.int32, (nb, blk), 0)

    def attend(n):
        nk = (n + 1) * blk
        q = q_ref[n * blk:nk, :]
        if n > MOBA_TOPK:
            gate = lax.dot_general(kmean_b, q.astype(BF16), nt, preferred_element_type=F32)
            gm = jnp.where(rowb < n, gate, -jnp.inf)
            cnt = jnp.zeros((nb, blk), F32)
            for kk in range(n):
                gk = gm[kk:kk + 1, :]
                beats = (gk > gm) | ((gk == gm) & (rowb > kk))
                cnt = cnt + jnp.where(beats, 1.0, 0.0)
            keep = ((rowb < n) & (cnt < MOBA_TOPK)) | (rowb == n)
            bias = jnp.where(keep, 0.0, NEG_BIG)
        else:
            bias = jnp.zeros((nb, blk), F32)
        pieces = [bias, ones_rows]
        if nb < ALIBI_COL:
            pieces.insert(1, jnp.zeros((ALIBI_COL - nb, blk), F32))
        pieces.append(jnp.zeros((MOBA_AUX - ALIBI_COL - 8, blk), F32))
        rhs = jnp.concatenate([(jnp.transpose(q) * scale2).astype(BF16),
                               jnp.concatenate(pieces, axis=0).astype(BF16)], axis=0)
        s = jnp.dot(kaug_scr[0:nk, :], rhs, preferred_element_type=F32)
        krow = lax.broadcasted_iota(jnp.int32, (blk, blk), 0)
        qcol = lax.broadcasted_iota(jnp.int32, (blk, blk), 1)
        s_own = jnp.where(krow <= qcol, s[n * blk:nk, :], -jnp.inf)
        m = jnp.max(s_own, axis=0, keepdims=True)
        if n > 0:
            m = jnp.maximum(m, jnp.max(s[0:n * blk, :], axis=0, keepdims=True))
        p_own = jnp.exp2(s_own - m)
        l = jnp.sum(p_own, axis=0, keepdims=True)
        ot = jnp.dot(vt_scr[:, n * blk:nk], p_own.astype(BF16), preferred_element_type=F32)
        if n > 0:
            p = jnp.exp2(s[0:n * blk, :] - m)
            l = l + jnp.sum(p, axis=0, keepdims=True)
            ot = ot + jnp.dot(vt_scr[:, 0:n * blk], p.astype(BF16), preferred_element_type=F32)
        o = jnp.transpose(ot / l)
        o_ref[n * blk:nk, :] = (o * _silu(ga_ref[n * blk:nk, :])).astype(o_ref.dtype)

    for n in range(nb):
        attend(n)


def _moba_prompt(z4, slopes):
    _, b, t, w = z4.shape
    nb = t // MOBA_BLOCK
    assert nb <= ALIBI_COL

    def seg_spec(sidx):
        return pl.BlockSpec((None, None, t, DH_A), lambda bi, h, s: (sidx, bi, 0, h))

    grid_spec = pltpu.PrefetchScalarGridSpec(
        num_scalar_prefetch=1,
        grid=(b, H_A),
        in_specs=[seg_spec(0), seg_spec(1), seg_spec(2), seg_spec(3)],
        out_specs=pl.BlockSpec((None, t, DH_A), lambda bi, h, s: (bi, 0, h)),
        scratch_shapes=[
            pltpu.VMEM((t, DH_A + MOBA_AUX), BF16),
            pltpu.VMEM((DH_A, t), BF16),
        ],
    )
    return pl.pallas_call(
        functools.partial(_moba_kernel, nb=nb),
        grid_spec=grid_spec,
        out_shape=jax.ShapeDtypeStruct((b, t, w), BF16),
        compiler_params=_cparams(("parallel", "parallel")),
        name="moba_prompt",
    )(slopes, z4, z4, z4, z4)


def _pool_kernel(u_ref, gb_ref, prev_ref, pw_ref, ps_ref, o_ref, pn_ref, ext_scr, *, tt, pos0, nt):
    t = pl.program_id(1)
    hist = POOL_HIST + 1

    @pl.when(t == 0)
    def _():
        ext_scr[0:1, :] = jnp.zeros((1, ext_scr.shape[1]), F32)
        ext_scr[1:hist, :] = prev_ref[...]

    ext_scr[hist:hist + tt, :] = u_ref[...]
    pos = pos0 + t * tt + lax.broadcasted_iota(jnp.int32, (tt, 1), 0)
    for g, w in enumerate(POOL_WINDOWS):
        sl = slice(g * GW_B, (g + 1) * GW_B)
        x = u_ref[:, sl]
        acc = x
        for jj in range(1, w):
            acc = acc + ext_scr[hist - jj:hist - jj + tt, sl]
        cnt = jnp.minimum(w, pos + 1).astype(F32)
        d = acc / cnt - x
        y = jnp.dot(d.astype(BF16), pw_ref[g], preferred_element_type=F32) * ps_ref[:, sl]
        o_ref[:, sl] = (y * _silu(gb_ref[:, sl])).astype(o_ref.dtype)

    tail = ext_scr[tt:tt + hist, :]
    ext_scr[0:hist, :] = tail

    @pl.when(t == nt - 1)
    def _():
        pn_ref[...] = ext_scr[1:hist, :]


def _pool(z4, prev, pool_w_bf, pool_scale, *, pos0, tt, out_dtype):
    _, b, t, w = z4.shape
    nt = t // tt
    return pl.pallas_call(
        functools.partial(_pool_kernel, tt=tt, pos0=pos0, nt=nt),
        grid=(b, nt),
        in_specs=[
            pl.BlockSpec((None, None, tt, w), lambda bi, ti: (4, bi, ti, 0)),
            pl.BlockSpec((None, None, tt, w), lambda bi, ti: (5, bi, ti, 0)),
            pl.BlockSpec((None, POOL_HIST, w), lambda bi, ti: (bi, 0, 0)),
            pl.BlockSpec(pool_w_bf.shape, lambda bi, ti: (0, 0, 0)),
            pl.BlockSpec((1, w), lambda bi, ti: (0, 0)),
        ],
        out_specs=[
            pl.BlockSpec((None, tt, w), lambda bi, ti: (bi, ti, 0)),
            pl.BlockSpec((None, POOL_HIST, w), lambda bi, ti: (bi, 0, 0)),
        ],
        out_shape=[
            jax.ShapeDtypeStruct((b, t, w), out_dtype),
            jax.ShapeDtypeStruct((b, POOL_HIST, w), F32),
        ],
        scratch_shapes=[pltpu.VMEM((POOL_HIST + 1 + tt, w), F32)],
        compiler_params=_cparams(("parallel", "arbitrary")),
        name="pool",
    )(z4, z4, prev, pool_w_bf, pool_scale.reshape(1, w))


def _mlstm_kernel(*refs, lv, lp, hps, zero_state):
    if zero_state:
        (bias_ref, q_ref, k_ref, v_ref, og_ref, g_ref, gt_ref, hn_ref,
         h_out, c_out, n_out, m_out, c_scr, n_scr, m_scr, pad_scr) = refs
    else:
        (bias_ref, m0_ref, q_ref, k_ref, v_ref, og_ref, g_ref, gt_ref, hn_ref, c0_ref, n0_ref,
         h_out, c_out, n_out, m_out, c_scr, n_scr, m_scr, pad_scr) = refs
    bi = pl.program_id(0)
    hg = pl.program_id(1)
    c = pl.program_id(2)
    nc = pl.num_programs(2)

    @pl.when(c == 0)
    def _():
        if zero_state:
            c_scr[...] = jnp.zeros(c_scr.shape, F32)
            n_scr[...] = jnp.zeros(n_scr.shape, F32)
            m_scr[...] = jnp.zeros(m_scr.shape, F32)
        else:
            c_scr[...] = c0_ref[...]
            n_scr[...] = n0_ref[...]
            for hh in range(hps):
                m_scr[hh] = jnp.full(m_scr.shape[1:], m0_ref[bi, hg * hps + hh], F32)

    def load(ref, slot):
        if lv == lp:
            return ref
        pad_scr[slot] = jnp.zeros(pad_scr.shape[1:], F32)
        pad_scr[slot, 0:lv, 0:ref.shape[1]] = ref[...]
        return pad_scr.at[slot]

    q_src, k_src, v_src, og_src, g_src, gt_src = (load(r, i) for i, r in
                                                  enumerate((q_ref, k_ref, v_ref, og_ref, g_ref, gt_ref)))
    gts = gt_src[:, 0:128]
    lane = lax.broadcasted_iota(jnp.int32, gts.shape, 1)
    row1 = lax.broadcasted_iota(jnp.int32, (lp, 1), 0)
    rr = lax.broadcasted_iota(jnp.int32, (lp, lp), 0)
    cc = lax.broadcasted_iota(jnp.int32, (lp, lp), 1)
    eye = rr == cc
    tril = rr >= cc
    nt = (((1,), (1,)), ((), ()))

    def to_row(col):
        return jnp.sum(jnp.where(eye, col, 0.0), axis=0, keepdims=True)

    for hh in range(hps):
        h = hg * hps + hh
        sl = slice(hh * DH_C, (hh + 1) * DH_C)
        q = q_src[:, sl]
        k = k_src[:, sl] * (DH_C ** -0.5)
        v = v_src[:, sl]

        ig_col = jnp.sum(jnp.where(lane == h, gts, 0.0), axis=1, keepdims=True) + bias_ref[0, h]
        gf_col = jnp.sum(jnp.where(lane == H_C + h, gts, 0.0), axis=1, keepdims=True) + bias_ref[1, h]
        lf_col = _log_sigmoid(gf_col)
        if lv != lp:
            ig_col = jnp.where(row1 < lv, ig_col, NEG_BIG)
            lf_col = jnp.where(row1 < lv, lf_col, 0.0)

        lf_row = to_row(lf_col)
        ig_row = to_row(ig_col)
        b_col = jnp.sum(jnp.where(tril, lf_row, 0.0), axis=1, keepdims=True)
        b_row = to_row(b_col)
        dmat = jnp.where(tril, b_col - b_row + ig_row, -jnp.inf)
        m_prev = m_scr[hh]
        carry = b_col + m_prev
        mt = jnp.maximum(carry, jnp.max(dmat, axis=1, keepdims=True))

        qb = q.astype(BF16)
        kb = k.astype(BF16)
        vb = v.astype(BF16)
        s = lax.dot_general(qb, kb, nt, preferred_element_type=F32) * jnp.exp(dmat - mt)
        inter = jnp.exp(carry - mt)
        cq = lax.dot_general(qb, c_scr[hh].astype(BF16), nt, preferred_element_type=F32)
        num = jnp.dot(s.astype(BF16), vb, preferred_element_type=F32) + inter * cq
        nq = jnp.sum(q * n_scr[hh], axis=1, keepdims=True)
        den = jnp.sum(s, axis=1, keepdims=True) + inter * nq
        hc = num / jnp.maximum(jnp.abs(den), jnp.exp(-mt))

        m_new = mt[lp - 1:lp, :]
        b_last = b_col[lp - 1:lp, :]
        w_col = jnp.exp(b_last - b_col + ig_col - m_new)
        decay = jnp.exp(b_last + m_prev - m_new)
        vw = (v * w_col).astype(BF16)
        c_scr[hh] = decay * c_scr[hh] + lax.dot_general(
            vw, kb, (((0,), (0,)), ((), ())), preferred_element_type=F32)
        n_scr[hh] = decay * n_scr[hh] + jnp.sum(k * w_col, axis=0, keepdims=True)
        m_scr[hh] = m_new

        hc = hc * _sigmoid(og_src[:, sl])
        hc = hc * lax.rsqrt(jnp.mean(hc * hc, axis=-1, keepdims=True) + EPS) * hn_ref[:, sl]
        res = (hc * _silu(g_src[:, sl])).astype(h_out.dtype)
        h_out[:, sl] = res if lv == lp else res[0:lv, :]

    @pl.when(c == nc - 1)
    def _():
        c_out[...] = c_scr[...]
        n_out[...] = n_scr[...]
        m_out[...] = jnp.broadcast_to(m_scr[...], m_out.shape)


def _mlstm(z4, gates3, bias, head_norm, state, *, chunk, lp):
    _, b, t, _ = z4.shape
    nc = t // chunk
    hps = SEG // DH_C
    ngrp = H_C // hps
    zero_state = state is None
    n_pref = 1 if zero_state else 2

    def seg_spec(base):
        return pl.BlockSpec((None, None, chunk, SEG), lambda bi, hg, c, *_: (base + hg, bi, c, 0))

    def state_spec(*tail):
        return pl.BlockSpec((None, hps) + tail, lambda bi, hg, c, *_: (bi, hg) + (0,) * len(tail))

    in_specs = [seg_spec(0), seg_spec(2), seg_spec(4), seg_spec(6), seg_spec(8),
                pl.BlockSpec((None, chunk, 128), lambda bi, hg, c, *_: (bi, c, 0)),
                pl.BlockSpec((1, SEG), lambda bi, hg, c, *_: (0, hg))]
    args = [z4, z4, z4, z4, z4, gates3, head_norm.reshape(1, H_C * DH_C)]
    prefetch = [bias]
    if not zero_state:
        c0, n0, m0 = state
        prefetch.append(m0)
        in_specs += [state_spec(DH_C, DH_C), state_spec(1, DH_C)]
        args += [c0, n0]
    grid_spec = pltpu.PrefetchScalarGridSpec(
        num_scalar_prefetch=n_pref,
        grid=(b, ngrp, nc),
        in_specs=in_specs,
        out_specs=[
            pl.BlockSpec((None, chunk, SEG), lambda bi, hg, c, *_: (bi, c, hg)),
            state_spec(DH_C, DH_C),
            state_spec(1, DH_C),
            state_spec(1, 128),
        ],
        scratch_shapes=[
            pltpu.VMEM((hps, DH_C, DH_C), F32),
            pltpu.VMEM((hps, 1, DH_C), F32),
            pltpu.VMEM((hps, 1, 1), F32),
            pltpu.VMEM((6, lp, SEG), F32),
        ],
    )
    return pl.pallas_call(
        functools.partial(_mlstm_kernel, lv=chunk, lp=lp, hps=hps, zero_state=zero_state),
        grid_spec=grid_spec,
        out_shape=[
            jax.ShapeDtypeStruct((b, t, H_C * DH_C), BF16),
            jax.ShapeDtypeStruct((b, H_C, DH_C, DH_C), F32),
            jax.ShapeDtypeStruct((b, H_C, 1, DH_C), F32),
            jax.ShapeDtypeStruct((b, H_C, 1, 128), F32),
        ],
        compiler_params=_cparams(("parallel", "parallel", "arbitrary")),
        name="mlstm",
    )(*prefetch, *args)


PAGES_PER_STEP = 16

def _sample_attn_kernel(pt_ref, slopes_ref, *refs, n_steps, tq, past_len):
    pp = PAGES_PER_STEP
    k_refs = refs[:pp]
    v_refs = refs[pp:2 * pp]
    qkvg_refs = refs[2 * pp:2 * pp + 4]
    o_ref, s_scr, pmax_scr, kmean_scr, acc_scr, l_scr, m_scr, sel_scr = refs[2 * pp + 4:]
    q_ref, kn_ref, vn_ref, ga_ref = qkvg_refs
    s = pl.program_id(1)
    rows = tq * H_A
    cols = PAGE_SIZE * H_A
    ppb = MOBA_BLOCK // PAGE_SIZE
    n_pages = n_steps * pp
    nt = (((1,), (1,)), ((), ()))
    log2e = math.log2(math.e)
    scale2 = DH_A ** -0.5 * log2e

    row = lax.broadcasted_iota(jnp.int32, (rows, 1), 0)
    row_h = row % H_A
    qpos = past_len + row // H_A
    slope2 = jnp.zeros((rows, 1), F32)
    for hh in range(H_A):
        slope2 = jnp.where(row_h == hh, slopes_ref[hh] * log2e, slope2)
    lane = lax.broadcasted_iota(jnp.int32, (1, cols), 1)

    def chunk(g):
        return pl.ds(pl.multiple_of(g * cols, cols), cols)

    def fold_lanes(x, op):
        acc = x[:, 0:128]
        for kk in range(1, cols // 128):
            acc = op(acc, x[:, kk * 128:(kk + 1) * 128])
        return acc

    def picked(sel, g):
        blk = g // ppb
        return (sel[0] == blk) | (sel[1] == blk) | (sel[2] == blk)

    @pl.when(s < n_steps)
    def _():
        qb = q_ref[...].reshape(rows, DH_A).astype(BF16)
        for jj in range(pp // ppb):
            tot = None
            for p in range(ppb):
                part = jnp.sum(k_refs[jj * ppb + p][...], axis=0)
                tot = part if tot is None else tot + part
            kmean_scr[pl.ds(pl.multiple_of((s * (pp // ppb) + jj) * H_A, H_A), H_A), :] = tot * (1.0 / MOBA_BLOCK)
        head_ok = (lane % H_A) == row_h
        qoff = slope2 * qpos.astype(F32)
        for p in range(pp):
            g = s * pp + p
            k2 = k_refs[p][...].reshape(cols, DH_A).astype(BF16)
            st = lax.dot_general(qb, k2, nt, preferred_element_type=F32)
            kpos = (g * PAGE_SIZE + lane // H_A).astype(F32)
            sv = st * scale2 + (slope2 * kpos - qoff)
            sv = jnp.where(head_ok, sv, -jnp.inf)
            s_scr[:, chunk(g)] = sv
            pmax_scr[g] = fold_lanes(sv, jnp.maximum)

    @pl.when(s == n_steps)
    def _():
        qb = q_ref[...].reshape(rows, DH_A).astype(BF16)
        nl = kmean_scr.shape[0]
        gate = lax.dot_general(qb, kmean_scr[...].astype(BF16), nt, preferred_element_type=F32)
        glane = lax.broadcasted_iota(jnp.int32, (rows, nl), 1)
        gate = jnp.where(glane % H_A == row_h, gate, -jnp.inf)
        sel = []
        sel_lane = lax.broadcasted_iota(jnp.int32, sel_scr.shape, 1)
        sel_tile = jnp.zeros(sel_scr.shape, jnp.int32)
        for t in range(MOBA_TOPK):
            mx = jnp.max(gate, axis=1, keepdims=True)
            idx = jnp.min(jnp.where(gate == mx, glane, nl), axis=1, keepdims=True)
            sel.append(idx // H_A)
            sel_tile = jnp.where(sel_lane == t, idx // H_A, sel_tile)
            gate = jnp.where(glane == idx, -jnp.inf, gate)
        sel_scr[...] = sel_tile

        kn2 = kn_ref[...].reshape(rows, DH_A).astype(BF16)
        olane = lax.broadcasted_iota(jnp.int32, (1, rows), 1)
        own_pos = past_len + olane // H_A
        s_own = lax.dot_general(qb, kn2, nt, preferred_element_type=F32) * scale2
        s_own = s_own - slope2 * (qpos - own_pos).astype(F32)
        s_own = jnp.where(((olane % H_A) == row_h) & (own_pos <= qpos), s_own, -jnp.inf)

        def max_body(g, macc):
            return jnp.maximum(macc, jnp.where(picked(sel, g), pmax_scr[g], -jnp.inf))

        macc = lax.fori_loop(0, n_pages, max_body, jnp.full((rows, 128), -jnp.inf, F32), unroll=4)
        m = jnp.maximum(jnp.max(macc, axis=1, keepdims=True), jnp.max(s_own, axis=1, keepdims=True))
        m_scr[...] = m
        p_own = jnp.exp2(s_own - m)
        l_scr[...] = jnp.where(lax.broadcasted_iota(jnp.int32, l_scr.shape, 1) == 0,
                               jnp.sum(p_own, axis=1, keepdims=True), 0.0)
        vn2 = vn_ref[...].reshape(rows, DH_A).astype(BF16)
        acc_scr[...] = jnp.dot(p_own.astype(BF16), vn2, preferred_element_type=F32)

    @pl.when(s >= n_steps)
    def _():
        sel = [sel_scr[:, t:t + 1] for t in range(MOBA_TOPK)]
        m = m_scr[...]
        acc = acc_scr[...]
        lacc = l_scr[...]
        for p in range(pp):
            g = (s - n_steps) * pp + p
            pr = jnp.exp2(s_scr[:, chunk(g)] - jnp.where(picked(sel, g), m, jnp.inf))
            lacc = lacc + fold_lanes(pr, jnp.add)
            v2 = v_refs[p][...].reshape(cols, DH_A).astype(BF16)
            acc = acc + jnp.dot(pr.astype(BF16), v2, preferred_element_type=F32)
        acc_scr[...] = acc
        l_scr[...] = lacc

    @pl.when(s == 2 * n_steps - 1)
    def _():
        o = acc_scr[...] / jnp.sum(l_scr[...], axis=1, keepdims=True)
        o = o * _silu(ga_ref[...].reshape(rows, DH_A))
        o_ref[...] = o.reshape(tq, H_A, DH_A)


def _sample_attn(cache_k4, cache_v4, page_table, slopes, qkvg5):
    b, n_pages = page_table.shape
    _, _, tq, _, _ = qkvg5.shape
    pp = PAGES_PER_STEP
    n_steps = n_pages // pp
    past_len = n_pages * PAGE_SIZE
    nbp = past_len // MOBA_BLOCK
    rows = tq * H_A

    def k_spec(p):
        return pl.BlockSpec((None, PAGE_SIZE, H_A, DH_A),
                            lambda bi, s, pt, sl: (pt[bi, jnp.minimum(s, n_steps - 1) * pp + p], 0, 0, 0))

    def v_spec(p):
        return pl.BlockSpec((None, PAGE_SIZE, H_A, DH_A),
                            lambda bi, s, pt, sl: (pt[bi, jnp.maximum(s - n_steps, 0) * pp + p], 0, 0, 0))

    def new_spec(sidx):
        return pl.BlockSpec((None, None, tq, H_A, DH_A), lambda bi, s, pt, sl: (sidx, bi, 0, 0, 0))

    grid_spec = pltpu.PrefetchScalarGridSpec(
        num_scalar_prefetch=2,
        grid=(b, 2 * n_steps),
        in_specs=[k_spec(p) for p in range(pp)] + [v_spec(p) for p in range(pp)]
        + [new_spec(0), new_spec(1), new_spec(2), new_spec(3)],
        out_specs=pl.BlockSpec((None, tq, H_A, DH_A), lambda bi, s, pt, sl: (bi, 0, 0, 0)),
        scratch_shapes=[
            pltpu.VMEM((rows, n_pages * PAGE_SIZE * H_A), F32),
            pltpu.VMEM((n_pages, rows, 128), F32),
            pltpu.VMEM((nbp * H_A, DH_A), F32),
            pltpu.VMEM((rows, DH_A), F32),
            pltpu.VMEM((rows, 128), F32),
            pltpu.VMEM((rows, 1), F32),
            pltpu.VMEM((rows, 128), jnp.int32),
        ],
    )
    return pl.pallas_call(
        functools.partial(_sample_attn_kernel, n_steps=n_steps, tq=tq, past_len=past_len),
        grid_spec=grid_spec,
        out_shape=jax.ShapeDtypeStruct((b, tq, H_A, DH_A), F32),
        compiler_params=_cparams(("parallel", "arbitrary")),
        name="sample_attn",
    )(page_table, slopes, *([cache_k4] * pp), *([cache_v4] * pp), qkvg5, qkvg5, qkvg5, qkvg5)


def _alibi_slopes(n_heads):
    return 2.0 ** (-8.0 * jnp.arange(1, n_heads + 1, dtype=F32) / n_heads)


def kernel(x_prompt, x_sample, cache_k, cache_v, state_pool, state_C, state_n, state_m, page_table,
           c_prompt, c_sample, ada_w, ada_b, norm_pre, norm_post, w_in_even, pool_w, pool_scale,
           w_out_even, w_in_odd, b_igate, b_fgate, head_norm, w_out_odd):
    bp, tp, d = x_prompt.shape
    bs, ts, _ = x_sample.shape
    n_pages = page_table.shape[1]
    past_len = n_pages * PAGE_SIZE
    w_a = H_A * DH_A
    w_c = H_C * DH_C
    rp, rs = bp * tp, bs * ts
    tm_p = 1024

    mods = _adaln(jnp.concatenate([c_prompt, c_sample], axis=0), ada_w, ada_b)

    def mod_parts(l):
        m = mods[l]
        shift, scale, gate = m[:, :d], m[:, d:2 * d], m[:, 2 * d:]
        prompt = tuple(a[:bp].reshape(bp, 1, d) for a in (shift, scale, gate))
        sample = tuple(jnp.repeat(a[bp:], ts, axis=0) for a in (shift, scale, gate))
        return prompt, sample

    slopes = _alibi_slopes(H_A)
    xp = x_prompt.reshape(rp, d)
    xs = x_sample.reshape(rs, d)

    (sh_p, sc_p, gt_p), (sh_s, sc_s, gt_s) = mod_parts(0)
    w_in0 = w_in_even[0].astype(BF16)
    zp, kv_p = _inproj(xp, norm_pre[0], sc_p, sh_p, w_in0, None, rows_per_group=tp, tm=tm_p, hm=(1, 2, H_A))
    zs, qkvg_s = _inproj(xs, norm_pre[0], sc_s, sh_s, w_in0, None, rows_per_group=ts, tm=rs, hm=(0, 4, H_A))
    kv_s = qkvg_s[1:3]
    z4p = zp.reshape(zp.shape[0], bp, tp, SEG)
    z4s = zs.reshape(zs.shape[0], bs, ts, SEG)

    pool_w_bf = pool_w[0].astype(BF16)
    att_p = _moba_prompt(z4p, slopes)
    pool_p, pstate_p = _pool(z4p, jnp.zeros((bp, POOL_HIST, SEG), F32), pool_w_bf, pool_scale[0],
                             pos0=0, tt=512, out_dtype=BF16)

    att_s = _sample_attn(cache_k[0], cache_v[0], page_table, slopes, qkvg_s.reshape(4, bs, ts, H_A, DH_A))
    pool_s, pstate_s = _pool(z4s, state_pool[0], pool_w_bf, pool_scale[0],
                             pos0=past_len, tt=ts, out_dtype=F32)

    w_out0 = w_out_even[0].astype(BF16)
    w_list0 = [w_out0[:w_a], w_out0[w_a:]]
    xp1 = _outproj([att_p.reshape(rp, w_a), pool_p.reshape(rp, SEG)], w_list0, xp, gt_p, norm_post[0],
                   rows_per_group=tp, tm=512)
    xs1 = _outproj([att_s.reshape(rs, w_a), pool_s.reshape(rs, SEG)], w_list0, xs, gt_s, norm_post[0],
                   rows_per_group=ts, tm=rs)

    (sh_p, sc_p, gt_p), (sh_s, sc_s, gt_s) = mod_parts(1)
    n_main = 5 * w_c
    w_in1 = w_in_odd[0].astype(BF16)
    wg1 = jnp.pad(w_in_odd[0][:, n_main:], ((0, 0), (0, 128 - 2 * H_C))).astype(BF16)
    zp, gates_p = _inproj(xp1, norm_pre[1], sc_p, sh_p, w_in1, wg1, rows_per_group=tp, tm=tm_p)
    zs, gates_s = _inproj(xs1, norm_pre[1], sc_s, sh_s, w_in1, wg1, rows_per_group=ts, tm=rs)
    bias = jnp.stack([b_igate[0], b_fgate[0]]).astype(F32)
    hc_p, c_p, n_p, m_p = _mlstm(zp.reshape(zp.shape[0], bp, tp, SEG), gates_p.reshape(bp, tp, 128),
                                 bias, head_norm[0], None, chunk=256, lp=256)
    state = (state_C[0], state_n[0].reshape(bs, H_C, 1, DH_C), state_m[0])
    hc_s, c_s, n_s, m_s = _mlstm(zs.reshape(zs.shape[0], bs, ts, SEG), gates_s.reshape(bs, ts, 128),
                                 bias, head_norm[0], state, chunk=ts, lp=8)
    w_out1 = w_out_odd[0].astype(BF16)
    xp2 = _outproj([hc_p.reshape(rp, w_c)], [w_out1], xp1, gt_p, norm_post[1], rows_per_group=tp, tm=512)
    xs2 = _outproj([hc_s.reshape(rs, w_c)], [w_out1], xs1, gt_s, norm_post[1], rows_per_group=ts, tm=rs)

    kv_shape_p = (1, bp, tp, H_A, DH_A)
    kv_shape_s = (1, bs, ts, H_A, DH_A)
    return (xp2.reshape(bp, tp, d), xs2.reshape(bs, ts, d),
            kv_p[0].reshape(kv_shape_p), kv_p[1].reshape(kv_shape_p),
            kv_s[0].reshape(kv_shape_s), kv_s[1].reshape(kv_shape_s),
            pstate_p[None], pstate_s[None],
            c_p[None], n_p.reshape(1, bp, H_C, DH_C), m_p[:, :, 0, 0][None],
            c_s[None], n_s.reshape(1, bs, H_C, DH_C), m_s[:, :, 0, 0][None])
```

```python
import functools
import math

import jax
import jax.numpy as jnp
from jax import lax
from jax.experimental import pallas as pl
from jax.experimental.pallas import tpu as pltpu

F32 = jnp.float32
BF16 = jnp.bfloat16

H_A = 8
DH_A = 128
MOBA_BLOCK = 256
MOBA_TOPK = 3
POOL_WINDOWS = (2, 4, 8, 16)
GW_B = 256
POOL_HIST = max(POOL_WINDOWS) - 1
H_C = 8
DH_C = 256
EPS = 1e-6
PAGE_SIZE = 128

SEG = 1024
NEG_BIG = -1e30

VMEM_LIMIT_V7X = 52 * 1024 * 1024


def _sigmoid(x):
    return 0.5 * jnp.tanh(0.5 * x) + 0.5


def _silu(x):
    return x * _sigmoid(x)


def _log_sigmoid(x):
    return -(jnp.maximum(-x, 0.0) + jnp.log(1.0 + jnp.exp(-jnp.abs(x))))


def _cparams(sem):
    return pltpu.CompilerParams(dimension_semantics=sem, vmem_limit_bytes=VMEM_LIMIT_V7X)


def _adaln_kernel(c_ref, w_ref, b_ref, o_ref):
    s = _silu(c_ref[...]).astype(BF16)
    o_ref[...] = jnp.dot(s, w_ref[...].astype(BF16), preferred_element_type=F32) + b_ref[...]


def _adaln(c_all, ada_w, ada_b, tn=512):
    depth, d, n = ada_w.shape
    r = c_all.shape[0]
    return pl.pallas_call(
        _adaln_kernel,
        grid=(depth, n // tn),
        in_specs=[
            pl.BlockSpec((r, d), lambda l, j: (0, 0)),
            pl.BlockSpec((None, d, tn), lambda l, j: (l, 0, j)),
            pl.BlockSpec((None, 1, tn), lambda l, j: (l, 0, j)),
        ],
        out_specs=pl.BlockSpec((None, r, tn), lambda l, j: (l, 0, j)),
        out_shape=jax.ShapeDtypeStruct((depth, r, n), F32),
        compiler_params=_cparams(("parallel", "parallel")),
        name="adaln",
    )(c_all, ada_w, ada_b.reshape(depth, 1, n))


def _inproj_kernel(*refs, has_gate, hm):
    hm_refs = ()
    gt_ref = wg_ref = None
    if has_gate:
        x_ref, g_ref, sc_ref, sh_ref, w_ref, wg_ref, z_ref, gt_ref, h_scr = refs
    elif hm:
        x_ref, g_ref, sc_ref, sh_ref, w_ref, z_ref = refs[:6]
        hm_refs = refs[6:-1]
        h_scr = refs[-1]
    else:
        x_ref, g_ref, sc_ref, sh_ref, w_ref, z_ref, h_scr = refs
    j = pl.program_id(1)

    @pl.when(pl.program_id(1) == 0)
    def _():
        x = x_ref[...]
        r = x * lax.rsqrt(jnp.mean(x * x, axis=-1, keepdims=True) + EPS)
        h = (r * g_ref[...]) * (1.0 + sc_ref[...]) + sh_ref[...]
        h_scr[...] = h.astype(BF16)
        if has_gate:
            gt_ref[...] = jnp.dot(h_scr[...], wg_ref[...], preferred_element_type=F32)

    res = jnp.dot(h_scr[...], w_ref[...], preferred_element_type=F32)
    z_ref[...] = res

    if hm:
        first, count, heads = hm
        dh = SEG // heads
        for idx, hm_ref in enumerate(hm_refs):
            @pl.when(j == first + idx)
            def _(hm_ref=hm_ref):
                for hh in range(heads):
                    hm_ref[:, hh, :] = res[:, hh * dh:(hh + 1) * dh]


def _inproj(x2, g_pre, scale, shift, w_bf, wg_bf, *, rows_per_group, tm, hm=None):
    r, d = x2.shape
    n = w_bf.shape[1]
    nseg = n // SEG
    has_gate = wg_bf is not None
    if scale.ndim == 3:
        per = rows_per_group // tm
        mod_spec = pl.BlockSpec((None, 1, d), lambda i, j: (i // per, 0, 0))
    else:
        mod_spec = pl.BlockSpec((tm, d), lambda i, j: (i, 0))
    in_specs = [
        pl.BlockSpec((tm, d), lambda i, j: (i, 0), pipeline_mode=pl.Buffered(1)),
        pl.BlockSpec((1, d), lambda i, j: (0, 0)),
        mod_spec,
        mod_spec,
        pl.BlockSpec((d, SEG), lambda i, j: (0, j)),
    ]
    args = [x2, g_pre.reshape(1, d), scale, shift, w_bf]
    out_specs = [pl.BlockSpec((None, tm, SEG), lambda i, j: (j, i, 0))]
    out_shape = [jax.ShapeDtypeStruct((nseg, r, SEG), F32)]
    if has_gate:
        in_specs.append(pl.BlockSpec((d, 128), lambda i, j: (0, 0)))
        args.append(wg_bf)
        out_specs.append(pl.BlockSpec((tm, 128), lambda i, j: (i, 0)))
        out_shape.append(jax.ShapeDtypeStruct((r, 128), F32))
    if hm:
        first, count, heads = hm
        for _ in range(count):
            out_specs.append(pl.BlockSpec((tm, heads, SEG // heads), lambda i, j: (i, 0, 0),
                                          pipeline_mode=pl.Buffered(1)))
            out_shape.append(jax.ShapeDtypeStruct((r, heads, SEG // heads), F32))
    res = pl.pallas_call(
        functools.partial(_inproj_kernel, has_gate=has_gate, hm=hm),
        grid=(r // tm, nseg),
        in_specs=in_specs,
        out_specs=out_specs,
        out_shape=out_shape,
        scratch_shapes=[pltpu.VMEM((tm, d), BF16)],
        compiler_params=_cparams(("parallel", "arbitrary")),
        name="inproj_gate" if has_gate else "inproj",
    )(*args)
    if hm:
        return res[0], list(res[1:])
    return (res[0], res[1]) if has_gate else (res[0], None)


def _outproj_kernel(*refs, n_a):
    a_refs = refs[:n_a]
    w_refs = refs[n_a:2 * n_a]
    x_ref, gate_ref, gp_ref, o_ref = refs[2 * n_a:]
    y = None
    for a_ref, w_ref in zip(a_refs, w_refs):
        t = jnp.dot(a_ref[...].astype(BF16), w_ref[...], preferred_element_type=F32)
        y = t if y is None else y + t
    r = y * lax.rsqrt(jnp.mean(y * y, axis=-1, keepdims=True) + EPS)
    o_ref[...] = x_ref[...] + gate_ref[...] * (r * gp_ref[...])


def _outproj(a_list, w_list, x2, gate, g_post, *, rows_per_group, tm):
    r, d = x2.shape
    n_a = len(a_list)
    if gate.ndim == 3:
        per = rows_per_group // tm
        gate_spec = pl.BlockSpec((None, 1, d), lambda i: (i // per, 0, 0))
    else:
        gate_spec = pl.BlockSpec((tm, d), lambda i: (i, 0))
    in_specs = [pl.BlockSpec((tm, a.shape[1]), lambda i: (i, 0)) for a in a_list]
    in_specs += [pl.BlockSpec(w.shape, lambda i: (0, 0)) for w in w_list]
    in_specs += [pl.BlockSpec((tm, d), lambda i: (i, 0)), gate_spec, pl.BlockSpec((1, d), lambda i: (0, 0))]
    return pl.pallas_call(
        functools.partial(_outproj_kernel, n_a=n_a),
        grid=(r // tm,),
        in_specs=in_specs,
        out_specs=pl.BlockSpec((tm, d), lambda i: (i, 0)),
        out_shape=jax.ShapeDtypeStruct((r, d), F32),
        compiler_params=_cparams(("parallel",)),
        name="outproj",
    )(*a_list, *w_list, x2, gate, g_post.reshape(1, d))


MOBA_AUX = 128
ALIBI_COL = 8
MOBA_HEADS_PER_STEP = 2


def _moba_kernel(slopes_ref, q_ref, k_ref, v_ref, ga_ref, o_ref, kaug_scr, vt_scr, *, nb, hps):
    blk = MOBA_BLOCK
    t = nb * blk
    log2e = math.log2(math.e)
    scale2 = DH_A ** -0.5 * log2e
    nt = (((1,), (1,)), ((), ()))
    pos = lax.broadcasted_iota(jnp.int32, (t, MOBA_AUX), 0)
    lane = lax.broadcasted_iota(jnp.int32, (t, MOBA_AUX), 1)
    onehot = jnp.where(lane == pos // blk, 1.0, 0.0)
    posf = pos.astype(F32)
    ones_rows = jnp.where(lax.broadcasted_iota(jnp.int32, (8, blk), 0) < 3, 1.0, 0.0)
    rowb = lax.broadcasted_iota(jnp.int32, (nb, blk), 0)
    krow = lax.broadcasted_iota(jnp.int32, (blk, blk), 0)
    qcol = lax.broadcasted_iota(jnp.int32, (blk, blk), 1)

    def head(hh):
        hsl = slice(hh * DH_A, (hh + 1) * DH_A)
        k = k_ref[:, hsl]
        kmean_b = jnp.concatenate(
            [jnp.mean(k[j * blk:(j + 1) * blk, :], axis=0, keepdims=True) for j in range(nb)],
            axis=0).astype(BF16)
        a = (slopes_ref[pl.program_id(1) * hps + hh] * log2e) * posf
        a_hi = a.astype(BF16).astype(F32)
        a_mid = (a - a_hi).astype(BF16).astype(F32)
        a_lo = a - a_hi - a_mid
        aux = jnp.where(lane == ALIBI_COL, a_hi,
                        jnp.where(lane == ALIBI_COL + 1, a_mid,
                                  jnp.where(lane == ALIBI_COL + 2, a_lo, onehot)))
        kaug_scr[hh] = jnp.concatenate([k.astype(BF16), aux.astype(BF16)], axis=1)
        vt_scr[hh] = jnp.transpose(v_ref[:, hsl]).astype(BF16)

        def attend(n):
            nk = (n + 1) * blk
            q = q_ref[n * blk:nk, hsl]
            if n > MOBA_TOPK:
                gate = lax.dot_general(kmean_b, q.astype(BF16), nt, preferred_element_type=F32)
                gm = jnp.where(rowb < n, gate, -jnp.inf)
                cnt = jnp.zeros((nb, blk), F32)
                for kk in range(n):
                    gk = gm[kk:kk + 1, :]
                    beats = (gk > gm) | ((gk == gm) & (rowb > kk))
                    cnt = cnt + jnp.where(beats, 1.0, 0.0)
                keep = ((rowb < n) & (cnt < MOBA_TOPK)) | (rowb == n)
                bias = jnp.where(keep, 0.0, NEG_BIG)
            else:
                bias = jnp.zeros((nb, blk), F32)
            pieces = [bias, ones_rows]
            if nb < ALIBI_COL:
                pieces.insert(1, jnp.zeros((ALIBI_COL - nb, blk), F32))
            pieces.append(jnp.zeros((MOBA_AUX - ALIBI_COL - 8, blk), F32))
            rhs = jnp.concatenate([(jnp.transpose(q) * scale2).astype(BF16),
                                   jnp.concatenate(pieces, axis=0).astype(BF16)], axis=0)
            s = jnp.dot(kaug_scr[hh, 0:nk, :], rhs, preferred_element_type=F32)
            s_own = jnp.where(krow <= qcol, s[n * blk:nk, :], -jnp.inf)
            m = jnp.max(s_own, axis=0, keepdims=True)
            if n > 0:
                m = jnp.maximum(m, jnp.max(s[0:n * blk, :], axis=0, keepdims=True))
            p_own = jnp.exp2(s_own - m)
            l = jnp.sum(p_own, axis=0, keepdims=True)
            ot = jnp.dot(vt_scr[hh, :, n * blk:nk], p_own.astype(BF16), preferred_element_type=F32)
            if n > 0:
                p = jnp.exp2(s[0:n * blk, :] - m)
                l = l + jnp.sum(p, axis=0, keepdims=True)
                ot = ot + jnp.dot(vt_scr[hh, :, 0:n * blk], p.astype(BF16), preferred_element_type=F32)
            o = jnp.transpose(ot / l)
            o_ref[n * blk:nk, hsl] = (o * _silu(ga_ref[n * blk:nk, hsl])).astype(o_ref.dtype)

        for n in range(nb):
            attend(n)

    for hh in range(hps):
        head(hh)


def _moba_prompt(z4, slopes):
    _, b, t, w = z4.shape
    nb = t // MOBA_BLOCK
    assert nb <= ALIBI_COL
    hps = MOBA_HEADS_PER_STEP

    def seg_spec(sidx):
        return pl.BlockSpec((None, None, t, hps * DH_A), lambda bi, hg, s: (sidx, bi, 0, hg))

    grid_spec = pltpu.PrefetchScalarGridSpec(
        num_scalar_prefetch=1,
        grid=(b, H_A // hps),
        in_specs=[seg_spec(0), seg_spec(1), seg_spec(2), seg_spec(3)],
        out_specs=pl.BlockSpec((None, t, hps * DH_A), lambda bi, hg, s: (bi, 0, hg)),
        scratch_shapes=[
            pltpu.VMEM((hps, t, DH_A + MOBA_AUX), BF16),
            pltpu.VMEM((hps, DH_A, t), BF16),
        ],
    )
    return pl.pallas_call(
        functools.partial(_moba_kernel, nb=nb, hps=hps),
        grid_spec=grid_spec,
        out_shape=jax.ShapeDtypeStruct((b, t, w), BF16),
        compiler_params=_cparams(("parallel", "parallel")),
        name="moba_prompt",
    )(slopes, z4, z4, z4, z4)


def _pool_kernel(u_ref, gb_ref, prev_ref, pw_ref, ps_ref, o_ref, pn_ref, ext_scr, *, tt, pos0, nt):
    t = pl.program_id(1)
    hist = POOL_HIST + 1

    @pl.when(t == 0)
    def _():
        ext_scr[0:1, :] = jnp.zeros((1, ext_scr.shape[1]), F32)
        ext_scr[1:hist, :] = prev_ref[...]

    ext_scr[hist:hist + tt, :] = u_ref[...]
    pos = pos0 + t * tt + lax.broadcasted_iota(jnp.int32, (tt, 1), 0)
    for g, w in enumerate(POOL_WINDOWS):
        sl = slice(g * GW_B, (g + 1) * GW_B)
        x = u_ref[:, sl]
        acc = x
        for jj in range(1, w):
            acc = acc + ext_scr[hist - jj:hist - jj + tt, sl]
        cnt = jnp.minimum(w, pos + 1).astype(F32)
        d = acc / cnt - x
        y = jnp.dot(d.astype(BF16), pw_ref[g], preferred_element_type=F32) * ps_ref[:, sl]
        o_ref[:, sl] = (y * _silu(gb_ref[:, sl])).astype(o_ref.dtype)

    tail = ext_scr[tt:tt + hist, :]
    ext_scr[0:hist, :] = tail

    @pl.when(t == nt - 1)
    def _():
        pn_ref[...] = ext_scr[1:hist, :]


def _pool(z4, prev, pool_w_bf, pool_scale, *, pos0, tt, out_dtype):
    _, b, t, w = z4.shape
    nt = t // tt
    return pl.pallas_call(
        functools.partial(_pool_kernel, tt=tt, pos0=pos0, nt=nt),
        grid=(b, nt),
        in_specs=[
            pl.BlockSpec((None, None, tt, w), lambda bi, ti: (4, bi, ti, 0)),
            pl.BlockSpec((None, None, tt, w), lambda bi, ti: (5, bi, ti, 0)),
            pl.BlockSpec((None, POOL_HIST, w), lambda bi, ti: (bi, 0, 0)),
            pl.BlockSpec(pool_w_bf.shape, lambda bi, ti: (0, 0, 0)),
            pl.BlockSpec((1, w), lambda bi, ti: (0, 0)),
        ],
        out_specs=[
            pl.BlockSpec((None, tt, w), lambda bi, ti: (bi, ti, 0)),
            pl.BlockSpec((None, POOL_HIST, w), lambda bi, ti: (bi, 0, 0)),
        ],
        out_shape=[
            jax.ShapeDtypeStruct((b, t, w), out_dtype),
            jax.ShapeDtypeStruct((b, POOL_HIST, w), F32),
        ],
        scratch_shapes=[pltpu.VMEM((POOL_HIST + 1 + tt, w), F32)],
        compiler_params=_cparams(("parallel", "arbitrary")),
        name="pool",
    )(z4, z4, prev, pool_w_bf, pool_scale.reshape(1, w))


def _mlstm_kernel(*refs, lv, lp, zero_state):
    nseg = 2 * 5
    if zero_state:
        bias_ref = refs[0]
        seg_refs = refs[1:1 + nseg]
        gt_ref, hn_ref = refs[1 + nseg:3 + nseg]
        rest = refs[3 + nseg:]
    else:
        bias_ref, m0_ref = refs[:2]
        seg_refs = refs[2:2 + nseg]
        gt_ref, hn_ref, c0_ref, n0_ref = refs[2 + nseg:6 + nseg]
        rest = refs[6 + nseg:]
    h_out, c_out, n_out, m_out, c_scr, n_scr, m_scr, pad_scr = rest
    bi = pl.program_id(0)
    c = pl.program_id(1)
    nc = pl.num_programs(1)
    hpseg = SEG // DH_C

    @pl.when(c == 0)
    def _():
        if zero_state:
            c_scr[...] = jnp.zeros(c_scr.shape, F32)
            n_scr[...] = jnp.zeros(n_scr.shape, F32)
            m_scr[...] = jnp.zeros(m_scr.shape, F32)
        else:
            c_scr[...] = c0_ref[...]
            n_scr[...] = n0_ref[...]
            for h in range(H_C):
                m_scr[h] = jnp.full(m_scr.shape[1:], m0_ref[bi, h], F32)

    def load(ref, slot):
        if lv == lp:
            return ref
        pad_scr[slot] = jnp.zeros(pad_scr.shape[1:], F32)
        pad_scr[slot, 0:lv, 0:ref.shape[1]] = ref[...]
        return pad_scr.at[slot]

    srcs = [load(r, i) for i, r in enumerate(seg_refs)]
    gt_src = load(gt_ref, nseg)

    lane = lax.broadcasted_iota(jnp.int32, (1, 128), 1)
    bias_row = jnp.zeros((1, 128), F32)
    for h in range(H_C):
        bias_row = jnp.where(lane == h, bias_ref[0, h], bias_row)
        bias_row = jnp.where(lane == H_C + h, bias_ref[1, h], bias_row)
    gpre = gt_src[:, 0:128] + bias_row
    ig_all = gpre
    lf_all = _log_sigmoid(gpre)
    if lv != lp:
        row1 = lax.broadcasted_iota(jnp.int32, (lp, 1), 0)
        ig_all = jnp.where(row1 < lv, ig_all, NEG_BIG)
        lf_all = jnp.where(row1 < lv, lf_all, 0.0)
    rr = lax.broadcasted_iota(jnp.int32, (lp, lp), 0)
    cc = lax.broadcasted_iota(jnp.int32, (lp, lp), 1)
    tril = rr >= cc
    tril_b = jnp.where(tril, 1.0, 0.0).astype(BF16)
    lf_hi = lf_all.astype(BF16)
    r1 = lf_all - lf_hi.astype(F32)
    lf_mid = r1.astype(BF16)
    lf_lo = (r1 - lf_mid.astype(F32)).astype(BF16)
    b_all = (jnp.dot(tril_b, lf_hi, preferred_element_type=F32)
             + jnp.dot(tril_b, lf_mid, preferred_element_type=F32)
             + jnp.dot(tril_b, lf_lo, preferred_element_type=F32))
    lane_f = lax.broadcasted_iota(jnp.int32, (lp, 128), 1)
    rows_t = jnp.transpose(jnp.where(lane_f < H_C, ig_all, b_all))
    nt = (((1,), (1,)), ((), ()))

    for h in range(H_C):
        sl = slice((h % hpseg) * DH_C, (h % hpseg + 1) * DH_C)
        q = srcs[0 + h // hpseg][:, sl]
        k = srcs[2 + h // hpseg][:, sl] * (DH_C ** -0.5)
        v = srcs[4 + h // hpseg][:, sl]
        og_src = srcs[6 + h // hpseg]
        g_src = srcs[8 + h // hpseg]

        ig_col = ig_all[:, h:h + 1]
        b_col = b_all[:, H_C + h:H_C + h + 1]
        ig_row = rows_t[h:h + 1, :]
        b_row = rows_t[H_C + h:H_C + h + 1, :]
        dmat = jnp.where(tril, b_col - b_row + ig_row, -jnp.inf)
        m_prev = m_scr[h]
        carry = b_col + m_prev
        mt = jnp.maximum(carry, jnp.max(dmat, axis=1, keepdims=True))

        qb = q.astype(BF16)
        kb = k.astype(BF16)
        vb = v.astype(BF16)
        s = lax.dot_general(qb, kb, nt, preferred_element_type=F32) * jnp.exp(dmat - mt)
        inter = jnp.exp(carry - mt)
        cq = lax.dot_general(qb, c_scr[h].astype(BF16), nt, preferred_element_type=F32)
        num = jnp.dot(s.astype(BF16), vb, preferred_element_type=F32) + inter * cq
        nq = jnp.sum(q * n_scr[h], axis=1, keepdims=True)
        den = jnp.sum(s, axis=1, keepdims=True) + inter * nq
        hc = num / jnp.maximum(jnp.abs(den), jnp.exp(-mt))

        m_new = mt[lp - 1:lp, :]
        b_last = b_col[lp - 1:lp, :]
        w_col = jnp.exp(b_last - b_col + ig_col - m_new)
        decay = jnp.exp(b_last + m_prev - m_new)
        vw = (v * w_col).astype(BF16)
        c_scr[h] = decay * c_scr[h] + lax.dot_general(
            vw, kb, (((0,), (0,)), ((), ())), preferred_element_type=F32)
        n_scr[h] = decay * n_scr[h] + jnp.sum(k * w_col, axis=0, keepdims=True)
        m_scr[h] = m_new

        hsl = slice(h * DH_C, (h + 1) * DH_C)
        hc = hc * _sigmoid(og_src[:, sl])
        hc = hc * lax.rsqrt(jnp.mean(hc * hc, axis=-1, keepdims=True) + EPS) * hn_ref[:, hsl]
        res = (hc * _silu(g_src[:, sl])).astype(h_out.dtype)
        h_out[:, hsl] = res if lv == lp else res[0:lv, :]

    @pl.when(c == nc - 1)
    def _():
        c_out[...] = c_scr[...]
        n_out[...] = n_scr[...]
        m_out[...] = jnp.broadcast_to(m_scr[...], m_out.shape)


def _mlstm(z4, gates3, bias, head_norm, state, *, chunk, lp):
    nseg, b, t, _ = z4.shape
    nc = t // chunk
    w_c = H_C * DH_C
    zero_state = state is None
    n_pref = 1 if zero_state else 2

    def seg_spec(sidx):
        return pl.BlockSpec((None, None, chunk, SEG), lambda bi, c, *_: (sidx, bi, c, 0))

    def state_spec(*tail):
        return pl.BlockSpec((None, H_C) + tail, lambda bi, c, *_: (bi,) + (0,) * (len(tail) + 1))

    in_specs = [seg_spec(i) for i in range(nseg)]
    in_specs += [pl.BlockSpec((None, chunk, 128), lambda bi, c, *_: (bi, c, 0)),
                 pl.BlockSpec((1, w_c), lambda bi, c, *_: (0, 0))]
    args = [z4] * nseg + [gates3, head_norm.reshape(1, w_c)]
    prefetch = [bias]
    if not zero_state:
        c0, n0, m0 = state
        prefetch.append(m0)
        in_specs += [state_spec(DH_C, DH_C), state_spec(1, DH_C)]
        args += [c0, n0]
    grid_spec = pltpu.PrefetchScalarGridSpec(
        num_scalar_prefetch=n_pref,
        grid=(b, nc),
        in_specs=in_specs,
        out_specs=[
            pl.BlockSpec((None, chunk, w_c), lambda bi, c, *_: (bi, c, 0)),
            state_spec(DH_C, DH_C),
            state_spec(1, DH_C),
            state_spec(1, 128),
        ],
        scratch_shapes=[
            pltpu.VMEM((H_C, DH_C, DH_C), F32),
            pltpu.VMEM((H_C, 1, DH_C), F32),
            pltpu.VMEM((H_C, 1, 1), F32),
            pltpu.VMEM((nseg + 1, lp, SEG), F32),
        ],
    )
    return pl.pallas_call(
        functools.partial(_mlstm_kernel, lv=chunk, lp=lp, zero_state=zero_state),
        grid_spec=grid_spec,
        out_shape=[
            jax.ShapeDtypeStruct((b, t, w_c), BF16),
            jax.ShapeDtypeStruct((b, H_C, DH_C, DH_C), F32),
            jax.ShapeDtypeStruct((b, H_C, 1, DH_C), F32),
            jax.ShapeDtypeStruct((b, H_C, 1, 128), F32),
        ],
        compiler_params=_cparams(("parallel", "arbitrary")),
        name="mlstm",
    )(*prefetch, *args)


PAGES_PER_STEP = 16


def _sample_attn_kernel(pt_ref, slopes_ref, *refs, n_steps, tq, past_len):
    pp = PAGES_PER_STEP
    k_refs = refs[:pp]
    v_refs = refs[pp:2 * pp]
    qkvg_refs = refs[2 * pp:2 * pp + 4]
    o_ref, s_scr, pmax_scr, kmean_scr, acc_scr, l_scr, m_scr, sel_scr = refs[2 * pp + 4:]
    q_ref, kn_ref, vn_ref, ga_ref = qkvg_refs
    s = pl.program_id(1)
    rows = tq * H_A
    cols = PAGE_SIZE * H_A
    ppb = MOBA_BLOCK // PAGE_SIZE
    n_pages = n_steps * pp
    nt = (((1,), (1,)), ((), ()))
    log2e = math.log2(math.e)
    scale2 = DH_A ** -0.5 * log2e

    row = lax.broadcasted_iota(jnp.int32, (rows, 1), 0)
    row_h = row % H_A
    qpos = past_len + row // H_A
    slope2 = jnp.zeros((rows, 1), F32)
    for hh in range(H_A):
        slope2 = jnp.where(row_h == hh, slopes_ref[hh] * log2e, slope2)
    lane = lax.broadcasted_iota(jnp.int32, (1, cols), 1)

    def chunk(g):
        return pl.ds(pl.multiple_of(g * cols, cols), cols)

    def fold_lanes(x, op):
        acc = x[:, 0:128]
        for kk in range(1, cols // 128):
            acc = op(acc, x[:, kk * 128:(kk + 1) * 128])
        return acc

    def picked(sel, g):
        blk = g // ppb
        return (sel[0] == blk) | (sel[1] == blk) | (sel[2] == blk)

    @pl.when(s < n_steps)
    def _():
        qb = q_ref[...].reshape(rows, DH_A).astype(BF16)
        for jj in range(pp // ppb):
            tot = None
            for p in range(ppb):
                part = jnp.sum(k_refs[jj * ppb + p][...], axis=0)
                tot = part if tot is None else tot + part
            kmean_scr[pl.ds(pl.multiple_of((s * (pp // ppb) + jj) * H_A, H_A), H_A), :] = tot * (1.0 / MOBA_BLOCK)
        head_ok = (lane % H_A) == row_h
        qoff = slope2 * qpos.astype(F32)
        for p in range(pp):
            g = s * pp + p
            k2 = k_refs[p][...].reshape(cols, DH_A).astype(BF16)
            st = lax.dot_general(qb, k2, nt, preferred_element_type=F32)
            kpos = (g * PAGE_SIZE + lane // H_A).astype(F32)
            sv = st * scale2 + (slope2 * kpos - qoff)
            sv = jnp.where(head_ok, sv, -jnp.inf)
            s_scr[:, chunk(g)] = sv
            pmax_scr[g] = fold_lanes(sv, jnp.maximum)

    @pl.when(s == n_steps)
    def _():
        qb = q_ref[...].reshape(rows, DH_A).astype(BF16)
        nl = kmean_scr.shape[0]
        gate = lax.dot_general(qb, kmean_scr[...].astype(BF16), nt, preferred_element_type=F32)
        glane = lax.broadcasted_iota(jnp.int32, (rows, nl), 1)
        gate = jnp.where(glane % H_A == row_h, gate, -jnp.inf)
        sel = []
        sel_lane = lax.broadcasted_iota(jnp.int32, sel_scr.shape, 1)
        sel_tile = jnp.zeros(sel_scr.shape, jnp.int32)
        for t in range(MOBA_TOPK):
            mx = jnp.max(gate, axis=1, keepdims=True)
            idx = jnp.min(jnp.where(gate == mx, glane, nl), axis=1, keepdims=True)
            sel.append(idx // H_A)
            sel_tile = jnp.where(sel_lane == t, idx // H_A, sel_tile)
            gate = jnp.where(glane == idx, -jnp.inf, gate)
        sel_scr[...] = sel_tile

        kn2 = kn_ref[...].reshape(rows, DH_A).astype(BF16)
        olane = lax.broadcasted_iota(jnp.int32, (1, rows), 1)
        own_pos = past_len + olane // H_A
        s_own = lax.dot_general(qb, kn2, nt, preferred_element_type=F32) * scale2
        s_own = s_own - slope2 * (qpos - own_pos).astype(F32)
        s_own = jnp.where(((olane % H_A) == row_h) & (own_pos <= qpos), s_own, -jnp.inf)

        def max_body(g, macc):
            return jnp.maximum(macc, jnp.where(picked(sel, g), pmax_scr[g], -jnp.inf))

        macc = lax.fori_loop(0, n_pages, max_body, jnp.full((rows, 128), -jnp.inf, F32), unroll=4)
        m = jnp.maximum(jnp.max(macc, axis=1, keepdims=True), jnp.max(s_own, axis=1, keepdims=True))
        m_scr[...] = m
        p_own = jnp.exp2(s_own - m)
        l_scr[...] = jnp.where(lax.broadcasted_iota(jnp.int32, l_scr.shape, 1) == 0,
                               jnp.sum(p_own, axis=1, keepdims=True), 0.0)
        vn2 = vn_ref[...].reshape(rows, DH_A).astype(BF16)
        acc_scr[...] = jnp.dot(p_own.astype(BF16), vn2, preferred_element_type=F32)

    @pl.when(s >= n_steps)
    def _():
        sel = [sel_scr[:, t:t + 1] for t in range(MOBA_TOPK)]
        m = m_scr[...]
        acc = acc_scr[...]
        lacc = l_scr[...]
        for p in range(pp):
            g = (s - n_steps) * pp + p
            pr = jnp.exp2(s_scr[:, chunk(g)] - jnp.where(picked(sel, g), m, jnp.inf))
            lacc = lacc + fold_lanes(pr, jnp.add)
            v2 = v_refs[p][...].reshape(cols, DH_A).astype(BF16)
            acc = acc + jnp.dot(pr.astype(BF16), v2, preferred_element_type=F32)
        acc_scr[...] = acc
        l_scr[...] = lacc

    @pl.when(s == 2 * n_steps - 1)
    def _():
        o = acc_scr[...] / jnp.sum(l_scr[...], axis=1, keepdims=True)
        o = o * _silu(ga_ref[...].reshape(rows, DH_A))
        o_ref[...] = o.reshape(tq, H_A, DH_A)


def _sample_attn(cache_k4, cache_v4, page_table, slopes, qkvg):
    b, n_pages = page_table.shape
    _, tq, _, _ = qkvg[0].shape
    pp = PAGES_PER_STEP
    n_steps = n_pages // pp
    past_len = n_pages * PAGE_SIZE
    nbp = past_len // MOBA_BLOCK
    rows = tq * H_A

    def k_spec(p):
        return pl.BlockSpec((None, PAGE_SIZE, H_A, DH_A),
                            lambda bi, s, pt, sl: (pt[bi, jnp.minimum(s, n_steps - 1) * pp + p], 0, 0, 0))

    def v_spec(p):
        return pl.BlockSpec((None, PAGE_SIZE, H_A, DH_A),
                            lambda bi, s, pt, sl: (pt[bi, jnp.maximum(s - n_steps, 0) * pp + p], 0, 0, 0))

    new_spec = pl.BlockSpec((None, tq, H_A, DH_A), lambda bi, s, pt, sl: (bi, 0, 0, 0))

    grid_spec = pltpu.PrefetchScalarGridSpec(
        num_scalar_prefetch=2,
        grid=(b, 2 * n_steps),
        in_specs=[k_spec(p) for p in range(pp)] + [v_spec(p) for p in range(pp)] + [new_spec] * 4,
        out_specs=pl.BlockSpec((None, tq, H_A, DH_A), lambda bi, s, pt, sl: (bi, 0, 0, 0)),
        scratch_shapes=[
            pltpu.VMEM((rows, n_pages * PAGE_SIZE * H_A), F32),
            pltpu.VMEM((n_pages, rows, 128), F32),
            pltpu.VMEM((nbp * H_A, DH_A), F32),
            pltpu.VMEM((rows, DH_A), F32),
            pltpu.VMEM((rows, 128), F32),
            pltpu.VMEM((rows, 1), F32),
            pltpu.VMEM((rows, 128), jnp.int32),
        ],
    )
    return pl.pallas_call(
        functools.partial(_sample_attn_kernel, n_steps=n_steps, tq=tq, past_len=past_len),
        grid_spec=grid_spec,
        out_shape=jax.ShapeDtypeStruct((b, tq, H_A, DH_A), F32),
        compiler_params=_cparams(("parallel", "arbitrary")),
        name="sample_attn",
    )(page_table, slopes, *([cache_k4] * pp), *([cache_v4] * pp), *qkvg)


def _alibi_slopes(n_heads):
    return 2.0 ** (-8.0 * jnp.arange(1, n_heads + 1, dtype=F32) / n_heads)


def kernel(x_prompt, x_sample, cache_k, cache_v, state_pool, state_C, state_n, state_m, page_table,
           c_prompt, c_sample, ada_w, ada_b, norm_pre, norm_post, w_in_even, pool_w, pool_scale,
           w_out_even, w_in_odd, b_igate, b_fgate, head_norm, w_out_odd):
    bp, tp, d = x_prompt.shape
    bs, ts, _ = x_sample.shape
    n_pages = page_table.shape[1]
    past_len = n_pages * PAGE_SIZE
    w_a = H_A * DH_A
    w_c = H_C * DH_C
    rp, rs = bp * tp, bs * ts
    tm_p = 1024

    mods = _adaln(jnp.concatenate([c_prompt, c_sample], axis=0), ada_w, ada_b)

    def mod_parts(l):
        m = mods[l]
        shift, scale, gate = m[:, :d], m[:, d:2 * d], m[:, 2 * d:]
        prompt = tuple(a[:bp].reshape(bp, 1, d) for a in (shift, scale, gate))
        sample = tuple(jnp.repeat(a[bp:], ts, axis=0) for a in (shift, scale, gate))
        return prompt, sample

    slopes = _alibi_slopes(H_A)
    xp = x_prompt.reshape(rp, d)
    xs = x_sample.reshape(rs, d)

    (sh_p, sc_p, gt_p), (sh_s, sc_s, gt_s) = mod_parts(0)
    w_in0 = w_in_even[0].astype(BF16)
    zp, kv_p = _inproj(xp, norm_pre[0], sc_p, sh_p, w_in0, None, rows_per_group=tp, tm=tm_p, hm=(1, 2, H_A))
    zs, qkvg_s = _inproj(xs, norm_pre[0], sc_s, sh_s, w_in0, None, rows_per_group=ts, tm=rs, hm=(0, 4, H_A))
    qkvg_s = [a.reshape(bs, ts, H_A, DH_A) for a in qkvg_s]
    kv_s = qkvg_s[1:3]
    z4p = zp.reshape(zp.shape[0], bp, tp, SEG)
    z4s = zs.reshape(zs.shape[0], bs, ts, SEG)

    pool_w_bf = pool_w[0].astype(BF16)
    att_p = _moba_prompt(z4p, slopes)
    pool_p, pstate_p = _pool(z4p, jnp.zeros((bp, POOL_HIST, SEG), F32), pool_w_bf, pool_scale[0],
                             pos0=0, tt=512, out_dtype=BF16)

    att_s = _sample_attn(cache_k[0], cache_v[0], page_table, slopes, qkvg_s)
    pool_s, pstate_s = _pool(z4s, state_pool[0], pool_w_bf, pool_scale[0],
                             pos0=past_len, tt=ts, out_dtype=F32)

    w_out0 = w_out_even[0].astype(BF16)
    w_list0 = [w_out0[:w_a], w_out0[w_a:]]
    xp1 = _outproj([att_p.reshape(rp, w_a), pool_p.reshape(rp, SEG)], w_list0, xp, gt_p, norm_post[0],
                   rows_per_group=tp, tm=512)
    xs1 = _outproj([att_s.reshape(rs, w_a), pool_s.reshape(rs, SEG)], w_list0, xs, gt_s, norm_post[0],
                   rows_per_group=ts, tm=rs)

    (sh_p, sc_p, gt_p), (sh_s, sc_s, gt_s) = mod_parts(1)
    n_main = 5 * w_c
    w_in1 = w_in_odd[0].astype(BF16)
    wg1 = jnp.pad(w_in_odd[0][:, n_main:], ((0, 0), (0, 128 - 2 * H_C))).astype(BF16)
    zp, gates_p = _inproj(xp1, norm_pre[1], sc_p, sh_p, w_in1, wg1, rows_per_group=tp, tm=tm_p)
    zs, gates_s = _inproj(xs1, norm_pre[1], sc_s, sh_s, w_in1, wg1, rows_per_group=ts, tm=rs)
    bias = jnp.stack([b_igate[0], b_fgate[0]]).astype(F32)
    hc_p, c_p, n_p, m_p = _mlstm(zp.reshape(zp.shape[0], bp, tp, SEG), gates_p.reshape(bp, tp, 128),
                                 bias, head_norm[0], None, chunk=256, lp=256)
    state = (state_C[0], state_n[0].reshape(bs, H_C, 1, DH_C), state_m[0])
    hc_s, c_s, n_s, m_s = _mlstm(zs.reshape(zs.shape[0], bs, ts, SEG), gates_s.reshape(bs, ts, 128),
                                 bias, head_norm[0], state, chunk=ts, lp=128)
    w_out1 = w_out_odd[0].astype(BF16)
    xp2 = _outproj([hc_p.reshape(rp, w_c)], [w_out1], xp1, gt_p, norm_post[1], rows_per_group=tp, tm=512)
    xs2 = _outproj([hc_s.reshape(rs, w_c)], [w_out1], xs1, gt_s, norm_post[1], rows_per_group=ts, tm=rs)

    kv_shape_p = (1, bp, tp, H_A, DH_A)
    kv_shape_s = (1, bs, ts, H_A, DH_A)
    return (xp2.reshape(bp, tp, d), xs2.reshape(bs, ts, d),
            kv_p[0].reshape(kv_shape_p), kv_p[1].reshape(kv_shape_p),
            kv_s[0].reshape(kv_shape_s), kv_s[1].reshape(kv_shape_s),
            pstate_p[None], pstate_s[None],
            c_p[None], n_p.reshape(1, bp, H_C, DH_C), m_p[:, :, 0, 0][None],
            c_s[None], n_s.reshape(1, bs, H_C, DH_C), m_s[:, :, 0, 0][None])
```

```python
import functools
import math

import jax
import jax.numpy as jnp
from jax import lax
from jax.experimental import pallas as pl
from jax.experimental.pallas import tpu as pltpu

F32 = jnp.float32
BF16 = jnp.bfloat16

H_A = 8
DH_A = 128
MOBA_BLOCK = 256
MOBA_TOPK = 3
POOL_WINDOWS = (2, 4, 8, 16)
GW_B = 256
POOL_HIST = max(POOL_WINDOWS) - 1
H_C = 8
DH_C = 256
EPS = 1e-6
PAGE_SIZE = 128

SEG = 1024
NEG_BIG = -1e30

VMEM_LIMIT_V7X = 52 * 1024 * 1024


def _sigmoid(x):
    return 0.5 * jnp.tanh(0.5 * x) + 0.5


def _silu(x):
    return x * _sigmoid(x)


def _log_sigmoid(x):
    return -(jnp.maximum(-x, 0.0) + jnp.log(1.0 + jnp.exp(-jnp.abs(x))))


def _cparams(sem):
    return pltpu.CompilerParams(dimension_semantics=sem, vmem_limit_bytes=VMEM_LIMIT_V7X)


def _adaln_kernel(c_ref, w_ref, b_ref, o_ref):
    s = _silu(c_ref[...]).astype(BF16)
    o_ref[...] = jnp.dot(s, w_ref[...].astype(BF16), preferred_element_type=F32) + b_ref[...]


def _adaln(c_all, ada_w, ada_b, tn=512):
    depth, d, n = ada_w.shape
    r = c_all.shape[0]
    return pl.pallas_call(
        _adaln_kernel,
        grid=(depth, n // tn),
        in_specs=[
            pl.BlockSpec((r, d), lambda l, j: (0, 0)),
            pl.BlockSpec((None, d, tn), lambda l, j: (l, 0, j)),
            pl.BlockSpec((None, 1, tn), lambda l, j: (l, 0, j)),
        ],
        out_specs=pl.BlockSpec((None, r, tn), lambda l, j: (l, 0, j)),
        out_shape=jax.ShapeDtypeStruct((depth, r, n), F32),
        compiler_params=_cparams(("parallel", "parallel")),
        name="adaln",
    )(c_all, ada_w, ada_b.reshape(depth, 1, n))


def _inproj_kernel(*refs, has_gate, hm, nch):
    hm_refs = ()
    gt_ref = wg_ref = None
    if has_gate:
        x_ref, g_ref, sc_ref, sh_ref, w_ref, wg_ref, z_ref, gt_ref, h_even, h_odd = refs
    elif hm:
        x_ref, g_ref, sc_ref, sh_ref, w_ref, z_ref = refs[:6]
        hm_refs = refs[6:-2]
        h_even, h_odd = refs[-2:]
    else:
        x_ref, g_ref, sc_ref, sh_ref, w_ref, z_ref, h_even, h_odd = refs
    i = pl.program_id(0)
    j = pl.program_id(1)
    tm = x_ref.shape[0]
    rc = tm // nch

    def normed(rows):
        x = x_ref[rows, :]
        sc = sc_ref[...] if sc_ref.shape[0] == 1 else sc_ref[rows, :]
        sh = sh_ref[...] if sh_ref.shape[0] == 1 else sh_ref[rows, :]
        r = x * lax.rsqrt(jnp.mean(x * x, axis=-1, keepdims=True) + EPS)
        return ((r * g_ref[...]) * (1.0 + sc) + sh).astype(BF16)

    @pl.when((i == 0) & (j == 0))
    def _():
        h_even[...] = normed(slice(None))

    def step(h_cur, h_nxt):
        if has_gate:
            @pl.when(j == 0)
            def _():
                gt_ref[...] = jnp.dot(h_cur[...], wg_ref[...], preferred_element_type=F32)

        c = jnp.clip(j - 1, 0, nch - 1)
        rows = pl.ds(pl.multiple_of(c * rc, rc), rc)
        h_nxt[rows, :] = normed(rows)
        res = jnp.dot(h_cur[...], w_ref[...], preferred_element_type=F32)
        z_ref[...] = res

        if hm:
            first, count, heads = hm
            dh = SEG // heads
            for idx, hm_ref in enumerate(hm_refs):
                @pl.when(j == first + idx)
                def _(hm_ref=hm_ref):
                    for hh in range(heads):
                        hm_ref[:, hh, :] = res[:, hh * dh:(hh + 1) * dh]

    pl.when(i % 2 == 0)(functools.partial(step, h_even, h_odd))
    pl.when(i % 2 == 1)(functools.partial(step, h_odd, h_even))


def _inproj(x2, g_pre, scale, shift, w_bf, wg_bf, *, rows_per_group, tm, hm=None):
    r, d = x2.shape
    n = w_bf.shape[1]
    nseg = n // SEG
    n_tiles = r // tm
    has_gate = wg_bf is not None
    nch = 1 << ((nseg - 1).bit_length() - 1)

    def ahead(i, j):
        return jnp.where((i == 0) & (j == 0), 0, jnp.minimum(i + 1, n_tiles - 1))

    if scale.ndim == 3:
        per = rows_per_group // tm
        mod_spec = pl.BlockSpec((None, 1, d), lambda i, j: (ahead(i, j) // per, 0, 0))
    else:
        mod_spec = pl.BlockSpec((tm, d), lambda i, j: (ahead(i, j), 0))
    x_mode = {"pipeline_mode": pl.Buffered(1)} if hm else {}
    in_specs = [
        pl.BlockSpec((tm, d), lambda i, j: (ahead(i, j), 0), **x_mode),
        pl.BlockSpec((1, d), lambda i, j: (0, 0)),
        mod_spec,
        mod_spec,
        pl.BlockSpec((d, SEG), lambda i, j: (0, j)),
    ]
    args = [x2, g_pre.reshape(1, d), scale, shift, w_bf]
    out_specs = [pl.BlockSpec((None, tm, SEG), lambda i, j: (j, i, 0))]
    out_shape = [jax.ShapeDtypeStruct((nseg, r, SEG), F32)]
    if has_gate:
        in_specs.append(pl.BlockSpec((d, 128), lambda i, j: (0, 0)))
        args.append(wg_bf)
        out_specs.append(pl.BlockSpec((tm, 128), lambda i, j: (i, 0)))
        out_shape.append(jax.ShapeDtypeStruct((r, 128), F32))
    if hm:
        first, count, heads = hm
        for _ in range(count):
            out_specs.append(pl.BlockSpec((tm, heads, SEG // heads), lambda i, j: (i, 0, 0),
                                          pipeline_mode=pl.Buffered(1)))
            out_shape.append(jax.ShapeDtypeStruct((r, heads, SEG // heads), F32))
    res = pl.pallas_call(
        functools.partial(_inproj_kernel, has_gate=has_gate, hm=hm, nch=nch),
        grid=(n_tiles, nseg),
        in_specs=in_specs,
        out_specs=out_specs,
        out_shape=out_shape,
        scratch_shapes=[pltpu.VMEM((tm, d), BF16), pltpu.VMEM((tm, d), BF16)],
        compiler_params=_cparams(("arbitrary", "arbitrary")),
        name="inproj_gate" if has_gate else "inproj",
    )(*args)
    if hm:
        return res[0], list(res[1:])
    return (res[0], res[1]) if has_gate else (res[0], None)


def _outproj_kernel(*refs, n_a):
    a_refs = refs[:n_a]
    w_refs = refs[n_a:2 * n_a]
    x_ref, gate_ref, gp_ref, o_ref = refs[2 * n_a:]
    y = None
    for a_ref, w_ref in zip(a_refs, w_refs):
        t = jnp.dot(a_ref[...].astype(BF16), w_ref[...], preferred_element_type=F32)
        y = t if y is None else y + t
    r = y * lax.rsqrt(jnp.mean(y * y, axis=-1, keepdims=True) + EPS)
    o_ref[...] = x_ref[...] + gate_ref[...] * (r * gp_ref[...])


def _outproj(a_list, w_list, x2, gate, g_post, *, rows_per_group, tm):
    r, d = x2.shape
    n_a = len(a_list)
    if gate.ndim == 3:
        per = rows_per_group // tm
        gate_spec = pl.BlockSpec((None, 1, d), lambda i: (i // per, 0, 0))
    else:
        gate_spec = pl.BlockSpec((tm, d), lambda i: (i, 0))
    in_specs = [pl.BlockSpec((tm, a.shape[1]), lambda i: (i, 0)) for a in a_list]
    in_specs += [pl.BlockSpec(w.shape, lambda i: (0, 0)) for w in w_list]
    in_specs += [pl.BlockSpec((tm, d), lambda i: (i, 0)), gate_spec, pl.BlockSpec((1, d), lambda i: (0, 0))]
    return pl.pallas_call(
        functools.partial(_outproj_kernel, n_a=n_a),
        grid=(r // tm,),
        in_specs=in_specs,
        out_specs=pl.BlockSpec((tm, d), lambda i: (i, 0)),
        out_shape=jax.ShapeDtypeStruct((r, d), F32),
        compiler_params=_cparams(("parallel",)),
        name="outproj",
    )(*a_list, *w_list, x2, gate, g_post.reshape(1, d))


MOBA_AUX = 128
ALIBI_COL = 8
MOBA_HEADS_PER_STEP = 2


def _moba_kernel(slopes_ref, q_ref, k_ref, v_ref, ga_ref, o_ref, kaug_scr, vt_scr, *, nb, hps):
    blk = MOBA_BLOCK
    t = nb * blk
    log2e = math.log2(math.e)
    scale2 = DH_A ** -0.5 * log2e
    nt = (((1,), (1,)), ((), ()))
    pos = lax.broadcasted_iota(jnp.int32, (t, MOBA_AUX), 0)
    lane = lax.broadcasted_iota(jnp.int32, (t, MOBA_AUX), 1)
    onehot = jnp.where(lane == pos // blk, 1.0, 0.0)
    posf = pos.astype(F32)
    ones_rows = jnp.where(lax.broadcasted_iota(jnp.int32, (8, blk), 0) < 3, 1.0, 0.0)
    rowb = lax.broadcasted_iota(jnp.int32, (nb, blk), 0)
    krow = lax.broadcasted_iota(jnp.int32, (blk, blk), 0)
    qcol = lax.broadcasted_iota(jnp.int32, (blk, blk), 1)

    def head(hh):
        hsl = slice(hh * DH_A, (hh + 1) * DH_A)
        k = k_ref[:, hsl]
        kmean_b = jnp.concatenate(
            [jnp.mean(k[j * blk:(j + 1) * blk, :], axis=0, keepdims=True) for j in range(nb)],
            axis=0).astype(BF16)
        a = (slopes_ref[pl.program_id(1) * hps + hh] * log2e) * posf
        a_hi = a.astype(BF16).astype(F32)
        a_mid = (a - a_hi).astype(BF16).astype(F32)
        a_lo = a - a_hi - a_mid
        aux = jnp.where(lane == ALIBI_COL, a_hi,
                        jnp.where(lane == ALIBI_COL + 1, a_mid,
                                  jnp.where(lane == ALIBI_COL + 2, a_lo, onehot)))
        kaug_scr[hh] = jnp.concatenate([k.astype(BF16), aux.astype(BF16)], axis=1)
        vt_scr[hh] = jnp.transpose(v_ref[:, hsl]).astype(BF16)

        def attend(n):
            nk = (n + 1) * blk
            q = q_ref[n * blk:nk, hsl]
            if n > MOBA_TOPK:
                gate = lax.dot_general(kmean_b, q.astype(BF16), nt, preferred_element_type=F32)
                gm = jnp.where(rowb < n, gate, -jnp.inf)
                cnt = jnp.zeros((nb, blk), F32)
                for kk in range(n):
                    gk = gm[kk:kk + 1, :]
                    beats = (gk > gm) | ((gk == gm) & (rowb > kk))
                    cnt = cnt + jnp.where(beats, 1.0, 0.0)
                keep = ((rowb < n) & (cnt < MOBA_TOPK)) | (rowb == n)
                bias = jnp.where(keep, 0.0, NEG_BIG)
            else:
                bias = jnp.zeros((nb, blk), F32)
            pieces = [bias, ones_rows]
            if nb < ALIBI_COL:
                pieces.insert(1, jnp.zeros((ALIBI_COL - nb, blk), F32))
            pieces.append(jnp.zeros((MOBA_AUX - ALIBI_COL - 8, blk), F32))
            rhs = jnp.concatenate([(jnp.transpose(q) * scale2).astype(BF16),
                                   jnp.concatenate(pieces, axis=0).astype(BF16)], axis=0)
            s = jnp.dot(kaug_scr[hh, 0:nk, :], rhs, preferred_element_type=F32)
            s_own = jnp.where(krow <= qcol, s[n * blk:nk, :], -jnp.inf)
            m = jnp.max(s_own, axis=0, keepdims=True)
            if n > 0:
                m = jnp.maximum(m, jnp.max(s[0:n * blk, :], axis=0, keepdims=True))
            p_own = jnp.exp2(s_own - m)
            l = jnp.sum(p_own, axis=0, keepdims=True)
            ot = jnp.dot(vt_scr[hh, :, n * blk:nk], p_own.astype(BF16), preferred_element_type=F32)
            if n > 0:
                p = jnp.exp2(s[0:n * blk, :] - m)
                l = l + jnp.sum(p, axis=0, keepdims=True)
                ot = ot + jnp.dot(vt_scr[hh, :, 0:n * blk], p.astype(BF16), preferred_element_type=F32)
            o = jnp.transpose(ot / l)
            o_ref[n * blk:nk, hsl] = (o * _silu(ga_ref[n * blk:nk, hsl])).astype(o_ref.dtype)

        for n in range(nb):
            attend(n)

    for hh in range(hps):
        head(hh)


def _moba_prompt(z4, slopes):
    _, b, t, w = z4.shape
    nb = t // MOBA_BLOCK
    assert nb <= ALIBI_COL
    hps = MOBA_HEADS_PER_STEP

    def seg_spec(sidx):
        return pl.BlockSpec((None, None, t, hps * DH_A), lambda bi, hg, s: (sidx, bi, 0, hg))

    grid_spec = pltpu.PrefetchScalarGridSpec(
        num_scalar_prefetch=1,
        grid=(b, H_A // hps),
        in_specs=[seg_spec(0), seg_spec(1), seg_spec(2), seg_spec(3)],
        out_specs=pl.BlockSpec((None, t, hps * DH_A), lambda bi, hg, s: (bi, 0, hg)),
        scratch_shapes=[
            pltpu.VMEM((hps, t, DH_A + MOBA_AUX), BF16),
            pltpu.VMEM((hps, DH_A, t), BF16),
        ],
    )
    return pl.pallas_call(
        functools.partial(_moba_kernel, nb=nb, hps=hps),
        grid_spec=grid_spec,
        out_shape=jax.ShapeDtypeStruct((b, t, w), BF16),
        compiler_params=_cparams(("parallel", "parallel")),
        name="moba_prompt",
    )(slopes, z4, z4, z4, z4)


def _pool_kernel(u_ref, gb_ref, prev_ref, pw_ref, ps_ref, o_ref, pn_ref, ext_scr, *, tt, pos0, nt):
    t = pl.program_id(1)
    hist = POOL_HIST + 1

    @pl.when(t == 0)
    def _():
        ext_scr[0:1, :] = jnp.zeros((1, ext_scr.shape[1]), F32)
        ext_scr[1:hist, :] = prev_ref[...]

    ext_scr[hist:hist + tt, :] = u_ref[...]
    pos = pos0 + t * tt + lax.broadcasted_iota(jnp.int32, (tt, 1), 0)
    for g, w in enumerate(POOL_WINDOWS):
        sl = slice(g * GW_B, (g + 1) * GW_B)
        x = u_ref[:, sl]
        acc = x
        for jj in range(1, w):
            acc = acc + ext_scr[hist - jj:hist - jj + tt, sl]
        cnt = jnp.minimum(w, pos + 1).astype(F32)
        d = acc / cnt - x
        y = jnp.dot(d.astype(BF16), pw_ref[g], preferred_element_type=F32) * ps_ref[:, sl]
        o_ref[:, sl] = (y * _silu(gb_ref[:, sl])).astype(o_ref.dtype)

    tail = ext_scr[tt:tt + hist, :]
    ext_scr[0:hist, :] = tail

    @pl.when(t == nt - 1)
    def _():
        pn_ref[...] = ext_scr[1:hist, :]


def _pool(z4, prev, pool_w_bf, pool_scale, *, pos0, tt, out_dtype):
    _, b, t, w = z4.shape
    nt = t // tt
    return pl.pallas_call(
        functools.partial(_pool_kernel, tt=tt, pos0=pos0, nt=nt),
        grid=(b, nt),
        in_specs=[
            pl.BlockSpec((None, None, tt, w), lambda bi, ti: (4, bi, ti, 0)),
            pl.BlockSpec((None, None, tt, w), lambda bi, ti: (5, bi, ti, 0)),
            pl.BlockSpec((None, POOL_HIST, w), lambda bi, ti: (bi, 0, 0)),
            pl.BlockSpec(pool_w_bf.shape, lambda bi, ti: (0, 0, 0)),
            pl.BlockSpec((1, w), lambda bi, ti: (0, 0)),
        ],
        out_specs=[
            pl.BlockSpec((None, tt, w), lambda bi, ti: (bi, ti, 0)),
            pl.BlockSpec((None, POOL_HIST, w), lambda bi, ti: (bi, 0, 0)),
        ],
        out_shape=[
            jax.ShapeDtypeStruct((b, t, w), out_dtype),
            jax.ShapeDtypeStruct((b, POOL_HIST, w), F32),
        ],
        scratch_shapes=[pltpu.VMEM((POOL_HIST + 1 + tt, w), F32)],
        compiler_params=_cparams(("parallel", "arbitrary")),
        name="pool",
    )(z4, z4, prev, pool_w_bf, pool_scale.reshape(1, w))


def _mlstm_kernel(*refs, lv, lp, zero_state):
    nseg = 2 * 5
    if zero_state:
        bias_ref = refs[0]
        seg_refs = refs[1:1 + nseg]
        gt_ref, hn_ref = refs[1 + nseg:3 + nseg]
        rest = refs[3 + nseg:]
    else:
        bias_ref, m0_ref = refs[:2]
        seg_refs = refs[2:2 + nseg]
        gt_ref, hn_ref, c0_ref, n0_ref = refs[2 + nseg:6 + nseg]
        rest = refs[6 + nseg:]
    h_out, c_out, n_out, m_out, c_scr, n_scr, m_scr, pad_scr = rest
    bi = pl.program_id(0)
    c = pl.program_id(1)
    nc = pl.num_programs(1)
    hpseg = SEG // DH_C

    @pl.when(c == 0)
    def _():
        if zero_state:
            c_scr[...] = jnp.zeros(c_scr.shape, F32)
            n_scr[...] = jnp.zeros(n_scr.shape, F32)
            m_scr[...] = jnp.zeros(m_scr.shape, F32)
        else:
            c_scr[...] = c0_ref[...]
            n_scr[...] = n0_ref[...]
            for h in range(H_C):
                m_scr[h] = jnp.full(m_scr.shape[1:], m0_ref[bi, h], F32)

    def load(ref, slot):
        if lv == lp:
            return ref
        pad_scr[slot] = jnp.zeros(pad_scr.shape[1:], F32)
        pad_scr[slot, 0:lv, 0:ref.shape[1]] = ref[...]
        return pad_scr.at[slot]

    srcs = [load(r, i) for i, r in enumerate(seg_refs)]
    gt_src = load(gt_ref, nseg)

    lane = lax.broadcasted_iota(jnp.int32, (1, 128), 1)
    bias_row = jnp.zeros((1, 128), F32)
    for h in range(H_C):
        bias_row = jnp.where(lane == h, bias_ref[0, h], bias_row)
        bias_row = jnp.where(lane == H_C + h, bias_ref[1, h], bias_row)
    gpre = gt_src[:, 0:128] + bias_row
    ig_all = gpre
    lf_all = _log_sigmoid(gpre)
    if lv != lp:
        row1 = lax.broadcasted_iota(jnp.int32, (lp, 1), 0)
        ig_all = jnp.where(row1 < lv, ig_all, NEG_BIG)
        lf_all = jnp.where(row1 < lv, lf_all, 0.0)
    rr = lax.broadcasted_iota(jnp.int32, (lp, lp), 0)
    cc = lax.broadcasted_iota(jnp.int32, (lp, lp), 1)
    tril = rr >= cc
    tril_b = jnp.where(tril, 1.0, 0.0).astype(BF16)
    lf_hi = lf_all.astype(BF16)
    r1 = lf_all - lf_hi.astype(F32)
    lf_mid = r1.astype(BF16)
    lf_lo = (r1 - lf_mid.astype(F32)).astype(BF16)
    b_all = (jnp.dot(tril_b, lf_hi, preferred_element_type=F32)
             + jnp.dot(tril_b, lf_mid, preferred_element_type=F32)
             + jnp.dot(tril_b, lf_lo, preferred_element_type=F32))
    lane_f = lax.broadcasted_iota(jnp.int32, (lp, 128), 1)
    rows_t = jnp.transpose(jnp.where(lane_f < H_C, ig_all, b_all))
    nt = (((1,), (1,)), ((), ()))

    for h in range(H_C):
        sl = slice((h % hpseg) * DH_C, (h % hpseg + 1) * DH_C)
        q = srcs[0 + h // hpseg][:, sl]
        k = srcs[2 + h // hpseg][:, sl] * (DH_C ** -0.5)
        v = srcs[4 + h // hpseg][:, sl]
        og_src = srcs[6 + h // hpseg]
        g_src = srcs[8 + h // hpseg]

        ig_col = ig_all[:, h:h + 1]
        b_col = b_all[:, H_C + h:H_C + h + 1]
        ig_row = rows_t[h:h + 1, :]
        b_row = rows_t[H_C + h:H_C + h + 1, :]
        dmat = jnp.where(tril, b_col - b_row + ig_row, -jnp.inf)
        m_prev = m_scr[h]
        carry = b_col + m_prev
        mt = jnp.maximum(carry, jnp.max(dmat, axis=1, keepdims=True))

        qb = q.astype(BF16)
        kb = k.astype(BF16)
        vb = v.astype(BF16)
        s = lax.dot_general(qb, kb, nt, preferred_element_type=F32) * jnp.exp(dmat - mt)
        inter = jnp.exp(carry - mt)
        cq = lax.dot_general(qb, c_scr[h].astype(BF16), nt, preferred_element_type=F32)
        num = jnp.dot(s.astype(BF16), vb, preferred_element_type=F32) + inter * cq
        nq = jnp.sum(q * n_scr[h], axis=1, keepdims=True)
        den = jnp.sum(s, axis=1, keepdims=True) + inter * nq
        hc = num / jnp.maximum(jnp.abs(den), jnp.exp(-mt))

        m_new = mt[lp - 1:lp, :]
        b_last = b_col[lp - 1:lp, :]
        w_col = jnp.exp(b_last - b_col + ig_col - m_new)
        decay = jnp.exp(b_last + m_prev - m_new)
        vw = (v * w_col).astype(BF16)
        c_scr[h] = decay * c_scr[h] + lax.dot_general(
            vw, kb, (((0,), (0,)), ((), ())), preferred_element_type=F32)
        n_scr[h] = decay * n_scr[h] + jnp.sum(k * w_col, axis=0, keepdims=True)
        m_scr[h] = m_new

        hsl = slice(h * DH_C, (h + 1) * DH_C)
        hc = hc * _sigmoid(og_src[:, sl])
        hc = hc * lax.rsqrt(jnp.mean(hc * hc, axis=-1, keepdims=True) + EPS) * hn_ref[:, hsl]
        res = (hc * _silu(g_src[:, sl])).astype(h_out.dtype)
        h_out[:, hsl] = res if lv == lp else res[0:lv, :]

    @pl.when(c == nc - 1)
    def _():
        c_out[...] = c_scr[...]
        n_out[...] = n_scr[...]
        m_out[...] = jnp.broadcast_to(m_scr[...], m_out.shape)


def _mlstm(z4, gates3, bias, head_norm, state, *, chunk, lp):
    nseg, b, t, _ = z4.shape
    nc = t // chunk
    w_c = H_C * DH_C
    zero_state = state is None
    n_pref = 1 if zero_state else 2

    def seg_spec(sidx):
        return pl.BlockSpec((None, None, chunk, SEG), lambda bi, c, *_: (sidx, bi, c, 0))

    def state_spec(*tail):
        return pl.BlockSpec((None, H_C) + tail, lambda bi, c, *_: (bi,) + (0,) * (len(tail) + 1))

    in_specs = [seg_spec(i) for i in range(nseg)]
    in_specs += [pl.BlockSpec((None, chunk, 128), lambda bi, c, *_: (bi, c, 0)),
                 pl.BlockSpec((1, w_c), lambda bi, c, *_: (0, 0))]
    args = [z4] * nseg + [gates3, head_norm.reshape(1, w_c)]
    prefetch = [bias]
    if not zero_state:
        c0, n0, m0 = state
        prefetch.append(m0)
        in_specs += [state_spec(DH_C, DH_C), state_spec(1, DH_C)]
        args += [c0, n0]
    grid_spec = pltpu.PrefetchScalarGridSpec(
        num_scalar_prefetch=n_pref,
        grid=(b, nc),
        in_specs=in_specs,
        out_specs=[
            pl.BlockSpec((None, chunk, w_c), lambda bi, c, *_: (bi, c, 0)),
            state_spec(DH_C, DH_C),
            state_spec(1, DH_C),
            state_spec(1, 128),
        ],
        scratch_shapes=[
            pltpu.VMEM((H_C, DH_C, DH_C), F32),
            pltpu.VMEM((H_C, 1, DH_C), F32),
            pltpu.VMEM((H_C, 1, 1), F32),
            pltpu.VMEM((nseg + 1, lp, SEG), F32),
        ],
    )
    return pl.pallas_call(
        functools.partial(_mlstm_kernel, lv=chunk, lp=lp, zero_state=zero_state),
        grid_spec=grid_spec,
        out_shape=[
            jax.ShapeDtypeStruct((b, t, w_c), BF16),
            jax.ShapeDtypeStruct((b, H_C, DH_C, DH_C), F32),
            jax.ShapeDtypeStruct((b, H_C, 1, DH_C), F32),
            jax.ShapeDtypeStruct((b, H_C, 1, 128), F32),
        ],
        compiler_params=_cparams(("parallel", "arbitrary")),
        name="mlstm",
    )(*prefetch, *args)


PAGES_PER_STEP = 16


def _sample_attn_kernel(pt_ref, slopes_ref, *refs, n_steps, tq, past_len):
    pp = PAGES_PER_STEP
    k_refs = refs[:pp]
    v_refs = refs[pp:2 * pp]
    qkvg_refs = refs[2 * pp:2 * pp + 4]
    o_ref, s_scr, pmax_scr, kmean_scr, acc_scr, l_scr, m_scr, sel_scr = refs[2 * pp + 4:]
    q_ref, kn_ref, vn_ref, ga_ref = qkvg_refs
    s = pl.program_id(1)
    rows = tq * H_A
    cols = PAGE_SIZE * H_A
    ppb = MOBA_BLOCK // PAGE_SIZE
    n_pages = n_steps * pp
    nt = (((1,), (1,)), ((), ()))
    log2e = math.log2(math.e)
    scale2 = DH_A ** -0.5 * log2e

    row = lax.broadcasted_iota(jnp.int32, (rows, 1), 0)
    row_h = row % H_A
    qpos = past_len + row // H_A
    slope2 = jnp.zeros((rows, 1), F32)
    for hh in range(H_A):
        slope2 = jnp.where(row_h == hh, slopes_ref[hh] * log2e, slope2)
    lane = lax.broadcasted_iota(jnp.int32, (1, cols), 1)

    def chunk(g):
        return pl.ds(pl.multiple_of(g * cols, cols), cols)

    def fold_lanes(x, op):
        acc = x[:, 0:128]
        for kk in range(1, cols // 128):
            acc = op(acc, x[:, kk * 128:(kk + 1) * 128])
        return acc

    def picked(sel, g):
        blk = g // ppb
        return (sel[0] == blk) | (sel[1] == blk) | (sel[2] == blk)

    @pl.when(s < n_steps)
    def _():
        qb = q_ref[...].reshape(rows, DH_A).astype(BF16)
        for jj in range(pp // ppb):
            tot = None
            for p in range(ppb):
                part = jnp.sum(k_refs[jj * ppb + p][...], axis=0)
                tot = part if tot is None else tot + part
            kmean_scr[pl.ds(pl.multiple_of((s * (pp // ppb) + jj) * H_A, H_A), H_A), :] = tot * (1.0 / MOBA_BLOCK)
        head_ok = (lane % H_A) == row_h
        qoff = slope2 * qpos.astype(F32)
        for p in range(pp):
            g = s * pp + p
            k2 = k_refs[p][...].reshape(cols, DH_A).astype(BF16)
            st = lax.dot_general(qb, k2, nt, preferred_element_type=F32)
            kpos = (g * PAGE_SIZE + lane // H_A).astype(F32)
            sv = st * scale2 + (slope2 * kpos - qoff)
            sv = jnp.where(head_ok, sv, -jnp.inf)
            s_scr[:, chunk(g)] = sv
            pmax_scr[g] = fold_lanes(sv, jnp.maximum)

    @pl.when(s == n_steps)
    def _():
        qb = q_ref[...].reshape(rows, DH_A).astype(BF16)
        nl = kmean_scr.shape[0]
        gate = lax.dot_general(qb, kmean_scr[...].astype(BF16), nt, preferred_element_type=F32)
        glane = lax.broadcasted_iota(jnp.int32, (rows, nl), 1)
        gate = jnp.where(glane % H_A == row_h, gate, -jnp.inf)
        sel = []
        sel_lane = lax.broadcasted_iota(jnp.int32, sel_scr.shape, 1)
        sel_tile = jnp.zeros(sel_scr.shape, jnp.int32)
        for t in range(MOBA_TOPK):
            mx = jnp.max(gate, axis=1, keepdims=True)
            idx = jnp.min(jnp.where(gate == mx, glane, nl), axis=1, keepdims=True)
            sel.append(idx // H_A)
            sel_tile = jnp.where(sel_lane == t, idx // H_A, sel_tile)
            gate = jnp.where(glane == idx, -jnp.inf, gate)
        sel_scr[...] = sel_tile

        kn2 = kn_ref[...].reshape(rows, DH_A).astype(BF16)
        olane = lax.broadcasted_iota(jnp.int32, (1, rows), 1)
        own_pos = past_len + olane // H_A
        s_own = lax.dot_general(qb, kn2, nt, preferred_element_type=F32) * scale2
        s_own = s_own - slope2 * (qpos - own_pos).astype(F32)
        s_own = jnp.where(((olane % H_A) == row_h) & (own_pos <= qpos), s_own, -jnp.inf)

        def max_body(g, macc):
            return jnp.maximum(macc, jnp.where(picked(sel, g), pmax_scr[g], -jnp.inf))

        macc = lax.fori_loop(0, n_pages, max_body, jnp.full((rows, 128), -jnp.inf, F32), unroll=4)
        m = jnp.maximum(jnp.max(macc, axis=1, keepdims=True), jnp.max(s_own, axis=1, keepdims=True))
        m_scr[...] = m
        p_own = jnp.exp2(s_own - m)
        l_scr[...] = jnp.where(lax.broadcasted_iota(jnp.int32, l_scr.shape, 1) == 0,
                               jnp.sum(p_own, axis=1, keepdims=True), 0.0)
        vn2 = vn_ref[...].reshape(rows, DH_A).astype(BF16)
        acc_scr[...] = jnp.dot(p_own.astype(BF16), vn2, preferred_element_type=F32)

    @pl.when(s >= n_steps)
    def _():
        sel = [sel_scr[:, t:t + 1] for t in range(MOBA_TOPK)]
        m = m_scr[...]
        acc = acc_scr[...]
        lacc = l_scr[...]
        for p in range(pp):
            g = (s - n_steps) * pp + p
            pr = jnp.exp2(s_scr[:, chunk(g)] - jnp.where(picked(sel, g), m, jnp.inf))
            lacc = lacc + fold_lanes(pr, jnp.add)
            v2 = v_refs[p][...].reshape(cols, DH_A).astype(BF16)
            acc = acc + jnp.dot(pr.astype(BF16), v2, preferred_element_type=F32)
        acc_scr[...] = acc
        l_scr[...] = lacc

    @pl.when(s == 2 * n_steps - 1)
    def _():
        o = acc_scr[...] / jnp.sum(l_scr[...], axis=1, keepdims=True)
        o = o * _silu(ga_ref[...].reshape(rows, DH_A))
        o_ref[...] = o.reshape(tq, H_A, DH_A)


def _sample_attn(cache_k4, cache_v4, page_table, slopes, qkvg):
    b, n_pages = page_table.shape
    _, tq, _, _ = qkvg[0].shape
    pp = PAGES_PER_STEP
    n_steps = n_pages // pp
    past_len = n_pages * PAGE_SIZE
    nbp = past_len // MOBA_BLOCK
    rows = tq * H_A

    def k_spec(p):
        return pl.BlockSpec((None, PAGE_SIZE, H_A, DH_A),
                            lambda bi, s, pt, sl: (pt[bi, jnp.minimum(s, n_steps - 1) * pp + p], 0, 0, 0))

    def v_spec(p):
        return pl.BlockSpec((None, PAGE_SIZE, H_A, DH_A),
                            lambda bi, s, pt, sl: (pt[bi, jnp.maximum(s - n_steps, 0) * pp + p], 0, 0, 0))

    new_spec = pl.BlockSpec((None, tq, H_A, DH_A), lambda bi, s, pt, sl: (bi, 0, 0, 0))

    grid_spec = pltpu.PrefetchScalarGridSpec(
        num_scalar_prefetch=2,
        grid=(b, 2 * n_steps),
        in_specs=[k_spec(p) for p in range(pp)] + [v_spec(p) for p in range(pp)] + [new_spec] * 4,
        out_specs=pl.BlockSpec((None, tq, H_A, DH_A), lambda bi, s, pt, sl: (bi, 0, 0, 0)),
        scratch_shapes=[
            pltpu.VMEM((rows, n_pages * PAGE_SIZE * H_A), F32),
            pltpu.VMEM((n_pages, rows, 128), F32),
            pltpu.VMEM((nbp * H_A, DH_A), F32),
            pltpu.VMEM((rows, DH_A), F32),
            pltpu.VMEM((rows, 128), F32),
            pltpu.VMEM((rows, 1), F32),
            pltpu.VMEM((rows, 128), jnp.int32),
        ],
    )
    return pl.pallas_call(
        functools.partial(_sample_attn_kernel, n_steps=n_steps, tq=tq, past_len=past_len),
        grid_spec=grid_spec,
        out_shape=jax.ShapeDtypeStruct((b, tq, H_A, DH_A), F32),
        compiler_params=_cparams(("parallel", "arbitrary")),
        name="sample_attn",
    )(page_table, slopes, *([cache_k4] * pp), *([cache_v4] * pp), *qkvg)


def _alibi_slopes(n_heads):
    return 2.0 ** (-8.0 * jnp.arange(1, n_heads + 1, dtype=F32) / n_heads)


def kernel(x_prompt, x_sample, cache_k, cache_v, state_pool, state_C, state_n, state_m, page_table,
           c_prompt, c_sample, ada_w, ada_b, norm_pre, norm_post, w_in_even, pool_w, pool_scale,
           w_out_even, w_in_odd, b_igate, b_fgate, head_norm, w_out_odd):
    bp, tp, d = x_prompt.shape
    bs, ts, _ = x_sample.shape
    n_pages = page_table.shape[1]
    past_len = n_pages * PAGE_SIZE
    w_a = H_A * DH_A
    w_c = H_C * DH_C
    rp, rs = bp * tp, bs * ts
    tm_p = 1024

    mods = _adaln(jnp.concatenate([c_prompt, c_sample], axis=0), ada_w, ada_b)

    def mod_parts(l):
        m = mods[l]
        shift, scale, gate = m[:, :d], m[:, d:2 * d], m[:, 2 * d:]
        prompt = tuple(a[:bp].reshape(bp, 1, d) for a in (shift, scale, gate))
        sample = tuple(jnp.repeat(a[bp:], ts, axis=0) for a in (shift, scale, gate))
        return prompt, sample

    slopes = _alibi_slopes(H_A)
    xp = x_prompt.reshape(rp, d)
    xs = x_sample.reshape(rs, d)

    (sh_p, sc_p, gt_p), (sh_s, sc_s, gt_s) = mod_parts(0)
    w_in0 = w_in_even[0].astype(BF16)
    zp, kv_p = _inproj(xp, norm_pre[0], sc_p, sh_p, w_in0, None, rows_per_group=tp, tm=tm_p, hm=(1, 2, H_A))
    zs, qkvg_s = _inproj(xs, norm_pre[0], sc_s, sh_s, w_in0, None, rows_per_group=ts, tm=rs, hm=(0, 4, H_A))
    qkvg_s = [a.reshape(bs, ts, H_A, DH_A) for a in qkvg_s]
    kv_s = qkvg_s[1:3]
    z4p = zp.reshape(zp.shape[0], bp, tp, SEG)
    z4s = zs.reshape(zs.shape[0], bs, ts, SEG)

    pool_w_bf = pool_w[0].astype(BF16)
    att_p = _moba_prompt(z4p, slopes)
    pool_p, pstate_p = _pool(z4p, jnp.zeros((bp, POOL_HIST, SEG), F32), pool_w_bf, pool_scale[0],
                             pos0=0, tt=512, out_dtype=BF16)

    att_s = _sample_attn(cache_k[0], cache_v[0], page_table, slopes, qkvg_s)
    pool_s, pstate_s = _pool(z4s, state_pool[0], pool_w_bf, pool_scale[0],
                             pos0=past_len, tt=ts, out_dtype=F32)

    w_out0 = w_out_even[0].astype(BF16)
    w_list0 = [w_out0[:w_a], w_out0[w_a:]]
    xp1 = _outproj([att_p.reshape(rp, w_a), pool_p.reshape(rp, SEG)], w_list0, xp, gt_p, norm_post[0],
                   rows_per_group=tp, tm=512)
    xs1 = _outproj([att_s.reshape(rs, w_a), pool_s.reshape(rs, SEG)], w_list0, xs, gt_s, norm_post[0],
                   rows_per_group=ts, tm=rs)

    (sh_p, sc_p, gt_p), (sh_s, sc_s, gt_s) = mod_parts(1)
    n_main = 5 * w_c
    w_in1 = w_in_odd[0].astype(BF16)
    wg1 = jnp.pad(w_in_odd[0][:, n_main:], ((0, 0), (0, 128 - 2 * H_C))).astype(BF16)
    zp, gates_p = _inproj(xp1, norm_pre[1], sc_p, sh_p, w_in1, wg1, rows_per_group=tp, tm=tm_p)
    zs, gates_s = _inproj(xs1, norm_pre[1], sc_s, sh_s, w_in1, wg1, rows_per_group=ts, tm=rs)
    bias = jnp.stack([b_igate[0], b_fgate[0]]).astype(F32)
    hc_p, c_p, n_p, m_p = _mlstm(zp.reshape(zp.shape[0], bp, tp, SEG), gates_p.reshape(bp, tp, 128),
                                 bias, head_norm[0], None, chunk=256, lp=256)
    state = (state_C[0], state_n[0].reshape(bs, H_C, 1, DH_C), state_m[0])
    hc_s, c_s, n_s, m_s = _mlstm(zs.reshape(zs.shape[0], bs, ts, SEG), gates_s.reshape(bs, ts, 128),
                                 bias, head_norm[0], state, chunk=ts, lp=128)
    w_out1 = w_out_odd[0].astype(BF16)
    xp2 = _outproj([hc_p.reshape(rp, w_c)], [w_out1], xp1, gt_p, norm_post[1], rows_per_group=tp, tm=512)
    xs2 = _outproj([hc_s.reshape(rs, w_c)], [w_out1], xs1, gt_s, norm_post[1], rows_per_group=ts, tm=rs)

    kv_shape_p = (1, bp, tp, H_A, DH_A)
    kv_shape_s = (1, bs, ts, H_A, DH_A)
    return (xp2.reshape(bp, tp, d), xs2.reshape(bs, ts, d),
            kv_p[0].reshape(kv_shape_p), kv_p[1].reshape(kv_shape_p),
            kv_s[0].reshape(kv_shape_s), kv_s[1].reshape(kv_shape_s),
            pstate_p[None], pstate_s[None],
            c_p[None], n_p.reshape(1, bp, H_C, DH_C), m_p[:, :, 0, 0][None],
            c_s[None], n_s.reshape(1, bs, H_C, DH_C), m_s[:, :, 0, 0][None])
```

```python
import functools
import math

import jax
import jax.numpy as jnp
from jax import lax
from jax.experimental import pallas as pl
from jax.experimental.pallas import tpu as pltpu

F32 = jnp.float32
BF16 = jnp.bfloat16

H_A = 8
DH_A = 128
MOBA_BLOCK = 256
MOBA_TOPK = 3
POOL_WINDOWS = (2, 4, 8, 16)
GW_B = 256
POOL_HIST = max(POOL_WINDOWS) - 1
H_C = 8
DH_C = 256
EPS = 1e-6
PAGE_SIZE = 128

SEG = 1024
NEG_BIG = -1e30

VMEM_LIMIT_V7X = 52 * 1024 * 1024


def _sigmoid(x):
    return 0.5 * jnp.tanh(0.5 * x) + 0.5


def _silu(x):
    return x * _sigmoid(x)


def _log_sigmoid(x):
    return -(jnp.maximum(-x, 0.0) + jnp.log(1.0 + jnp.exp(-jnp.abs(x))))


def _cparams(sem):
    return pltpu.CompilerParams(dimension_semantics=sem, vmem_limit_bytes=VMEM_LIMIT_V7X)


def _adaln_kernel(c_ref, w_ref, b_ref, o_ref):
    s = _silu(c_ref[...]).astype(BF16)
    o_ref[...] = jnp.dot(s, w_ref[...].astype(BF16), preferred_element_type=F32) + b_ref[...]


def _adaln(c_all, ada_w, ada_b, tn=512):
    depth, d, n = ada_w.shape
    r = c_all.shape[0]
    return pl.pallas_call(
        _adaln_kernel,
        grid=(depth, n // tn),
        in_specs=[
            pl.BlockSpec((r, d), lambda l, j: (0, 0)),
            pl.BlockSpec((None, d, tn), lambda l, j: (l, 0, j)),
            pl.BlockSpec((None, 1, tn), lambda l, j: (l, 0, j)),
        ],
        out_specs=pl.BlockSpec((None, r, tn), lambda l, j: (l, 0, j)),
        out_shape=jax.ShapeDtypeStruct((depth, r, n), F32),
        compiler_params=_cparams(("parallel", "parallel")),
        name="adaln",
    )(c_all, ada_w, ada_b.reshape(depth, 1, n))


def _inproj_kernel(*refs, has_gate, hm, nch):
    hm_refs = ()
    gt_ref = wg_ref = None
    if has_gate:
        x_ref, g_ref, sc_ref, sh_ref, w_ref, wg_ref, z_ref, gt_ref, h_even, h_odd = refs
    elif hm:
        x_ref, g_ref, sc_ref, sh_ref, w_ref, z_ref = refs[:6]
        hm_refs = refs[6:-2]
        h_even, h_odd = refs[-2:]
    else:
        x_ref, g_ref, sc_ref, sh_ref, w_ref, z_ref, h_even, h_odd = refs
    i = pl.program_id(0)
    j = pl.program_id(1)
    tm = x_ref.shape[0]
    rc = tm // nch

    def normed(rows):
        x = x_ref[rows, :]
        sc = sc_ref[...] if sc_ref.shape[0] == 1 else sc_ref[rows, :]
        sh = sh_ref[...] if sh_ref.shape[0] == 1 else sh_ref[rows, :]
        r = x * lax.rsqrt(jnp.mean(x * x, axis=-1, keepdims=True) + EPS)
        return ((r * g_ref[...]) * (1.0 + sc) + sh).astype(BF16)

    @pl.when((i == 0) & (j == 0))
    def _():
        h_even[...] = normed(slice(None))

    def step(h_cur, h_nxt):
        if has_gate:
            @pl.when(j == 0)
            def _():
                gt_ref[...] = jnp.dot(h_cur[...], wg_ref[...], preferred_element_type=F32)

        c = jnp.clip(j - 1, 0, nch - 1)
        rows = pl.ds(pl.multiple_of(c * rc, rc), rc)
        h_nxt[rows, :] = normed(rows)
        res = jnp.dot(h_cur[...], w_ref[...], preferred_element_type=F32)
        z_ref[...] = res

        if hm:
            first, count, heads = hm
            dh = SEG // heads
            for idx, hm_ref in enumerate(hm_refs):
                @pl.when(j == first + idx)
                def _(hm_ref=hm_ref):
                    for hh in range(heads):
                        hm_ref[:, hh, :] = res[:, hh * dh:(hh + 1) * dh]

    pl.when(i % 2 == 0)(functools.partial(step, h_even, h_odd))
    pl.when(i % 2 == 1)(functools.partial(step, h_odd, h_even))


def _inproj(x2, g_pre, scale, shift, w_bf, wg_bf, *, rows_per_group, tm, hm=None):
    r, d = x2.shape
    n = w_bf.shape[1]
    nseg = n // SEG
    n_tiles = r // tm
    has_gate = wg_bf is not None
    nch = 1 << ((nseg - 1).bit_length() - 1)

    def ahead(i, j):
        return jnp.where((i == 0) & (j == 0), 0, jnp.minimum(i + 1, n_tiles - 1))

    if scale.ndim == 3:
        per = rows_per_group // tm
        mod_spec = pl.BlockSpec((None, 1, d), lambda i, j: (ahead(i, j) // per, 0, 0))
    else:
        mod_spec = pl.BlockSpec((tm, d), lambda i, j: (ahead(i, j), 0))
    x_mode = {"pipeline_mode": pl.Buffered(1)} if hm else {}
    in_specs = [
        pl.BlockSpec((tm, d), lambda i, j: (ahead(i, j), 0), **x_mode),
        pl.BlockSpec((1, d), lambda i, j: (0, 0)),
        mod_spec,
        mod_spec,
        pl.BlockSpec((d, SEG), lambda i, j: (0, j)),
    ]
    args = [x2, g_pre.reshape(1, d), scale, shift, w_bf]
    out_specs = [pl.BlockSpec((None, tm, SEG), lambda i, j: (j, i, 0))]
    out_shape = [jax.ShapeDtypeStruct((nseg, r, SEG), F32)]
    if has_gate:
        in_specs.append(pl.BlockSpec((d, 128), lambda i, j: (0, 0)))
        args.append(wg_bf)
        out_specs.append(pl.BlockSpec((tm, 128), lambda i, j: (i, 0)))
        out_shape.append(jax.ShapeDtypeStruct((r, 128), F32))
    if hm:
        first, count, heads = hm
        for _ in range(count):
            out_specs.append(pl.BlockSpec((tm, heads, SEG // heads), lambda i, j: (i, 0, 0),
                                          pipeline_mode=pl.Buffered(1)))
            out_shape.append(jax.ShapeDtypeStruct((r, heads, SEG // heads), F32))
    res = pl.pallas_call(
        functools.partial(_inproj_kernel, has_gate=has_gate, hm=hm, nch=nch),
        grid=(n_tiles, nseg),
        in_specs=in_specs,
        out_specs=out_specs,
        out_shape=out_shape,
        scratch_shapes=[pltpu.VMEM((tm, d), BF16), pltpu.VMEM((tm, d), BF16)],
        compiler_params=_cparams(("arbitrary", "arbitrary")),
        name="inproj_gate" if has_gate else "inproj",
    )(*args)
    if hm:
        return res[0], list(res[1:])
    return (res[0], res[1]) if has_gate else (res[0], None)


def _outproj_kernel(*refs, n_a):
    a_refs = refs[:n_a]
    w_refs = refs[n_a:2 * n_a]
    x_ref, gate_ref, gp_ref, o_ref = refs[2 * n_a:]
    y = None
    for a_ref, w_ref in zip(a_refs, w_refs):
        t = jnp.dot(a_ref[...].astype(BF16), w_ref[...], preferred_element_type=F32)
        y = t if y is None else y + t
    r = y * lax.rsqrt(jnp.mean(y * y, axis=-1, keepdims=True) + EPS)
    o_ref[...] = x_ref[...] + gate_ref[...] * (r * gp_ref[...])


def _outproj(a_list, w_list, x2, gate, g_post, *, rows_per_group, tm):
    r, d = x2.shape
    n_a = len(a_list)
    if gate.ndim == 3:
        per = rows_per_group // tm
        gate_spec = pl.BlockSpec((None, 1, d), lambda i: (i // per, 0, 0))
    else:
        gate_spec = pl.BlockSpec((tm, d), lambda i: (i, 0))
    in_specs = [pl.BlockSpec((tm, a.shape[1]), lambda i: (i, 0)) for a in a_list]
    in_specs += [pl.BlockSpec(w.shape, lambda i: (0, 0)) for w in w_list]
    in_specs += [pl.BlockSpec((tm, d), lambda i: (i, 0)), gate_spec, pl.BlockSpec((1, d), lambda i: (0, 0))]
    return pl.pallas_call(
        functools.partial(_outproj_kernel, n_a=n_a),
        grid=(r // tm,),
        in_specs=in_specs,
        out_specs=pl.BlockSpec((tm, d), lambda i: (i, 0)),
        out_shape=jax.ShapeDtypeStruct((r, d), F32),
        compiler_params=_cparams(("parallel",)),
        name="outproj",
    )(*a_list, *w_list, x2, gate, g_post.reshape(1, d))


MOBA_AUX = 128
ALIBI_COL = 8
MOBA_HEADS_PER_STEP = 2


def _moba_kernel(slopes_ref, q_ref, k_ref, v_ref, ga_ref, o_ref, kaug_scr, vt_scr, *, nb, hps):
    blk = MOBA_BLOCK
    t = nb * blk
    log2e = math.log2(math.e)
    scale2 = DH_A ** -0.5 * log2e
    nt = (((1,), (1,)), ((), ()))
    pos = lax.broadcasted_iota(jnp.int32, (t, MOBA_AUX), 0)
    lane = lax.broadcasted_iota(jnp.int32, (t, MOBA_AUX), 1)
    posf = pos.astype(F32)
    pos_hi = posf.astype(BF16).astype(F32)
    pos_lo = posf - pos_hi
    aux_b = jnp.where(lane < ALIBI_COL, jnp.where(lane == pos // blk, 1.0, 0.0),
                      jnp.where(lane < ALIBI_COL + 3, pos_hi,
                                jnp.where(lane < ALIBI_COL + 6, pos_lo, 0.0))).astype(BF16)
    srow = lax.broadcasted_iota(jnp.int32, (8, blk), 0)
    rowb = lax.broadcasted_iota(jnp.int32, (nb, blk), 0)
    krow = lax.broadcasted_iota(jnp.int32, (blk, blk), 0)
    qcol = lax.broadcasted_iota(jnp.int32, (blk, blk), 1)

    def head(hh):
        hsl = slice(hh * DH_A, (hh + 1) * DH_A)
        k = k_ref[:, hsl]
        kmean_b = jnp.concatenate(
            [jnp.mean(k[j * blk:(j + 1) * blk, :], axis=0, keepdims=True) for j in range(nb)],
            axis=0).astype(BF16)
        sl2 = jnp.full((8, blk), slopes_ref[pl.program_id(1) * hps + hh] * log2e, F32)
        sl_hi = sl2.astype(BF16).astype(F32)
        sl_mid = (sl2 - sl_hi).astype(BF16).astype(F32)
        sl_lo = sl2 - sl_hi - sl_mid
        slope_rows = jnp.where(srow >= 6, 0.0,
                               jnp.where(srow % 3 == 0, sl_hi, jnp.where(srow % 3 == 1, sl_mid, sl_lo)))
        kaug_scr[hh] = jnp.concatenate([k.astype(BF16), aux_b], axis=1)
        vt_scr[hh] = jnp.transpose(v_ref[:, hsl]).astype(BF16)

        def scores(n):
            nk = (n + 1) * blk
            q = q_ref[n * blk:nk, hsl]
            if n > MOBA_TOPK:
                gate = lax.dot_general(kmean_b, q.astype(BF16), nt, preferred_element_type=F32)
                gm = jnp.where(rowb < n, gate, -jnp.inf)
                cnt = jnp.zeros((nb, blk), F32)
                for kk in range(n):
                    gk = gm[kk:kk + 1, :]
                    beats = (gk > gm) | ((gk == gm) & (rowb > kk))
                    cnt = cnt + jnp.where(beats, 1.0, 0.0)
                keep = ((rowb < n) & (cnt < MOBA_TOPK)) | (rowb == n)
                bias = jnp.where(keep, 0.0, NEG_BIG)
            else:
                bias = jnp.zeros((nb, blk), F32)
            pieces = [bias, slope_rows]
            if nb < ALIBI_COL:
                pieces.insert(1, jnp.zeros((ALIBI_COL - nb, blk), F32))
            pieces.append(jnp.zeros((MOBA_AUX - ALIBI_COL - 8, blk), F32))
            rhs = jnp.concatenate([(jnp.transpose(q) * scale2).astype(BF16),
                                   jnp.concatenate(pieces, axis=0).astype(BF16)], axis=0)
            return jnp.dot(kaug_scr[hh, 0:nk, :], rhs, preferred_element_type=F32)

        def finish(n, s):
            nk = (n + 1) * blk
            s_own = jnp.where(krow <= qcol, s[n * blk:nk, :], -jnp.inf)
            m = jnp.max(s_own, axis=0, keepdims=True)
            if n > 0:
                m = jnp.maximum(m, jnp.max(s[0:n * blk, :], axis=0, keepdims=True))
            p_own = jnp.exp2(s_own - m)
            l = jnp.sum(p_own, axis=0, keepdims=True)
            ot = jnp.dot(vt_scr[hh, :, n * blk:nk], p_own.astype(BF16), preferred_element_type=F32)
            if n > 0:
                p = jnp.exp2(s[0:n * blk, :] - m)
                l = l + jnp.sum(p, axis=0, keepdims=True)
                ot = ot + jnp.dot(vt_scr[hh, :, 0:n * blk], p.astype(BF16), preferred_element_type=F32)
            o = jnp.transpose(ot / l)
            o_ref[n * blk:nk, hsl] = (o * _silu(ga_ref[n * blk:nk, hsl])).astype(o_ref.dtype)

        s_cur = scores(0)
        for n in range(nb):
            s_nxt = scores(n + 1) if n + 1 < nb else None
            finish(n, s_cur)
            s_cur = s_nxt

    for hh in range(hps):
        head(hh)


def _moba_prompt(z4, slopes):
    _, b, t, w = z4.shape
    nb = t // MOBA_BLOCK
    assert nb <= ALIBI_COL
    hps = MOBA_HEADS_PER_STEP

    def seg_spec(sidx):
        return pl.BlockSpec((None, None, t, hps * DH_A), lambda bi, hg, s: (sidx, bi, 0, hg))

    grid_spec = pltpu.PrefetchScalarGridSpec(
        num_scalar_prefetch=1,
        grid=(b, H_A // hps),
        in_specs=[seg_spec(0), seg_spec(1), seg_spec(2), seg_spec(3)],
        out_specs=pl.BlockSpec((None, t, hps * DH_A), lambda bi, hg, s: (bi, 0, hg)),
        scratch_shapes=[
            pltpu.VMEM((hps, t, DH_A + MOBA_AUX), BF16),
            pltpu.VMEM((hps, DH_A, t), BF16),
        ],
    )
    return pl.pallas_call(
        functools.partial(_moba_kernel, nb=nb, hps=hps),
        grid_spec=grid_spec,
        out_shape=jax.ShapeDtypeStruct((b, t, w), BF16),
        compiler_params=_cparams(("parallel", "parallel")),
        name="moba_prompt",
    )(slopes, z4, z4, z4, z4)


def _pool_kernel(u_ref, gb_ref, prev_ref, pw_ref, ps_ref, o_ref, pn_ref, ext_scr, *, tt, pos0, nt):
    t = pl.program_id(1)
    hist = POOL_HIST + 1

    @pl.when(t == 0)
    def _():
        ext_scr[0:1, :] = jnp.zeros((1, ext_scr.shape[1]), F32)
        ext_scr[1:hist, :] = prev_ref[...]

    ext_scr[hist:hist + tt, :] = u_ref[...]
    pos = pos0 + t * tt + lax.broadcasted_iota(jnp.int32, (tt, 1), 0)
    for g, w in enumerate(POOL_WINDOWS):
        sl = slice(g * GW_B, (g + 1) * GW_B)
        x = u_ref[:, sl]
        acc = x
        for jj in range(1, w):
            acc = acc + ext_scr[hist - jj:hist - jj + tt, sl]
        cnt = jnp.minimum(w, pos + 1).astype(F32)
        d = acc / cnt - x
        y = jnp.dot(d.astype(BF16), pw_ref[g], preferred_element_type=F32) * ps_ref[:, sl]
        o_ref[:, sl] = (y * _silu(gb_ref[:, sl])).astype(o_ref.dtype)

    tail = ext_scr[tt:tt + hist, :]
    ext_scr[0:hist, :] = tail

    @pl.when(t == nt - 1)
    def _():
        pn_ref[...] = ext_scr[1:hist, :]


def _pool(z4, prev, pool_w_bf, pool_scale, *, pos0, tt, out_dtype):
    _, b, t, w = z4.shape
    nt = t // tt
    return pl.pallas_call(
        functools.partial(_pool_kernel, tt=tt, pos0=pos0, nt=nt),
        grid=(b, nt),
        in_specs=[
            pl.BlockSpec((None, None, tt, w), lambda bi, ti: (4, bi, ti, 0)),
            pl.BlockSpec((None, None, tt, w), lambda bi, ti: (5, bi, ti, 0)),
            pl.BlockSpec((None, POOL_HIST, w), lambda bi, ti: (bi, 0, 0)),
            pl.BlockSpec(pool_w_bf.shape, lambda bi, ti: (0, 0, 0)),
            pl.BlockSpec((1, w), lambda bi, ti: (0, 0)),
        ],
        out_specs=[
            pl.BlockSpec((None, tt, w), lambda bi, ti: (bi, ti, 0)),
            pl.BlockSpec((None, POOL_HIST, w), lambda bi, ti: (bi, 0, 0)),
        ],
        out_shape=[
            jax.ShapeDtypeStruct((b, t, w), out_dtype),
            jax.ShapeDtypeStruct((b, POOL_HIST, w), F32),
        ],
        scratch_shapes=[pltpu.VMEM((POOL_HIST + 1 + tt, w), F32)],
        compiler_params=_cparams(("parallel", "arbitrary")),
        name="pool",
    )(z4, z4, prev, pool_w_bf, pool_scale.reshape(1, w))


def _mlstm_kernel(*refs, lv, lp, zero_state):
    nseg = 2 * 5
    if zero_state:
        bias_ref = refs[0]
        seg_refs = refs[1:1 + nseg]
        gt_ref, hn_ref = refs[1 + nseg:3 + nseg]
        rest = refs[3 + nseg:]
    else:
        bias_ref, m0_ref = refs[:2]
        seg_refs = refs[2:2 + nseg]
        gt_ref, hn_ref, c0_ref, n0_ref = refs[2 + nseg:6 + nseg]
        rest = refs[6 + nseg:]
    h_out, c_out, n_out, m_out, c_scr, n_scr, m_scr, pad_scr = rest
    bi = pl.program_id(0)
    c = pl.program_id(1)
    nc = pl.num_programs(1)
    hpseg = SEG // DH_C

    @pl.when(c == 0)
    def _():
        if zero_state:
            c_scr[...] = jnp.zeros(c_scr.shape, F32)
            n_scr[...] = jnp.zeros(n_scr.shape, F32)
            m_scr[...] = jnp.zeros(m_scr.shape, F32)
        else:
            c_scr[...] = c0_ref[...]
            n_scr[...] = n0_ref[...]
            for h in range(H_C):
                m_scr[h] = jnp.full(m_scr.shape[1:], m0_ref[bi, h], F32)

    def load(ref, slot):
        if lv == lp:
            return ref
        pad_scr[slot] = jnp.zeros(pad_scr.shape[1:], F32)
        pad_scr[slot, 0:lv, 0:ref.shape[1]] = ref[...]
        return pad_scr.at[slot]

    srcs = [load(r, i) for i, r in enumerate(seg_refs)]
    gt_src = load(gt_ref, nseg)

    lane = lax.broadcasted_iota(jnp.int32, (1, 128), 1)
    bias_row = jnp.zeros((1, 128), F32)
    for h in range(H_C):
        bias_row = jnp.where(lane == h, bias_ref[0, h], bias_row)
        bias_row = jnp.where(lane == H_C + h, bias_ref[1, h], bias_row)
    gpre = gt_src[:, 0:128] + bias_row
    ig_all = gpre
    lf_all = _log_sigmoid(gpre)
    if lv != lp:
        row1 = lax.broadcasted_iota(jnp.int32, (lp, 1), 0)
        ig_all = jnp.where(row1 < lv, ig_all, NEG_BIG)
        lf_all = jnp.where(row1 < lv, lf_all, 0.0)
    rr = lax.broadcasted_iota(jnp.int32, (lp, lp), 0)
    cc = lax.broadcasted_iota(jnp.int32, (lp, lp), 1)
    tril = rr >= cc
    tril_b = jnp.where(tril, 1.0, 0.0).astype(BF16)
    lf_hi = lf_all.astype(BF16)
    r1 = lf_all - lf_hi.astype(F32)
    lf_mid = r1.astype(BF16)
    lf_lo = (r1 - lf_mid.astype(F32)).astype(BF16)
    b_all = (jnp.dot(tril_b, lf_hi, preferred_element_type=F32)
             + jnp.dot(tril_b, lf_mid, preferred_element_type=F32)
             + jnp.dot(tril_b, lf_lo, preferred_element_type=F32))
    lane_f = lax.broadcasted_iota(jnp.int32, (lp, 128), 1)
    rows_t = jnp.transpose(jnp.where(lane_f < H_C, ig_all, b_all))
    nt = (((1,), (1,)), ((), ()))

    def front(h):
        sl = slice((h % hpseg) * DH_C, (h % hpseg + 1) * DH_C)
        q = srcs[0 + h // hpseg][:, sl]
        k = srcs[2 + h // hpseg][:, sl] * (DH_C ** -0.5)
        v = srcs[4 + h // hpseg][:, sl]
        ig_col = ig_all[:, h:h + 1]
        b_col = b_all[:, H_C + h:H_C + h + 1]
        ig_row = rows_t[h:h + 1, :]
        b_row = rows_t[H_C + h:H_C + h + 1, :]
        dmat = jnp.where(tril, b_col - b_row + ig_row, -jnp.inf)
        m_prev = m_scr[h]
        carry = b_col + m_prev
        mt = jnp.maximum(carry, jnp.max(dmat, axis=1, keepdims=True))
        qb = q.astype(BF16)
        kb = k.astype(BF16)
        qk = lax.dot_general(qb, kb, nt, preferred_element_type=F32)
        cq = lax.dot_general(qb, c_scr[h].astype(BF16), nt, preferred_element_type=F32)
        return dict(sl=sl, q=q, k=k, v=v, kb=kb, ig_col=ig_col, b_col=b_col, m_prev=m_prev, carry=carry,
                    mt=mt, qk=qk, cq=cq, decay_w=jnp.exp(dmat - mt))

    def back(h, f):
        sl, q, k, v, kb, mt, carry = f["sl"], f["q"], f["k"], f["v"], f["kb"], f["mt"], f["carry"]
        s = f["qk"] * f["decay_w"]
        inter = jnp.exp(carry - mt)
        num = jnp.dot(s.astype(BF16), v.astype(BF16), preferred_element_type=F32) + inter * f["cq"]
        nq = jnp.sum(q * n_scr[h], axis=1, keepdims=True)
        den = jnp.sum(s, axis=1, keepdims=True) + inter * nq
        hc = num / jnp.maximum(jnp.abs(den), jnp.exp(-mt))

        m_new = mt[lp - 1:lp, :]
        b_last = f["b_col"][lp - 1:lp, :]
        w_col = jnp.exp(b_last - f["b_col"] + f["ig_col"] - m_new)
        decay = jnp.exp(b_last + f["m_prev"] - m_new)
        vw = (v * w_col).astype(BF16)
        c_scr[h] = decay * c_scr[h] + lax.dot_general(
            vw, kb, (((0,), (0,)), ((), ())), preferred_element_type=F32)
        n_scr[h] = decay * n_scr[h] + jnp.sum(k * w_col, axis=0, keepdims=True)
        m_scr[h] = m_new

        hsl = slice(h * DH_C, (h + 1) * DH_C)
        hc = hc * _sigmoid(srcs[6 + h // hpseg][:, sl])
        hc = hc * lax.rsqrt(jnp.mean(hc * hc, axis=-1, keepdims=True) + EPS) * hn_ref[:, hsl]
        res = (hc * _silu(srcs[8 + h // hpseg][:, sl])).astype(h_out.dtype)
        h_out[:, hsl] = res if lv == lp else res[0:lv, :]

    f_cur = front(0)
    for h in range(H_C):
        f_nxt = front(h + 1) if h + 1 < H_C else None
        back(h, f_cur)
        f_cur = f_nxt

    @pl.when(c == nc - 1)
    def _():
        c_out[...] = c_scr[...]
        n_out[...] = n_scr[...]
        m_out[...] = jnp.broadcast_to(m_scr[...], m_out.shape)


def _mlstm(z4, gates3, bias, head_norm, state, *, chunk, lp):
    nseg, b, t, _ = z4.shape
    nc = t // chunk
    w_c = H_C * DH_C
    zero_state = state is None
    n_pref = 1 if zero_state else 2

    def seg_spec(sidx):
        return pl.BlockSpec((None, None, chunk, SEG), lambda bi, c, *_: (sidx, bi, c, 0))

    def state_spec(*tail):
        return pl.BlockSpec((None, H_C) + tail, lambda bi, c, *_: (bi,) + (0,) * (len(tail) + 1))

    in_specs = [seg_spec(i) for i in range(nseg)]
    in_specs += [pl.BlockSpec((None, chunk, 128), lambda bi, c, *_: (bi, c, 0)),
                 pl.BlockSpec((1, w_c), lambda bi, c, *_: (0, 0))]
    args = [z4] * nseg + [gates3, head_norm.reshape(1, w_c)]
    prefetch = [bias]
    if not zero_state:
        c0, n0, m0 = state
        prefetch.append(m0)
        in_specs += [state_spec(DH_C, DH_C), state_spec(1, DH_C)]
        args += [c0, n0]
    grid_spec = pltpu.PrefetchScalarGridSpec(
        num_scalar_prefetch=n_pref,
        grid=(b, nc),
        in_specs=in_specs,
        out_specs=[
            pl.BlockSpec((None, chunk, w_c), lambda bi, c, *_: (bi, c, 0)),
            state_spec(DH_C, DH_C),
            state_spec(1, DH_C),
            state_spec(1, 128),
        ],
        scratch_shapes=[
            pltpu.VMEM((H_C, DH_C, DH_C), F32),
            pltpu.VMEM((H_C, 1, DH_C), F32),
            pltpu.VMEM((H_C, 1, 1), F32),
            pltpu.VMEM((nseg + 1, lp, SEG), F32),
        ],
    )
    return pl.pallas_call(
        functools.partial(_mlstm_kernel, lv=chunk, lp=lp, zero_state=zero_state),
        grid_spec=grid_spec,
        out_shape=[
            jax.ShapeDtypeStruct((b, t, w_c), BF16),
            jax.ShapeDtypeStruct((b, H_C, DH_C, DH_C), F32),
            jax.ShapeDtypeStruct((b, H_C, 1, DH_C), F32),
            jax.ShapeDtypeStruct((b, H_C, 1, 128), F32),
        ],
        compiler_params=_cparams(("parallel", "arbitrary")),
        name="mlstm",
    )(*prefetch, *args)


PAGES_PER_STEP = 16


def _sample_attn_kernel(pt_ref, slopes_ref, *refs, n_steps, tq, past_len):
    pp = PAGES_PER_STEP
    k_refs = refs[:pp]
    v_refs = refs[pp:2 * pp]
    qkvg_refs = refs[2 * pp:2 * pp + 4]
    o_ref, s_scr, pmax_scr, kmean_scr, acc_scr, l_scr, m_scr, sel_scr = refs[2 * pp + 4:]
    q_ref, kn_ref, vn_ref, ga_ref = qkvg_refs
    s = pl.program_id(1)
    rows = tq * H_A
    cols = PAGE_SIZE * H_A
    ppb = MOBA_BLOCK // PAGE_SIZE
    n_pages = n_steps * pp
    nt = (((1,), (1,)), ((), ()))
    log2e = math.log2(math.e)
    scale2 = DH_A ** -0.5 * log2e

    row = lax.broadcasted_iota(jnp.int32, (rows, 1), 0)
    row_h = row % H_A
    qpos = past_len + row // H_A
    slope2 = jnp.zeros((rows, 1), F32)
    for hh in range(H_A):
        slope2 = jnp.where(row_h == hh, slopes_ref[hh] * log2e, slope2)
    lane = lax.broadcasted_iota(jnp.int32, (1, cols), 1)

    def chunk(g):
        return pl.ds(pl.multiple_of(g * cols, cols), cols)

    def fold_lanes(x, op):
        acc = x[:, 0:128]
        for kk in range(1, cols // 128):
            acc = op(acc, x[:, kk * 128:(kk + 1) * 128])
        return acc

    def picked(sel, g):
        blk = g // ppb
        return (sel[0] == blk) | (sel[1] == blk) | (sel[2] == blk)

    @pl.when(s < n_steps)
    def _():
        qb = q_ref[...].reshape(rows, DH_A).astype(BF16)
        for jj in range(pp // ppb):
            tot = None
            for p in range(ppb):
                part = jnp.sum(k_refs[jj * ppb + p][...], axis=0)
                tot = part if tot is None else tot + part
            kmean_scr[pl.ds(pl.multiple_of((s * (pp // ppb) + jj) * H_A, H_A), H_A), :] = tot * (1.0 / MOBA_BLOCK)
        head_ok = (lane % H_A) == row_h
        qoff = slope2 * qpos.astype(F32)
        for p in range(pp):
            g = s * pp + p
            k2 = k_refs[p][...].reshape(cols, DH_A).astype(BF16)
            st = lax.dot_general(qb, k2, nt, preferred_element_type=F32)
            kpos = (g * PAGE_SIZE + lane // H_A).astype(F32)
            sv = st * scale2 + (slope2 * kpos - qoff)
            sv = jnp.where(head_ok, sv, -jnp.inf)
            s_scr[:, chunk(g)] = sv
            pmax_scr[g] = fold_lanes(sv, jnp.maximum)

    @pl.when(s == n_steps)
    def _():
        qb = q_ref[...].reshape(rows, DH_A).astype(BF16)
        nl = kmean_scr.shape[0]
        gate = lax.dot_general(qb, kmean_scr[...].astype(BF16), nt, preferred_element_type=F32)
        glane = lax.broadcasted_iota(jnp.int32, (rows, nl), 1)
        gate = jnp.where(glane % H_A == row_h, gate, -jnp.inf)
        sel = []
        sel_lane = lax.broadcasted_iota(jnp.int32, sel_scr.shape, 1)
        sel_tile = jnp.zeros(sel_scr.shape, jnp.int32)
        for t in range(MOBA_TOPK):
            mx = jnp.max(gate, axis=1, keepdims=True)
            idx = jnp.min(jnp.where(gate == mx, glane, nl), axis=1, keepdims=True)
            sel.append(idx // H_A)
            sel_tile = jnp.where(sel_lane == t, idx // H_A, sel_tile)
            gate = jnp.where(glane == idx, -jnp.inf, gate)
        sel_scr[...] = sel_tile

        kn2 = kn_ref[...].reshape(rows, DH_A).astype(BF16)
        olane = lax.broadcasted_iota(jnp.int32, (1, rows), 1)
        own_pos = past_len + olane // H_A
        s_own = lax.dot_general(qb, kn2, nt, preferred_element_type=F32) * scale2
        s_own = s_own - slope2 * (qpos - own_pos).astype(F32)
        s_own = jnp.where(((olane % H_A) == row_h) & (own_pos <= qpos), s_own, -jnp.inf)

        def max_body(g, macc):
            return jnp.maximum(macc, jnp.where(picked(sel, g), pmax_scr[g], -jnp.inf))

        macc = lax.fori_loop(0, n_pages, max_body, jnp.full((rows, 128), -jnp.inf, F32), unroll=4)
        m = jnp.maximum(jnp.max(macc, axis=1, keepdims=True), jnp.max(s_own, axis=1, keepdims=True))
        m_scr[...] = m
        p_own = jnp.exp2(s_own - m)
        l_scr[...] = jnp.where(lax.broadcasted_iota(jnp.int32, l_scr.shape, 1) == 0,
                               jnp.sum(p_own, axis=1, keepdims=True), 0.0)
        vn2 = vn_ref[...].reshape(rows, DH_A).astype(BF16)
        acc_scr[...] = jnp.dot(p_own.astype(BF16), vn2, preferred_element_type=F32)

    @pl.when(s >= n_steps)
    def _():
        sel = [sel_scr[:, t:t + 1] for t in range(MOBA_TOPK)]
        m = m_scr[...]
        acc = acc_scr[...]
        lacc = l_scr[...]
        for p in range(pp):
            g = (s - n_steps) * pp + p
            pr = jnp.exp2(s_scr[:, chunk(g)] - jnp.where(picked(sel, g), m, jnp.inf))
            lacc = lacc + fold_lanes(pr, jnp.add)
            v2 = v_refs[p][...].reshape(cols, DH_A).astype(BF16)
            acc = acc + jnp.dot(pr.astype(BF16), v2, preferred_element_type=F32)
        acc_scr[...] = acc
        l_scr[...] = lacc

    @pl.when(s == 2 * n_steps - 1)
    def _():
        o = acc_scr[...] / jnp.sum(l_scr[...], axis=1, keepdims=True)
        o = o * _silu(ga_ref[...].reshape(rows, DH_A))
        o_ref[...] = o.reshape(tq, H_A, DH_A)


def _sample_attn(cache_k4, cache_v4, page_table, slopes, qkvg):
    b, n_pages = page_table.shape
    _, tq, _, _ = qkvg[0].shape
    pp = PAGES_PER_STEP
    n_steps = n_pages // pp
    past_len = n_pages * PAGE_SIZE
    nbp = past_len // MOBA_BLOCK
    rows = tq * H_A

    def k_spec(p):
        return pl.BlockSpec((None, PAGE_SIZE, H_A, DH_A),
                            lambda bi, s, pt, sl: (pt[bi, jnp.minimum(s, n_steps - 1) * pp + p], 0, 0, 0))

    def v_spec(p):
        return pl.BlockSpec((None, PAGE_SIZE, H_A, DH_A),
                            lambda bi, s, pt, sl: (pt[bi, jnp.maximum(s - n_steps, 0) * pp + p], 0, 0, 0))

    new_spec = pl.BlockSpec((None, tq, H_A, DH_A), lambda bi, s, pt, sl: (bi, 0, 0, 0))

    grid_spec = pltpu.PrefetchScalarGridSpec(
        num_scalar_prefetch=2,
        grid=(b, 2 * n_steps),
        in_specs=[k_spec(p) for p in range(pp)] + [v_spec(p) for p in range(pp)] + [new_spec] * 4,
        out_specs=pl.BlockSpec((None, tq, H_A, DH_A), lambda bi, s, pt, sl: (bi, 0, 0, 0)),
        scratch_shapes=[
            pltpu.VMEM((rows, n_pages * PAGE_SIZE * H_A), F32),
            pltpu.VMEM((n_pages, rows, 128), F32),
            pltpu.VMEM((nbp * H_A, DH_A), F32),
            pltpu.VMEM((rows, DH_A), F32),
            pltpu.VMEM((rows, 128), F32),
            pltpu.VMEM((rows, 1), F32),
            pltpu.VMEM((rows, 128), jnp.int32),
        ],
    )
    return pl.pallas_call(
        functools.partial(_sample_attn_kernel, n_steps=n_steps, tq=tq, past_len=past_len),
        grid_spec=grid_spec,
        out_shape=jax.ShapeDtypeStruct((b, tq, H_A, DH_A), F32),
        compiler_params=_cparams(("parallel", "arbitrary")),
        name="sample_attn",
    )(page_table, slopes, *([cache_k4] * pp), *([cache_v4] * pp), *qkvg)


def _alibi_slopes(n_heads):
    return 2.0 ** (-8.0 * jnp.arange(1, n_heads + 1, dtype=F32) / n_heads)


def kernel(x_prompt, x_sample, cache_k, cache_v, state_pool, state_C, state_n, state_m, page_table,
           c_prompt, c_sample, ada_w, ada_b, norm_pre, norm_post, w_in_even, pool_w, pool_scale,
           w_out_even, w_in_odd, b_igate, b_fgate, head_norm, w_out_odd):
    bp, tp, d = x_prompt.shape
    bs, ts, _ = x_sample.shape
    n_pages = page_table.shape[1]
    past_len = n_pages * PAGE_SIZE
    w_a = H_A * DH_A
    w_c = H_C * DH_C
    rp, rs = bp * tp, bs * ts
    tm_p = 1024

    mods = _adaln(jnp.concatenate([c_prompt, c_sample], axis=0), ada_w, ada_b)

    def mod_parts(l):
        m = mods[l]
        shift, scale, gate = m[:, :d], m[:, d:2 * d], m[:, 2 * d:]
        prompt = tuple(a[:bp].reshape(bp, 1, d) for a in (shift, scale, gate))
        sample = tuple(jnp.repeat(a[bp:], ts, axis=0) for a in (shift, scale, gate))
        return prompt, sample

    slopes = _alibi_slopes(H_A)
    xp = x_prompt.reshape(rp, d)
    xs = x_sample.reshape(rs, d)

    (sh_p, sc_p, gt_p), (sh_s, sc_s, gt_s) = mod_parts(0)
    w_in0 = w_in_even[0].astype(BF16)
    zp, kv_p = _inproj(xp, norm_pre[0], sc_p, sh_p, w_in0, None, rows_per_group=tp, tm=tm_p, hm=(1, 2, H_A))
    zs, qkvg_s = _inproj(xs, norm_pre[0], sc_s, sh_s, w_in0, None, rows_per_group=ts, tm=rs, hm=(0, 4, H_A))
    qkvg_s = [a.reshape(bs, ts, H_A, DH_A) for a in qkvg_s]
    kv_s = qkvg_s[1:3]
    z4p = zp.reshape(zp.shape[0], bp, tp, SEG)
    z4s = zs.reshape(zs.shape[0], bs, ts, SEG)

    pool_w_bf = pool_w[0].astype(BF16)
    att_p = _moba_prompt(z4p, slopes)
    pool_p, pstate_p = _pool(z4p, jnp.zeros((bp, POOL_HIST, SEG), F32), pool_w_bf, pool_scale[0],
                             pos0=0, tt=512, out_dtype=BF16)

    att_s = _sample_attn(cache_k[0], cache_v[0], page_table, slopes, qkvg_s)
    pool_s, pstate_s = _pool(z4s, state_pool[0], pool_w_bf, pool_scale[0],
                             pos0=past_len, tt=ts, out_dtype=F32)

    w_out0 = w_out_even[0].astype(BF16)
    w_list0 = [w_out0[:w_a], w_out0[w_a:]]
    xp1 = _outproj([att_p.reshape(rp, w_a), pool_p.reshape(rp, SEG)], w_list0, xp, gt_p, norm_post[0],
                   rows_per_group=tp, tm=512)
    xs1 = _outproj([att_s.reshape(rs, w_a), pool_s.reshape(rs, SEG)], w_list0, xs, gt_s, norm_post[0],
                   rows_per_group=ts, tm=rs)

    (sh_p, sc_p, gt_p), (sh_s, sc_s, gt_s) = mod_parts(1)
    n_main = 5 * w_c
    w_in1 = w_in_odd[0].astype(BF16)
    wg1 = jnp.pad(w_in_odd[0][:, n_main:], ((0, 0), (0, 128 - 2 * H_C))).astype(BF16)
    zp, gates_p = _inproj(xp1, norm_pre[1], sc_p, sh_p, w_in1, wg1, rows_per_group=tp, tm=tm_p)
    zs, gates_s = _inproj(xs1, norm_pre[1], sc_s, sh_s, w_in1, wg1, rows_per_group=ts, tm=rs)
    bias = jnp.stack([b_igate[0], b_fgate[0]]).astype(F32)
    hc_p, c_p, n_p, m_p = _mlstm(zp.reshape(zp.shape[0], bp, tp, SEG), gates_p.reshape(bp, tp, 128),
                                 bias, head_norm[0], None, chunk=256, lp=256)
    state = (state_C[0], state_n[0].reshape(bs, H_C, 1, DH_C), state_m[0])
    hc_s, c_s, n_s, m_s = _mlstm(zs.reshape(zs.shape[0], bs, ts, SEG), gates_s.reshape(bs, ts, 128),
                                 bias, head_norm[0], state, chunk=ts, lp=128)
    w_out1 = w_out_odd[0].astype(BF16)
    xp2 = _outproj([hc_p.reshape(rp, w_c)], [w_out1], xp1, gt_p, norm_post[1], rows_per_group=tp, tm=512)
    xs2 = _outproj([hc_s.reshape(rs, w_c)], [w_out1], xs1, gt_s, norm_post[1], rows_per_group=ts, tm=rs)

    kv_shape_p = (1, bp, tp, H_A, DH_A)
    kv_shape_s = (1, bs, ts, H_A, DH_A)
    return (xp2.reshape(bp, tp, d), xs2.reshape(bs, ts, d),
            kv_p[0].reshape(kv_shape_p), kv_p[1].reshape(kv_shape_p),
            kv_s[0].reshape(kv_shape_s), kv_s[1].reshape(kv_shape_s),
            pstate_p[None], pstate_s[None],
            c_p[None], n_p.reshape(1, bp, H_C, DH_C), m_p[:, :, 0, 0][None],
            c_s[None], n_s.reshape(1, bs, H_C, DH_C), m_s[:, :, 0, 0][None])
```

```python
import functools
import math

import jax
import jax.numpy as jnp
from jax import lax
from jax.experimental import pallas as pl
from jax.experimental.pallas import tpu as pltpu

F32 = jnp.float32
BF16 = jnp.bfloat16

H_A = 8
DH_A = 128
MOBA_BLOCK = 256
MOBA_TOPK = 3
POOL_WINDOWS = (2, 4, 8, 16)
GW_B = 256
POOL_HIST = max(POOL_WINDOWS) - 1
H_C = 8
DH_C = 256
EPS = 1e-6
PAGE_SIZE = 128

SEG = 1024
NEG_BIG = -1e30

VMEM_LIMIT_V7X = 52 * 1024 * 1024


def _sigmoid(x):
    return 0.5 * jnp.tanh(0.5 * x) + 0.5


def _silu(x):
    return x * _sigmoid(x)


def _log_sigmoid(x):
    return -(jnp.maximum(-x, 0.0) + jnp.log(1.0 + jnp.exp(-jnp.abs(x))))


def _cparams(sem):
    return pltpu.CompilerParams(dimension_semantics=sem, vmem_limit_bytes=VMEM_LIMIT_V7X)


def _adaln_kernel(c_ref, w_ref, b_ref, o_ref):
    s = _silu(c_ref[...]).astype(BF16)
    o_ref[...] = jnp.dot(s, w_ref[...].astype(BF16), preferred_element_type=F32) + b_ref[...]


def _adaln(c_all, ada_w, ada_b, tn=512):
    depth, d, n = ada_w.shape
    r = c_all.shape[0]
    return pl.pallas_call(
        _adaln_kernel,
        grid=(depth, n // tn),
        in_specs=[
            pl.BlockSpec((r, d), lambda l, j: (0, 0)),
            pl.BlockSpec((None, d, tn), lambda l, j: (l, 0, j)),
            pl.BlockSpec((None, 1, tn), lambda l, j: (l, 0, j)),
        ],
        out_specs=pl.BlockSpec((None, r, tn), lambda l, j: (l, 0, j)),
        out_shape=jax.ShapeDtypeStruct((depth, r, n), F32),
        compiler_params=_cparams(("parallel", "parallel")),
        name="adaln",
    )(c_all, ada_w, ada_b.reshape(depth, 1, n))


def _inproj_kernel(*refs, has_gate, hm, nch):
    hm_refs = ()
    gt_ref = wg_ref = None
    if has_gate:
        x_ref, g_ref, sc_ref, sh_ref, w_ref, wg_ref, z_ref, gt_ref, h_even, h_odd = refs
    elif hm:
        x_ref, g_ref, sc_ref, sh_ref, w_ref, z_ref = refs[:6]
        hm_refs = refs[6:-2]
        h_even, h_odd = refs[-2:]
    else:
        x_ref, g_ref, sc_ref, sh_ref, w_ref, z_ref, h_even, h_odd = refs
    i = pl.program_id(0)
    j = pl.program_id(1)
    tm = x_ref.shape[0]
    rc = tm // nch

    def normed(rows):
        x = x_ref[rows, :]
        sc = sc_ref[...] if sc_ref.shape[0] == 1 else sc_ref[rows, :]
        sh = sh_ref[...] if sh_ref.shape[0] == 1 else sh_ref[rows, :]
        r = x * lax.rsqrt(jnp.mean(x * x, axis=-1, keepdims=True) + EPS)
        return ((r * g_ref[...]) * (1.0 + sc) + sh).astype(BF16)

    @pl.when((i == 0) & (j == 0))
    def _():
        h_even[...] = normed(slice(None))

    def step(h_cur, h_nxt):
        if has_gate:
            @pl.when(j == 0)
            def _():
                gt_ref[...] = jnp.dot(h_cur[...], wg_ref[...], preferred_element_type=F32)

        c = jnp.clip(j - 1, 0, nch - 1)
        rows = pl.ds(pl.multiple_of(c * rc, rc), rc)
        h_nxt[rows, :] = normed(rows)
        res = jnp.dot(h_cur[...], w_ref[...], preferred_element_type=F32)
        z_ref[...] = res

        if hm:
            first, count, heads = hm
            dh = SEG // heads
            for idx, hm_ref in enumerate(hm_refs):
                @pl.when(j == first + idx)
                def _(hm_ref=hm_ref):
                    for hh in range(heads):
                        hm_ref[:, hh, :] = res[:, hh * dh:(hh + 1) * dh]

    pl.when(i % 2 == 0)(functools.partial(step, h_even, h_odd))
    pl.when(i % 2 == 1)(functools.partial(step, h_odd, h_even))


def _inproj(x2, g_pre, scale, shift, w_bf, wg_bf, *, rows_per_group, tm, hm=None):
    r, d = x2.shape
    n = w_bf.shape[1]
    nseg = n // SEG
    n_tiles = r // tm
    has_gate = wg_bf is not None
    nch = 1 << ((nseg - 1).bit_length() - 1)

    def ahead(i, j):
        return jnp.where((i == 0) & (j == 0), 0, jnp.minimum(i + 1, n_tiles - 1))

    if scale.ndim == 3:
        per = rows_per_group // tm
        mod_spec = pl.BlockSpec((None, 1, d), lambda i, j: (ahead(i, j) // per, 0, 0))
    else:
        mod_spec = pl.BlockSpec((tm, d), lambda i, j: (ahead(i, j), 0))
    x_mode = {"pipeline_mode": pl.Buffered(1)} if hm else {}
    in_specs = [
        pl.BlockSpec((tm, d), lambda i, j: (ahead(i, j), 0), **x_mode),
        pl.BlockSpec((1, d), lambda i, j: (0, 0)),
        mod_spec,
        mod_spec,
        pl.BlockSpec((d, SEG), lambda i, j: (0, j)),
    ]
    args = [x2, g_pre.reshape(1, d), scale, shift, w_bf]
    out_specs = [pl.BlockSpec((None, tm, SEG), lambda i, j: (j, i, 0))]
    out_shape = [jax.ShapeDtypeStruct((nseg, r, SEG), F32)]
    if has_gate:
        in_specs.append(pl.BlockSpec((d, 128), lambda i, j: (0, 0)))
        args.append(wg_bf)
        out_specs.append(pl.BlockSpec((tm, 128), lambda i, j: (i, 0)))
        out_shape.append(jax.ShapeDtypeStruct((r, 128), F32))
    if hm:
        first, count, heads = hm
        for _ in range(count):
            out_specs.append(pl.BlockSpec((tm, heads, SEG // heads), lambda i, j: (i, 0, 0),
                                          pipeline_mode=pl.Buffered(1)))
            out_shape.append(jax.ShapeDtypeStruct((r, heads, SEG // heads), F32))
    res = pl.pallas_call(
        functools.partial(_inproj_kernel, has_gate=has_gate, hm=hm, nch=nch),
        grid=(n_tiles, nseg),
        in_specs=in_specs,
        out_specs=out_specs,
        out_shape=out_shape,
        scratch_shapes=[pltpu.VMEM((tm, d), BF16), pltpu.VMEM((tm, d), BF16)],
        compiler_params=_cparams(("arbitrary", "arbitrary")),
        name="inproj_gate" if has_gate else "inproj",
    )(*args)
    if hm:
        return res[0], list(res[1:])
    return (res[0], res[1]) if has_gate else (res[0], None)


def _outproj_kernel(*refs, n_a, n_hm, heads):
    a_refs = refs[:n_a]
    w_refs = refs[n_a:2 * n_a]
    x_ref, gate_ref, gp_ref = refs[2 * n_a:2 * n_a + 3]
    seg_refs = refs[2 * n_a + 3:2 * n_a + 3 + n_hm]
    o_ref = refs[2 * n_a + 3 + n_hm]
    hm_refs = refs[2 * n_a + 4 + n_hm:]
    for seg_ref, hm_ref in zip(seg_refs, hm_refs):
        dh = seg_ref.shape[1] // heads
        for hh in range(heads):
            hm_ref[:, hh, :] = seg_ref[:, hh * dh:(hh + 1) * dh]
    y = None
    for a_ref, w_ref in zip(a_refs, w_refs):
        t = jnp.dot(a_ref[...].astype(BF16), w_ref[...], preferred_element_type=F32)
        y = t if y is None else y + t
    r = y * lax.rsqrt(jnp.mean(y * y, axis=-1, keepdims=True) + EPS)
    o_ref[...] = x_ref[...] + gate_ref[...] * (r * gp_ref[...])


def _outproj(a_list, w_list, x2, gate, g_post, *, rows_per_group, tm, hm=None):
    r, d = x2.shape
    n_a = len(a_list)
    if gate.ndim == 3:
        per = rows_per_group // tm
        gate_spec = pl.BlockSpec((None, 1, d), lambda i: (i // per, 0, 0))
    else:
        gate_spec = pl.BlockSpec((tm, d), lambda i: (i, 0))
    in_specs = [pl.BlockSpec((tm, a.shape[1]), lambda i: (i, 0)) for a in a_list]
    in_specs += [pl.BlockSpec(w.shape, lambda i: (0, 0), pipeline_mode=pl.Buffered(1)) for w in w_list]
    in_specs += [pl.BlockSpec((tm, d), lambda i: (i, 0)), gate_spec, pl.BlockSpec((1, d), lambda i: (0, 0))]
    args = [*a_list, *w_list, x2, gate, g_post.reshape(1, d)]
    out_specs = [pl.BlockSpec((tm, d), lambda i: (i, 0))]
    out_shape = [jax.ShapeDtypeStruct((r, d), F32)]
    n_hm, heads = 0, 1
    if hm:
        z3, seg_ids, heads = hm
        n_hm = len(seg_ids)
        for sidx in seg_ids:
            in_specs.append(pl.BlockSpec((None, tm, SEG), lambda i, sidx=sidx: (sidx, i, 0)))
            args.append(z3)
            out_specs.append(pl.BlockSpec((tm, heads, SEG // heads), lambda i: (i, 0, 0)))
            out_shape.append(jax.ShapeDtypeStruct((r, heads, SEG // heads), F32))
    res = pl.pallas_call(
        functools.partial(_outproj_kernel, n_a=n_a, n_hm=n_hm, heads=heads),
        grid=(r // tm,),
        in_specs=in_specs,
        out_specs=out_specs,
        out_shape=out_shape,
        compiler_params=_cparams(("parallel",)),
        name="outproj",
    )(*args)
    return (res[0], list(res[1:])) if hm else res[0]


MOBA_AUX = 128
ALIBI_COL = 8
MOBA_HEADS_PER_STEP = 2


def _moba_kernel(slopes_ref, q_ref, k_ref, v_ref, ga_ref, o_ref, kaug_scr, vt_scr, *, nb, hps):
    blk = MOBA_BLOCK
    t = nb * blk
    log2e = math.log2(math.e)
    scale2 = DH_A ** -0.5 * log2e
    nt = (((1,), (1,)), ((), ()))
    pos = lax.broadcasted_iota(jnp.int32, (t, MOBA_AUX), 0)
    lane = lax.broadcasted_iota(jnp.int32, (t, MOBA_AUX), 1)
    posf = pos.astype(F32)
    pos_hi = posf.astype(BF16).astype(F32)
    pos_lo = posf - pos_hi
    aux_b = jnp.where(lane < ALIBI_COL, jnp.where(lane == pos // blk, 1.0, 0.0),
                      jnp.where(lane < ALIBI_COL + 3, pos_hi,
                                jnp.where(lane < ALIBI_COL + 6, pos_lo, 0.0))).astype(BF16)
    srow = lax.broadcasted_iota(jnp.int32, (8, blk), 0)
    rowb = lax.broadcasted_iota(jnp.int32, (nb, blk), 0)
    krow = lax.broadcasted_iota(jnp.int32, (blk, blk), 0)
    qcol = lax.broadcasted_iota(jnp.int32, (blk, blk), 1)

    def head(hh):
        hsl = slice(hh * DH_A, (hh + 1) * DH_A)
        k = k_ref[:, hsl]
        kmean_b = jnp.concatenate(
            [jnp.mean(k[j * blk:(j + 1) * blk, :], axis=0, keepdims=True) for j in range(nb)],
            axis=0).astype(BF16)
        sl2 = jnp.full((8, blk), slopes_ref[pl.program_id(1) * hps + hh] * log2e, F32)
        sl_hi = sl2.astype(BF16).astype(F32)
        sl_mid = (sl2 - sl_hi).astype(BF16).astype(F32)
        sl_lo = sl2 - sl_hi - sl_mid
        slope_rows = jnp.where(srow >= 6, 0.0,
                               jnp.where(srow % 3 == 0, sl_hi, jnp.where(srow % 3 == 1, sl_mid, sl_lo)))
        kaug_scr[hh] = jnp.concatenate([k.astype(BF16), aux_b], axis=1)
        vt_scr[hh] = jnp.transpose(v_ref[:, hsl]).astype(BF16)

        def scores(n):
            nk = (n + 1) * blk
            q = q_ref[n * blk:nk, hsl]
            if n > MOBA_TOPK:
                gate = lax.dot_general(kmean_b, q.astype(BF16), nt, preferred_element_type=F32)
                gm = jnp.where(rowb < n, gate, -jnp.inf)
                cnt = jnp.zeros((nb, blk), F32)
                for kk in range(n):
                    gk = gm[kk:kk + 1, :]
                    beats = (gk > gm) | ((gk == gm) & (rowb > kk))
                    cnt = cnt + jnp.where(beats, 1.0, 0.0)
                keep = ((rowb < n) & (cnt < MOBA_TOPK)) | (rowb == n)
                bias = jnp.where(keep, 0.0, NEG_BIG)
            else:
                bias = jnp.zeros((nb, blk), F32)
            pieces = [bias, slope_rows]
            if nb < ALIBI_COL:
                pieces.insert(1, jnp.zeros((ALIBI_COL - nb, blk), F32))
            pieces.append(jnp.zeros((MOBA_AUX - ALIBI_COL - 8, blk), F32))
            rhs = jnp.concatenate([(jnp.transpose(q) * scale2).astype(BF16),
                                   jnp.concatenate(pieces, axis=0).astype(BF16)], axis=0)
            return jnp.dot(kaug_scr[hh, 0:nk, :], rhs, preferred_element_type=F32)

        def finish(n, s):
            nk = (n + 1) * blk
            s_own = jnp.where(krow <= qcol, s[n * blk:nk, :], -jnp.inf)
            m = jnp.max(s_own, axis=0, keepdims=True)
            if n > 0:
                m = jnp.maximum(m, jnp.max(s[0:n * blk, :], axis=0, keepdims=True))
            p_own = jnp.exp2(s_own - m)
            l = jnp.sum(p_own, axis=0, keepdims=True)
            ot = jnp.dot(vt_scr[hh, :, n * blk:nk], p_own.astype(BF16), preferred_element_type=F32)
            if n > 0:
                p = jnp.exp2(s[0:n * blk, :] - m)
                l = l + jnp.sum(p, axis=0, keepdims=True)
                ot = ot + jnp.dot(vt_scr[hh, :, 0:n * blk], p.astype(BF16), preferred_element_type=F32)
            o = jnp.transpose(ot / l)
            o_ref[n * blk:nk, hsl] = (o * _silu(ga_ref[n * blk:nk, hsl])).astype(o_ref.dtype)

        s_cur = scores(0)
        for n in range(nb):
            s_nxt = scores(n + 1) if n + 1 < nb else None
            finish(n, s_cur)
            s_cur = s_nxt

    for hh in range(hps):
        head(hh)


def _moba_prompt(z4, slopes):
    _, b, t, w = z4.shape
    nb = t // MOBA_BLOCK
    assert nb <= ALIBI_COL
    hps = MOBA_HEADS_PER_STEP

    def seg_spec(sidx):
        return pl.BlockSpec((None, None, t, hps * DH_A), lambda bi, hg, s: (sidx, bi, 0, hg))

    grid_spec = pltpu.PrefetchScalarGridSpec(
        num_scalar_prefetch=1,
        grid=(b, H_A // hps),
        in_specs=[seg_spec(0), seg_spec(1), seg_spec(2), seg_spec(3)],
        out_specs=pl.BlockSpec((None, t, hps * DH_A), lambda bi, hg, s: (bi, 0, hg)),
        scratch_shapes=[
            pltpu.VMEM((hps, t, DH_A + MOBA_AUX), BF16),
            pltpu.VMEM((hps, DH_A, t), BF16),
        ],
    )
    return pl.pallas_call(
        functools.partial(_moba_kernel, nb=nb, hps=hps),
        grid_spec=grid_spec,
        out_shape=jax.ShapeDtypeStruct((b, t, w), BF16),
        compiler_params=_cparams(("parallel", "parallel")),
        name="moba_prompt",
    )(slopes, z4, z4, z4, z4)


def _pool_kernel(u_ref, gb_ref, prev_ref, pw_ref, ps_ref, o_ref, pn_ref, ext_scr, *, tt, pos0, nt):
    t = pl.program_id(1)
    hist = POOL_HIST + 1

    @pl.when(t == 0)
    def _():
        ext_scr[0:1, :] = jnp.zeros((1, ext_scr.shape[1]), F32)
        ext_scr[1:hist, :] = prev_ref[...]

    ext_scr[hist:hist + tt, :] = u_ref[...]
    pos = pos0 + t * tt + lax.broadcasted_iota(jnp.int32, (tt, 1), 0)
    for g, w in enumerate(POOL_WINDOWS):
        sl = slice(g * GW_B, (g + 1) * GW_B)
        x = u_ref[:, sl]
        acc = x
        for jj in range(1, w):
            acc = acc + ext_scr[hist - jj:hist - jj + tt, sl]
        cnt = jnp.minimum(w, pos + 1).astype(F32)
        d = acc / cnt - x
        y = jnp.dot(d.astype(BF16), pw_ref[g], preferred_element_type=F32) * ps_ref[:, sl]
        o_ref[:, sl] = (y * _silu(gb_ref[:, sl])).astype(o_ref.dtype)

    tail = ext_scr[tt:tt + hist, :]
    ext_scr[0:hist, :] = tail

    @pl.when(t == nt - 1)
    def _():
        pn_ref[...] = ext_scr[1:hist, :]


def _pool(z4, prev, pool_w_bf, pool_scale, *, pos0, tt, out_dtype):
    _, b, t, w = z4.shape
    nt = t // tt
    return pl.pallas_call(
        functools.partial(_pool_kernel, tt=tt, pos0=pos0, nt=nt),
        grid=(b, nt),
        in_specs=[
            pl.BlockSpec((None, None, tt, w), lambda bi, ti: (4, bi, ti, 0)),
            pl.BlockSpec((None, None, tt, w), lambda bi, ti: (5, bi, ti, 0)),
            pl.BlockSpec((None, POOL_HIST, w), lambda bi, ti: (bi, 0, 0)),
            pl.BlockSpec(pool_w_bf.shape, lambda bi, ti: (0, 0, 0)),
            pl.BlockSpec((1, w), lambda bi, ti: (0, 0)),
        ],
        out_specs=[
            pl.BlockSpec((None, tt, w), lambda bi, ti: (bi, ti, 0)),
            pl.BlockSpec((None, POOL_HIST, w), lambda bi, ti: (bi, 0, 0)),
        ],
        out_shape=[
            jax.ShapeDtypeStruct((b, t, w), out_dtype),
            jax.ShapeDtypeStruct((b, POOL_HIST, w), F32),
        ],
        scratch_shapes=[pltpu.VMEM((POOL_HIST + 1 + tt, w), F32)],
        compiler_params=_cparams(("parallel", "arbitrary")),
        name="pool",
    )(z4, z4, prev, pool_w_bf, pool_scale.reshape(1, w))


def _mlstm_kernel(*refs, lv, lp, zero_state):
    nseg = 2 * 5
    if zero_state:
        bias_ref = refs[0]
        seg_refs = refs[1:1 + nseg]
        gt_ref, hn_ref = refs[1 + nseg:3 + nseg]
        rest = refs[3 + nseg:]
    else:
        bias_ref, m0_ref = refs[:2]
        seg_refs = refs[2:2 + nseg]
        gt_ref, hn_ref, c0_ref, n0_ref = refs[2 + nseg:6 + nseg]
        rest = refs[6 + nseg:]
    h_out, c_out, n_out, m_out, c_scr, n_scr, m_scr, pad_scr = rest
    bi = pl.program_id(0)
    c = pl.program_id(1)
    nc = pl.num_programs(1)
    hpseg = SEG // DH_C

    @pl.when(c == 0)
    def _():
        if zero_state:
            c_scr[...] = jnp.zeros(c_scr.shape, F32)
            n_scr[...] = jnp.zeros(n_scr.shape, F32)
            m_scr[...] = jnp.zeros(m_scr.shape, F32)
        else:
            c_scr[...] = c0_ref[...]
            n_scr[...] = n0_ref[...]
            for h in range(H_C):
                m_scr[h] = jnp.full(m_scr.shape[1:], m0_ref[bi, h], F32)

    def load(ref, slot):
        if lv == lp:
            return ref
        pad_scr[slot] = jnp.zeros(pad_scr.shape[1:], F32)
        pad_scr[slot, 0:lv, 0:ref.shape[1]] = ref[...]
        return pad_scr.at[slot]

    srcs = [load(r, i) for i, r in enumerate(seg_refs)]
    gt_src = load(gt_ref, nseg)

    lane = lax.broadcasted_iota(jnp.int32, (1, 128), 1)
    bias_row = jnp.zeros((1, 128), F32)
    for h in range(H_C):
        bias_row = jnp.where(lane == h, bias_ref[0, h], bias_row)
        bias_row = jnp.where(lane == H_C + h, bias_ref[1, h], bias_row)
    gpre = gt_src[:, 0:128] + bias_row
    ig_all = gpre
    lf_all = _log_sigmoid(gpre)
    if lv != lp:
        row1 = lax.broadcasted_iota(jnp.int32, (lp, 1), 0)
        ig_all = jnp.where(row1 < lv, ig_all, NEG_BIG)
        lf_all = jnp.where(row1 < lv, lf_all, 0.0)
    rr = lax.broadcasted_iota(jnp.int32, (lp, lp), 0)
    cc = lax.broadcasted_iota(jnp.int32, (lp, lp), 1)
    tril = rr >= cc
    tril_b = jnp.where(tril, 1.0, 0.0).astype(BF16)
    lf_hi = lf_all.astype(BF16)
    r1 = lf_all - lf_hi.astype(F32)
    lf_mid = r1.astype(BF16)
    lf_lo = (r1 - lf_mid.astype(F32)).astype(BF16)
    b_all = (jnp.dot(tril_b, lf_hi, preferred_element_type=F32)
             + jnp.dot(tril_b, lf_mid, preferred_element_type=F32)
             + jnp.dot(tril_b, lf_lo, preferred_element_type=F32))
    lane_f = lax.broadcasted_iota(jnp.int32, (lp, 128), 1)
    rows_t = jnp.transpose(jnp.where(lane_f < H_C, ig_all, b_all))
    nt = (((1,), (1,)), ((), ()))

    def front(h):
        sl = slice((h % hpseg) * DH_C, (h % hpseg + 1) * DH_C)
        q = srcs[0 + h // hpseg][:, sl]
        k = srcs[2 + h // hpseg][:, sl] * (DH_C ** -0.5)
        v = srcs[4 + h // hpseg][:, sl]
        ig_col = ig_all[:, h:h + 1]
        b_col = b_all[:, H_C + h:H_C + h + 1]
        ig_row = rows_t[h:h + 1, :]
        b_row = rows_t[H_C + h:H_C + h + 1, :]
        dmat = jnp.where(tril, b_col - b_row + ig_row, -jnp.inf)
        m_prev = m_scr[h]
        carry = b_col + m_prev
        mt = jnp.maximum(carry, jnp.max(dmat, axis=1, keepdims=True))
        qb = q.astype(BF16)
        kb = k.astype(BF16)
        qk = lax.dot_general(qb, kb, nt, preferred_element_type=F32)
        cq = lax.dot_general(qb, c_scr[h].astype(BF16), nt, preferred_element_type=F32)
        return dict(sl=sl, q=q, k=k, v=v, kb=kb, ig_col=ig_col, b_col=b_col, m_prev=m_prev, carry=carry,
                    mt=mt, qk=qk, cq=cq, decay_w=jnp.exp(dmat - mt))

    def back(h, f):
        sl, q, k, v, kb, mt, carry = f["sl"], f["q"], f["k"], f["v"], f["kb"], f["mt"], f["carry"]
        s = f["qk"] * f["decay_w"]
        inter = jnp.exp(carry - mt)
        nq = jnp.sum(q * n_scr[h], axis=1, keepdims=True)
        num = jnp.dot(s.astype(BF16), v.astype(BF16), preferred_element_type=F32) + inter * f["cq"]

        m_new = mt[lp - 1:lp, :]
        b_last = f["b_col"][lp - 1:lp, :]
        w_col = jnp.exp(b_last - f["b_col"] + f["ig_col"] - m_new)
        decay = jnp.exp(b_last + f["m_prev"] - m_new)
        vw = (v * w_col).astype(BF16)
        c_scr[h] = decay * c_scr[h] + lax.dot_general(
            vw, kb, (((0,), (0,)), ((), ())), preferred_element_type=F32)
        n_scr[h] = decay * n_scr[h] + jnp.sum(k * w_col, axis=0, keepdims=True)
        m_scr[h] = m_new

        den = jnp.sum(s, axis=1, keepdims=True) + inter * nq
        hc = num / jnp.maximum(jnp.abs(den), jnp.exp(-mt))

        hsl = slice(h * DH_C, (h + 1) * DH_C)
        hc = hc * _sigmoid(srcs[6 + h // hpseg][:, sl])
        hc = hc * lax.rsqrt(jnp.mean(hc * hc, axis=-1, keepdims=True) + EPS) * hn_ref[:, hsl]
        res = (hc * _silu(srcs[8 + h // hpseg][:, sl])).astype(h_out.dtype)
        h_out[:, hsl] = res if lv == lp else res[0:lv, :]

    f_cur = front(0)
    for h in range(H_C):
        f_nxt = front(h + 1) if h + 1 < H_C else None
        back(h, f_cur)
        f_cur = f_nxt

    @pl.when(c == nc - 1)
    def _():
        c_out[...] = c_scr[...]
        n_out[...] = n_scr[...]
        m_out[...] = jnp.broadcast_to(m_scr[...], m_out.shape)


def _mlstm(z4, gates3, bias, head_norm, state, *, chunk, lp):
    nseg, b, t, _ = z4.shape
    nc = t // chunk
    w_c = H_C * DH_C
    zero_state = state is None
    n_pref = 1 if zero_state else 2

    def seg_spec(sidx):
        return pl.BlockSpec((None, None, chunk, SEG), lambda bi, c, *_: (sidx, bi, c, 0))

    def state_spec(*tail):
        return pl.BlockSpec((None, H_C) + tail, lambda bi, c, *_: (bi,) + (0,) * (len(tail) + 1))

    in_specs = [seg_spec(i) for i in range(nseg)]
    in_specs += [pl.BlockSpec((None, chunk, 128), lambda bi, c, *_: (bi, c, 0)),
                 pl.BlockSpec((1, w_c), lambda bi, c, *_: (0, 0))]
    args = [z4] * nseg + [gates3, head_norm.reshape(1, w_c)]
    prefetch = [bias]
    if not zero_state:
        c0, n0, m0 = state
        prefetch.append(m0)
        in_specs += [state_spec(DH_C, DH_C), state_spec(1, DH_C)]
        args += [c0, n0]
    grid_spec = pltpu.PrefetchScalarGridSpec(
        num_scalar_prefetch=n_pref,
        grid=(b, nc),
        in_specs=in_specs,
        out_specs=[
            pl.BlockSpec((None, chunk, w_c), lambda bi, c, *_: (bi, c, 0)),
            state_spec(DH_C, DH_C),
            state_spec(1, DH_C),
            state_spec(1, 128),
        ],
        scratch_shapes=[
            pltpu.VMEM((H_C, DH_C, DH_C), F32),
            pltpu.VMEM((H_C, 1, DH_C), F32),
            pltpu.VMEM((H_C, 1, 1), F32),
            pltpu.VMEM((nseg + 1, lp, SEG), F32),
        ],
    )
    return pl.pallas_call(
        functools.partial(_mlstm_kernel, lv=chunk, lp=lp, zero_state=zero_state),
        grid_spec=grid_spec,
        out_shape=[
            jax.ShapeDtypeStruct((b, t, w_c), BF16),
            jax.ShapeDtypeStruct((b, H_C, DH_C, DH_C), F32),
            jax.ShapeDtypeStruct((b, H_C, 1, DH_C), F32),
            jax.ShapeDtypeStruct((b, H_C, 1, 128), F32),
        ],
        compiler_params=_cparams(("parallel", "arbitrary")),
        name="mlstm",
    )(*prefetch, *args)


PAGES_PER_STEP = 16


def _sample_attn_kernel(pt_ref, slopes_ref, *refs, n_steps, tq, past_len):
    pp = PAGES_PER_STEP
    k_refs = refs[:pp]
    v_refs = refs[pp:2 * pp]
    qkvg_refs = refs[2 * pp:2 * pp + 4]
    o_ref, s_scr, pmax_scr, kmean_scr, acc_scr, l_scr, m_scr, sel_scr = refs[2 * pp + 4:]
    q_ref, kn_ref, vn_ref, ga_ref = qkvg_refs
    s = pl.program_id(1)
    rows = tq * H_A
    cols = PAGE_SIZE * H_A
    ppb = MOBA_BLOCK // PAGE_SIZE
    n_pages = n_steps * pp
    nt = (((1,), (1,)), ((), ()))
    log2e = math.log2(math.e)
    scale2 = DH_A ** -0.5 * log2e

    row = lax.broadcasted_iota(jnp.int32, (rows, 1), 0)
    row_h = row % H_A
    qpos = past_len + row // H_A
    slope2 = jnp.zeros((rows, 1), F32)
    for hh in range(H_A):
        slope2 = jnp.where(row_h == hh, slopes_ref[hh] * log2e, slope2)
    lane = lax.broadcasted_iota(jnp.int32, (1, cols), 1)

    def chunk(g):
        return pl.ds(pl.multiple_of(g * cols, cols), cols)

    def fold_lanes(x, op):
        acc = x[:, 0:128]
        for kk in range(1, cols // 128):
            acc = op(acc, x[:, kk * 128:(kk + 1) * 128])
        return acc

    def picked(sel, g):
        blk = g // ppb
        return (sel[0] == blk) | (sel[1] == blk) | (sel[2] == blk)

    @pl.when(s < n_steps)
    def _():
        qb = q_ref[...].reshape(rows, DH_A).astype(BF16)
        for jj in range(pp // ppb):
            tot = None
            for p in range(ppb):
                part = jnp.sum(k_refs[jj * ppb + p][...], axis=0)
                tot = part if tot is None else tot + part
            kmean_scr[pl.ds(pl.multiple_of((s * (pp // ppb) + jj) * H_A, H_A), H_A), :] = tot * (1.0 / MOBA_BLOCK)
        head_ok = (lane % H_A) == row_h
        qoff = slope2 * qpos.astype(F32)
        for p in range(pp):
            g = s * pp + p
            k2 = k_refs[p][...].reshape(cols, DH_A).astype(BF16)
            st = lax.dot_general(qb, k2, nt, preferred_element_type=F32)
            kpos = (g * PAGE_SIZE + lane // H_A).astype(F32)
            sv = st * scale2 + (slope2 * kpos - qoff)
            sv = jnp.where(head_ok, sv, -jnp.inf)
            s_scr[:, chunk(g)] = sv
            pmax_scr[g] = fold_lanes(sv, jnp.maximum)

    @pl.when(s == n_steps)
    def _():
        qb = q_ref[...].reshape(rows, DH_A).astype(BF16)
        nl = kmean_scr.shape[0]
        gate = lax.dot_general(qb, kmean_scr[...].astype(BF16), nt, preferred_element_type=F32)
        glane = lax.broadcasted_iota(jnp.int32, (rows, nl), 1)
        gate = jnp.where(glane % H_A == row_h, gate, -jnp.inf)
        sel = []
        sel_lane = lax.broadcasted_iota(jnp.int32, sel_scr.shape, 1)
        sel_tile = jnp.zeros(sel_scr.shape, jnp.int32)
        for t in range(MOBA_TOPK):
            mx = jnp.max(gate, axis=1, keepdims=True)
            idx = jnp.min(jnp.where(gate == mx, glane, nl), axis=1, keepdims=True)
            sel.append(idx // H_A)
            sel_tile = jnp.where(sel_lane == t, idx // H_A, sel_tile)
            gate = jnp.where(glane == idx, -jnp.inf, gate)
        sel_scr[...] = sel_tile

        kn2 = kn_ref[...].reshape(rows, DH_A).astype(BF16)
        olane = lax.broadcasted_iota(jnp.int32, (1, rows), 1)
        own_pos = past_len + olane // H_A
        s_own = lax.dot_general(qb, kn2, nt, preferred_element_type=F32) * scale2
        s_own = s_own - slope2 * (qpos - own_pos).astype(F32)
        s_own = jnp.where(((olane % H_A) == row_h) & (own_pos <= qpos), s_own, -jnp.inf)

        def max_body(g, macc):
            return jnp.maximum(macc, jnp.where(picked(sel, g), pmax_scr[g], -jnp.inf))

        macc = lax.fori_loop(0, n_pages, max_body, jnp.full((rows, 128), -jnp.inf, F32), unroll=4)
        m = jnp.maximum(jnp.max(macc, axis=1, keepdims=True), jnp.max(s_own, axis=1, keepdims=True))
        m_scr[...] = m
        p_own = jnp.exp2(s_own - m)
        l_scr[...] = jnp.where(lax.broadcasted_iota(jnp.int32, l_scr.shape, 1) == 0,
                               jnp.sum(p_own, axis=1, keepdims=True), 0.0)
        vn2 = vn_ref[...].reshape(rows, DH_A).astype(BF16)
        acc_scr[...] = jnp.dot(p_own.astype(BF16), vn2, preferred_element_type=F32)

    @pl.when(s >= n_steps)
    def _():
        sel = [sel_scr[:, t:t + 1] for t in range(MOBA_TOPK)]
        m = m_scr[...]
        acc = acc_scr[...]
        lacc = l_scr[...]
        for p in range(pp):
            g = (s - n_steps) * pp + p
            pr = jnp.exp2(s_scr[:, chunk(g)] - jnp.where(picked(sel, g), m, jnp.inf))
            lacc = lacc + fold_lanes(pr, jnp.add)
            v2 = v_refs[p][...].reshape(cols, DH_A).astype(BF16)
            acc = acc + jnp.dot(pr.astype(BF16), v2, preferred_element_type=F32)
        acc_scr[...] = acc
        l_scr[...] = lacc

    @pl.when(s == 2 * n_steps - 1)
    def _():
        o = acc_scr[...] / jnp.sum(l_scr[...], axis=1, keepdims=True)
        o = o * _silu(ga_ref[...].reshape(rows, DH_A))
        o_ref[...] = o.reshape(tq, H_A, DH_A)


def _sample_attn(cache_k4, cache_v4, page_table, slopes, qkvg):
    b, n_pages = page_table.shape
    _, tq, _, _ = qkvg[0].shape
    pp = PAGES_PER_STEP
    n_steps = n_pages // pp
    past_len = n_pages * PAGE_SIZE
    nbp = past_len // MOBA_BLOCK
    rows = tq * H_A

    def k_spec(p):
        return pl.BlockSpec((None, PAGE_SIZE, H_A, DH_A),
                            lambda bi, s, pt, sl: (pt[bi, jnp.minimum(s, n_steps - 1) * pp + p], 0, 0, 0))

    def v_spec(p):
        return pl.BlockSpec((None, PAGE_SIZE, H_A, DH_A),
                            lambda bi, s, pt, sl: (pt[bi, jnp.maximum(s - n_steps, 0) * pp + p], 0, 0, 0))

    new_spec = pl.BlockSpec((None, tq, H_A, DH_A), lambda bi, s, pt, sl: (bi, 0, 0, 0))

    grid_spec = pltpu.PrefetchScalarGridSpec(
        num_scalar_prefetch=2,
        grid=(b, 2 * n_steps),
        in_specs=[k_spec(p) for p in range(pp)] + [v_spec(p) for p in range(pp)] + [new_spec] * 4,
        out_specs=pl.BlockSpec((None, tq, H_A, DH_A), lambda bi, s, pt, sl: (bi, 0, 0, 0)),
        scratch_shapes=[
            pltpu.VMEM((rows, n_pages * PAGE_SIZE * H_A), F32),
            pltpu.VMEM((n_pages, rows, 128), F32),
            pltpu.VMEM((nbp * H_A, DH_A), F32),
            pltpu.VMEM((rows, DH_A), F32),
            pltpu.VMEM((rows, 128), F32),
            pltpu.VMEM((rows, 1), F32),
            pltpu.VMEM((rows, 128), jnp.int32),
        ],
    )
    return pl.pallas_call(
        functools.partial(_sample_attn_kernel, n_steps=n_steps, tq=tq, past_len=past_len),
        grid_spec=grid_spec,
        out_shape=jax.ShapeDtypeStruct((b, tq, H_A, DH_A), F32),
        compiler_params=_cparams(("parallel", "arbitrary")),
        name="sample_attn",
    )(page_table, slopes, *([cache_k4] * pp), *([cache_v4] * pp), *qkvg)


def _alibi_slopes(n_heads):
    return 2.0 ** (-8.0 * jnp.arange(1, n_heads + 1, dtype=F32) / n_heads)


def kernel(x_prompt, x_sample, cache_k, cache_v, state_pool, state_C, state_n, state_m, page_table,
           c_prompt, c_sample, ada_w, ada_b, norm_pre, norm_post, w_in_even, pool_w, pool_scale,
           w_out_even, w_in_odd, b_igate, b_fgate, head_norm, w_out_odd):
    bp, tp, d = x_prompt.shape
    bs, ts, _ = x_sample.shape
    n_pages = page_table.shape[1]
    past_len = n_pages * PAGE_SIZE
    w_a = H_A * DH_A
    w_c = H_C * DH_C
    rp, rs = bp * tp, bs * ts
    tm_p = 1024

    mods = _adaln(jnp.concatenate([c_prompt, c_sample], axis=0), ada_w, ada_b)

    def mod_parts(l):
        m = mods[l]
        shift, scale, gate = m[:, :d], m[:, d:2 * d], m[:, 2 * d:]
        prompt = tuple(a[:bp].reshape(bp, 1, d) for a in (shift, scale, gate))
        sample = tuple(jnp.repeat(a[bp:], ts, axis=0) for a in (shift, scale, gate))
        return prompt, sample

    slopes = _alibi_slopes(H_A)
    xp = x_prompt.reshape(rp, d)
    xs = x_sample.reshape(rs, d)

    (sh_p, sc_p, gt_p), (sh_s, sc_s, gt_s) = mod_parts(0)
    w_in0 = w_in_even[0].astype(BF16)
    zp, _ = _inproj(xp, norm_pre[0], sc_p, sh_p, w_in0, None, rows_per_group=tp, tm=tm_p)
    zs, qkvg_s = _inproj(xs, norm_pre[0], sc_s, sh_s, w_in0, None, rows_per_group=ts, tm=rs, hm=(0, 4, H_A))
    qkvg_s = [a.reshape(bs, ts, H_A, DH_A) for a in qkvg_s]
    kv_s = qkvg_s[1:3]
    z4p = zp.reshape(zp.shape[0], bp, tp, SEG)
    z4s = zs.reshape(zs.shape[0], bs, ts, SEG)

    pool_w_bf = pool_w[0].astype(BF16)
    att_p = _moba_prompt(z4p, slopes)
    pool_p, pstate_p = _pool(z4p, jnp.zeros((bp, POOL_HIST, SEG), F32), pool_w_bf, pool_scale[0],
                             pos0=0, tt=512, out_dtype=BF16)

    att_s = _sample_attn(cache_k[0], cache_v[0], page_table, slopes, qkvg_s)
    pool_s, pstate_s = _pool(z4s, state_pool[0], pool_w_bf, pool_scale[0],
                             pos0=past_len, tt=ts, out_dtype=F32)

    w_out0 = w_out_even[0].astype(BF16)
    w_list0 = [w_out0[:w_a], w_out0[w_a:]]
    xp1, kv_p = _outproj([att_p.reshape(rp, w_a), pool_p.reshape(rp, SEG)], w_list0, xp, gt_p, norm_post[0],
                         rows_per_group=tp, tm=512, hm=(zp, (1, 2), H_A))
    xs1 = _outproj([att_s.reshape(rs, w_a), pool_s.reshape(rs, SEG)], w_list0, xs, gt_s, norm_post[0],
                   rows_per_group=ts, tm=rs)

    (sh_p, sc_p, gt_p), (sh_s, sc_s, gt_s) = mod_parts(1)
    n_main = 5 * w_c
    w_in1 = w_in_odd[0].astype(BF16)
    wg1 = jnp.pad(w_in_odd[0][:, n_main:], ((0, 0), (0, 128 - 2 * H_C))).astype(BF16)
    zp, gates_p = _inproj(xp1, norm_pre[1], sc_p, sh_p, w_in1, wg1, rows_per_group=tp, tm=tm_p)
    zs, gates_s = _inproj(xs1, norm_pre[1], sc_s, sh_s, w_in1, wg1, rows_per_group=ts, tm=rs)
    bias = jnp.stack([b_igate[0], b_fgate[0]]).astype(F32)
    hc_p, c_p, n_p, m_p = _mlstm(zp.reshape(zp.shape[0], bp, tp, SEG), gates_p.reshape(bp, tp, 128),
                                 bias, head_norm[0], None, chunk=256, lp=256)
    state = (state_C[0], state_n[0].reshape(bs, H_C, 1, DH_C), state_m[0])
    hc_s, c_s, n_s, m_s = _mlstm(zs.reshape(zs.shape[0], bs, ts, SEG), gates_s.reshape(bs, ts, 128),
                                 bias, head_norm[0], state, chunk=ts, lp=128)
    w_out1 = w_out_odd[0].astype(BF16)
    xp2 = _outproj([hc_p.reshape(rp, w_c)], [w_out1], xp1, gt_p, norm_post[1], rows_per_group=tp, tm=512)
    xs2 = _outproj([hc_s.reshape(rs, w_c)], [w_out1], xs1, gt_s, norm_post[1], rows_per_group=ts, tm=rs)

    kv_shape_p = (1, bp, tp, H_A, DH_A)
    kv_shape_s = (1, bs, ts, H_A, DH_A)
    return (xp2.reshape(bp, tp, d), xs2.reshape(bs, ts, d),
            kv_p[0].reshape(kv_shape_p), kv_p[1].reshape(kv_shape_p),
            kv_s[0].reshape(kv_shape_s), kv_s[1].reshape(kv_shape_s),
            pstate_p[None], pstate_s[None],
            c_p[None], n_p.reshape(1, bp, H_C, DH_C), m_p[:, :, 0, 0][None],
            c_s[None], n_s.reshape(1, bs, H_C, DH_C), m_s[:, :, 0, 0][None])
```

```python
import functools
import math

import jax
import jax.numpy as jnp
from jax import lax
from jax.experimental import pallas as pl
from jax.experimental.pallas import tpu as pltpu

F32 = jnp.float32
BF16 = jnp.bfloat16

H_A = 8
DH_A = 128
MOBA_BLOCK = 256
MOBA_TOPK = 3
POOL_WINDOWS = (2, 4, 8, 16)
GW_B = 256
POOL_HIST = max(POOL_WINDOWS) - 1
H_C = 8
DH_C = 256
EPS = 1e-6
PAGE_SIZE = 128

SEG = 1024
NEG_BIG = -1e30

VMEM_LIMIT_V7X = 52 * 1024 * 1024


def _sigmoid(x):
    return 0.5 * jnp.tanh(0.5 * x) + 0.5


def _silu(x):
    return x * _sigmoid(x)


def _log_sigmoid(x):
    return -(jnp.maximum(-x, 0.0) + jnp.log(1.0 + jnp.exp(-jnp.abs(x))))


def _cparams(sem):
    return pltpu.CompilerParams(dimension_semantics=sem, vmem_limit_bytes=VMEM_LIMIT_V7X)


def _adaln_kernel(c_ref, w_ref, b_ref, o_ref):
    s = _silu(c_ref[...]).astype(BF16)
    o_ref[...] = jnp.dot(s, w_ref[...].astype(BF16), preferred_element_type=F32) + b_ref[...]


def _adaln(c_all, ada_w, ada_b, tn=512):
    depth, d, n = ada_w.shape
    r = c_all.shape[0]
    return pl.pallas_call(
        _adaln_kernel,
        grid=(depth, n // tn),
        in_specs=[
            pl.BlockSpec((r, d), lambda l, j: (0, 0)),
            pl.BlockSpec((None, d, tn), lambda l, j: (l, 0, j)),
            pl.BlockSpec((None, 1, tn), lambda l, j: (l, 0, j)),
        ],
        out_specs=pl.BlockSpec((None, r, tn), lambda l, j: (l, 0, j)),
        out_shape=jax.ShapeDtypeStruct((depth, r, n), F32),
        compiler_params=_cparams(("parallel", "parallel")),
        name="adaln",
    )(c_all, ada_w, ada_b.reshape(depth, 1, n))


def _inproj_kernel(*refs, has_gate, hm, nch):
    hm_refs = ()
    gt_ref = wg_ref = None
    if has_gate:
        x_ref, g_ref, sc_ref, sh_ref, w_ref, wg_ref, z_ref, gt_ref, h_even, h_odd = refs
    elif hm:
        x_ref, g_ref, sc_ref, sh_ref, w_ref, z_ref = refs[:6]
        hm_refs = refs[6:-2]
        h_even, h_odd = refs[-2:]
    else:
        x_ref, g_ref, sc_ref, sh_ref, w_ref, z_ref, h_even, h_odd = refs
    i = pl.program_id(0)
    j = pl.program_id(1)
    tm = x_ref.shape[0]
    rc = tm // nch

    def normed(rows):
        x = x_ref[rows, :]
        sc = sc_ref[...] if sc_ref.shape[0] == 1 else sc_ref[rows, :]
        sh = sh_ref[...] if sh_ref.shape[0] == 1 else sh_ref[rows, :]
        r = x * lax.rsqrt(jnp.mean(x * x, axis=-1, keepdims=True) + EPS)
        return ((r * g_ref[...]) * (1.0 + sc) + sh).astype(BF16)

    @pl.when((i == 0) & (j == 0))
    def _():
        h_even[...] = normed(slice(None))

    def step(h_cur, h_nxt):
        if has_gate:
            @pl.when(j == 0)
            def _():
                gt_ref[...] = jnp.dot(h_cur[...], wg_ref[...], preferred_element_type=F32)

        c = jnp.clip(j - 1, 0, nch - 1)
        rows = pl.ds(pl.multiple_of(c * rc, rc), rc)
        h_nxt[rows, :] = normed(rows)
        res = jnp.dot(h_cur[...], w_ref[...], preferred_element_type=F32)
        z_ref[...] = res

        if hm:
            first, count, heads = hm
            dh = SEG // heads
            for idx, hm_ref in enumerate(hm_refs):
                @pl.when(j == first + idx)
                def _(hm_ref=hm_ref):
                    for hh in range(heads):
                        hm_ref[:, hh, :] = res[:, hh * dh:(hh + 1) * dh]

    pl.when(i % 2 == 0)(functools.partial(step, h_even, h_odd))
    pl.when(i % 2 == 1)(functools.partial(step, h_odd, h_even))


def _inproj(x2, g_pre, scale, shift, w_bf, wg_bf, *, rows_per_group, tm, hm=None):
    r, d = x2.shape
    n = w_bf.shape[1]
    nseg = n // SEG
    n_tiles = r // tm
    has_gate = wg_bf is not None
    nch = 1 << ((nseg - 1).bit_length() - 1)

    def ahead(i, j):
        return jnp.where((i == 0) & (j == 0), 0, jnp.minimum(i + 1, n_tiles - 1))

    if scale.ndim == 3:
        per = rows_per_group // tm
        mod_spec = pl.BlockSpec((None, 1, d), lambda i, j: (ahead(i, j) // per, 0, 0))
    else:
        mod_spec = pl.BlockSpec((tm, d), lambda i, j: (ahead(i, j), 0))
    x_mode = {"pipeline_mode": pl.Buffered(1)} if hm else {}
    in_specs = [
        pl.BlockSpec((tm, d), lambda i, j: (ahead(i, j), 0), **x_mode),
        pl.BlockSpec((1, d), lambda i, j: (0, 0)),
        mod_spec,
        mod_spec,
        pl.BlockSpec((d, SEG), lambda i, j: (0, j)),
    ]
    args = [x2, g_pre.reshape(1, d), scale, shift, w_bf]
    out_specs = [pl.BlockSpec((None, tm, SEG), lambda i, j: (j, i, 0))]
    out_shape = [jax.ShapeDtypeStruct((nseg, r, SEG), F32)]
    if has_gate:
        in_specs.append(pl.BlockSpec((d, 128), lambda i, j: (0, 0)))
        args.append(wg_bf)
        out_specs.append(pl.BlockSpec((tm, 128), lambda i, j: (i, 0)))
        out_shape.append(jax.ShapeDtypeStruct((r, 128), F32))
    if hm:
        first, count, heads = hm
        for _ in range(count):
            out_specs.append(pl.BlockSpec((tm, heads, SEG // heads), lambda i, j: (i, 0, 0),
                                          pipeline_mode=pl.Buffered(1)))
            out_shape.append(jax.ShapeDtypeStruct((r, heads, SEG // heads), F32))
    res = pl.pallas_call(
        functools.partial(_inproj_kernel, has_gate=has_gate, hm=hm, nch=nch),
        grid=(n_tiles, nseg),
        in_specs=in_specs,
        out_specs=out_specs,
        out_shape=out_shape,
        scratch_shapes=[pltpu.VMEM((tm, d), BF16), pltpu.VMEM((tm, d), BF16)],
        compiler_params=_cparams(("arbitrary", "arbitrary")),
        name="inproj_gate" if has_gate else "inproj",
    )(*args)
    if hm:
        return res[0], list(res[1:])
    return (res[0], res[1]) if has_gate else (res[0], None)


def _outproj_kernel(*refs, n_a, n_hm, heads):
    a_refs = refs[:n_a]
    w_refs = refs[n_a:2 * n_a]
    x_ref, gate_ref, gp_ref = refs[2 * n_a:2 * n_a + 3]
    seg_refs = refs[2 * n_a + 3:2 * n_a + 3 + n_hm]
    o_ref = refs[2 * n_a + 3 + n_hm]
    hm_refs = refs[2 * n_a + 4 + n_hm:]
    for seg_ref, hm_ref in zip(seg_refs, hm_refs):
        dh = seg_ref.shape[1] // heads
        for hh in range(heads):
            hm_ref[pl.ds(hh, seg_ref.shape[0], stride=heads), :] = seg_ref[:, hh * dh:(hh + 1) * dh]
    y = None
    for a_ref, w_ref in zip(a_refs, w_refs):
        t = jnp.dot(a_ref[...].astype(BF16), w_ref[...], preferred_element_type=F32)
        y = t if y is None else y + t
    r = y * lax.rsqrt(jnp.mean(y * y, axis=-1, keepdims=True) + EPS)
    o_ref[...] = x_ref[...] + gate_ref[...] * (r * gp_ref[...])


def _outproj(a_list, w_list, x2, gate, g_post, *, rows_per_group, tm, hm=None):
    r, d = x2.shape
    n_a = len(a_list)
    if gate.ndim == 3:
        per = rows_per_group // tm
        gate_spec = pl.BlockSpec((None, 1, d), lambda i: (i // per, 0, 0))
    else:
        gate_spec = pl.BlockSpec((tm, d), lambda i: (i, 0))
    in_specs = [pl.BlockSpec((tm, a.shape[1]), lambda i: (i, 0)) for a in a_list]
    in_specs += [pl.BlockSpec(w.shape, lambda i: (0, 0), pipeline_mode=pl.Buffered(1)) for w in w_list]
    in_specs += [pl.BlockSpec((tm, d), lambda i: (i, 0)), gate_spec, pl.BlockSpec((1, d), lambda i: (0, 0))]
    args = [*a_list, *w_list, x2, gate, g_post.reshape(1, d)]
    out_specs = [pl.BlockSpec((tm, d), lambda i: (i, 0))]
    out_shape = [jax.ShapeDtypeStruct((r, d), F32)]
    n_hm, heads = 0, 1
    if hm:
        z3, seg_ids, heads = hm
        n_hm = len(seg_ids)
        for sidx in seg_ids:
            in_specs.append(pl.BlockSpec((None, tm, SEG), lambda i, sidx=sidx: (sidx, i, 0)))
            args.append(z3)
            out_specs.append(pl.BlockSpec((tm * heads, SEG // heads), lambda i: (i, 0)))
            out_shape.append(jax.ShapeDtypeStruct((r * heads, SEG // heads), F32))
    res = pl.pallas_call(
        functools.partial(_outproj_kernel, n_a=n_a, n_hm=n_hm, heads=heads),
        grid=(r // tm,),
        in_specs=in_specs,
        out_specs=out_specs,
        out_shape=out_shape,
        compiler_params=_cparams(("parallel",)),
        name="outproj",
    )(*args)
    return (res[0], list(res[1:])) if hm else res[0]


MOBA_AUX = 128
ALIBI_COL = 8
MOBA_HEADS_PER_STEP = 2


def _moba_kernel(slopes_ref, q_ref, k_ref, v_ref, ga_ref, o_ref, kaug_scr, vt_scr, *, nb, hps):
    blk = MOBA_BLOCK
    t = nb * blk
    log2e = math.log2(math.e)
    scale2 = DH_A ** -0.5 * log2e
    nt = (((1,), (1,)), ((), ()))
    pos = lax.broadcasted_iota(jnp.int32, (t, MOBA_AUX), 0)
    lane = lax.broadcasted_iota(jnp.int32, (t, MOBA_AUX), 1)
    posf = pos.astype(F32)
    pos_hi = posf.astype(BF16).astype(F32)
    pos_lo = posf - pos_hi
    aux_b = jnp.where(lane < ALIBI_COL, jnp.where(lane == pos // blk, 1.0, 0.0),
                      jnp.where(lane < ALIBI_COL + 3, pos_hi,
                                jnp.where(lane < ALIBI_COL + 6, pos_lo, 0.0))).astype(BF16)
    srow = lax.broadcasted_iota(jnp.int32, (8, blk), 0)
    rowb = lax.broadcasted_iota(jnp.int32, (nb, blk), 0)
    krow = lax.broadcasted_iota(jnp.int32, (blk, blk), 0)
    qcol = lax.broadcasted_iota(jnp.int32, (blk, blk), 1)

    def head(hh):
        hsl = slice(hh * DH_A, (hh + 1) * DH_A)
        k = k_ref[:, hsl]
        kmean_b = jnp.concatenate(
            [jnp.mean(k[j * blk:(j + 1) * blk, :], axis=0, keepdims=True) for j in range(nb)],
            axis=0).astype(BF16)
        sl2 = jnp.full((8, blk), slopes_ref[pl.program_id(1) * hps + hh] * log2e, F32)
        sl_hi = sl2.astype(BF16).astype(F32)
        sl_mid = (sl2 - sl_hi).astype(BF16).astype(F32)
        sl_lo = sl2 - sl_hi - sl_mid
        slope_rows = jnp.where(srow >= 6, 0.0,
                               jnp.where(srow % 3 == 0, sl_hi, jnp.where(srow % 3 == 1, sl_mid, sl_lo)))
        kaug_scr[hh] = jnp.concatenate([k.astype(BF16), aux_b], axis=1)
        vt_scr[hh] = jnp.transpose(v_ref[:, hsl]).astype(BF16)

        def scores(n):
            nk = (n + 1) * blk
            q = q_ref[n * blk:nk, hsl]
            if n > MOBA_TOPK:
                gate = lax.dot_general(kmean_b, q.astype(BF16), nt, preferred_element_type=F32)
                gm = jnp.where(rowb < n, gate, -jnp.inf)
                cnt = jnp.zeros((nb, blk), F32)
                for kk in range(n):
                    gk = gm[kk:kk + 1, :]
                    beats = (gk > gm) | ((gk == gm) & (rowb > kk))
                    cnt = cnt + jnp.where(beats, 1.0, 0.0)
                keep = ((rowb < n) & (cnt < MOBA_TOPK)) | (rowb == n)
                bias = jnp.where(keep, 0.0, NEG_BIG)
            else:
                bias = jnp.zeros((nb, blk), F32)
            pieces = [bias, slope_rows]
            if nb < ALIBI_COL:
                pieces.insert(1, jnp.zeros((ALIBI_COL - nb, blk), F32))
            pieces.append(jnp.zeros((MOBA_AUX - ALIBI_COL - 8, blk), F32))
            rhs = jnp.concatenate([(jnp.transpose(q) * scale2).astype(BF16),
                                   jnp.concatenate(pieces, axis=0).astype(BF16)], axis=0)
            return jnp.dot(kaug_scr[hh, 0:nk, :], rhs, preferred_element_type=F32)

        def finish(n, s):
            nk = (n + 1) * blk
            s_own = jnp.where(krow <= qcol, s[n * blk:nk, :], -jnp.inf)
            m = jnp.max(s_own, axis=0, keepdims=True)
            if n > 0:
                m = jnp.maximum(m, jnp.max(s[0:n * blk, :], axis=0, keepdims=True))
            p_own = jnp.exp2(s_own - m)
            l = jnp.sum(p_own, axis=0, keepdims=True)
            ot = jnp.dot(vt_scr[hh, :, n * blk:nk], p_own.astype(BF16), preferred_element_type=F32)
            if n > 0:
                p = jnp.exp2(s[0:n * blk, :] - m)
                l = l + jnp.sum(p, axis=0, keepdims=True)
                ot = ot + jnp.dot(vt_scr[hh, :, 0:n * blk], p.astype(BF16), preferred_element_type=F32)
            o = jnp.transpose(ot / l)
            o_ref[n * blk:nk, hsl] = (o * _silu(ga_ref[n * blk:nk, hsl])).astype(o_ref.dtype)

        s_cur = scores(0)
        for n in range(nb):
            s_nxt = scores(n + 1) if n + 1 < nb else None
            finish(n, s_cur)
            s_cur = s_nxt

    for hh in range(hps):
        head(hh)


def _moba_prompt(z4, slopes):
    _, b, t, w = z4.shape
    nb = t // MOBA_BLOCK
    assert nb <= ALIBI_COL
    hps = MOBA_HEADS_PER_STEP

    def seg_spec(sidx):
        return pl.BlockSpec((None, None, t, hps * DH_A), lambda bi, hg, s: (sidx, bi, 0, hg))

    grid_spec = pltpu.PrefetchScalarGridSpec(
        num_scalar_prefetch=1,
        grid=(b, H_A // hps),
        in_specs=[seg_spec(0), seg_spec(1), seg_spec(2), seg_spec(3)],
        out_specs=pl.BlockSpec((None, t, hps * DH_A), lambda bi, hg, s: (bi, 0, hg)),
        scratch_shapes=[
            pltpu.VMEM((hps, t, DH_A + MOBA_AUX), BF16),
            pltpu.VMEM((hps, DH_A, t), BF16),
        ],
    )
    return pl.pallas_call(
        functools.partial(_moba_kernel, nb=nb, hps=hps),
        grid_spec=grid_spec,
        out_shape=jax.ShapeDtypeStruct((b, t, w), BF16),
        compiler_params=_cparams(("parallel", "parallel")),
        name="moba_prompt",
    )(slopes, z4, z4, z4, z4)


def _pool_kernel(u_ref, gb_ref, prev_ref, pw_ref, ps_ref, o_ref, pn_ref, ext_scr, *, tt, pos0, nt):
    t = pl.program_id(1)
    hist = POOL_HIST + 1

    @pl.when(t == 0)
    def _():
        ext_scr[0:1, :] = jnp.zeros((1, ext_scr.shape[1]), F32)
        ext_scr[1:hist, :] = prev_ref[...]

    ext_scr[hist:hist + tt, :] = u_ref[...]
    pos = pos0 + t * tt + lax.broadcasted_iota(jnp.int32, (tt, 1), 0)
    for g, w in enumerate(POOL_WINDOWS):
        sl = slice(g * GW_B, (g + 1) * GW_B)
        x = u_ref[:, sl]
        acc = x
        for jj in range(1, w):
            acc = acc + ext_scr[hist - jj:hist - jj + tt, sl]
        cnt = jnp.minimum(w, pos + 1).astype(F32)
        d = acc / cnt - x
        y = jnp.dot(d.astype(BF16), pw_ref[g], preferred_element_type=F32) * ps_ref[:, sl]
        o_ref[:, sl] = (y * _silu(gb_ref[:, sl])).astype(o_ref.dtype)

    tail = ext_scr[tt:tt + hist, :]
    ext_scr[0:hist, :] = tail

    @pl.when(t == nt - 1)
    def _():
        pn_ref[...] = ext_scr[1:hist, :]


def _pool(z4, prev, pool_w_bf, pool_scale, *, pos0, tt, out_dtype):
    _, b, t, w = z4.shape
    nt = t // tt
    return pl.pallas_call(
        functools.partial(_pool_kernel, tt=tt, pos0=pos0, nt=nt),
        grid=(b, nt),
        in_specs=[
            pl.BlockSpec((None, None, tt, w), lambda bi, ti: (4, bi, ti, 0)),
            pl.BlockSpec((None, None, tt, w), lambda bi, ti: (5, bi, ti, 0)),
            pl.BlockSpec((None, POOL_HIST, w), lambda bi, ti: (bi, 0, 0)),
            pl.BlockSpec(pool_w_bf.shape, lambda bi, ti: (0, 0, 0)),
            pl.BlockSpec((1, w), lambda bi, ti: (0, 0)),
        ],
        out_specs=[
            pl.BlockSpec((None, tt, w), lambda bi, ti: (bi, ti, 0)),
            pl.BlockSpec((None, POOL_HIST, w), lambda bi, ti: (bi, 0, 0)),
        ],
        out_shape=[
            jax.ShapeDtypeStruct((b, t, w), out_dtype),
            jax.ShapeDtypeStruct((b, POOL_HIST, w), F32),
        ],
        scratch_shapes=[pltpu.VMEM((POOL_HIST + 1 + tt, w), F32)],
        compiler_params=_cparams(("parallel", "arbitrary")),
        name="pool",
    )(z4, z4, prev, pool_w_bf, pool_scale.reshape(1, w))


def _mlstm_kernel(*refs, lv, lp, zero_state):
    nseg = 2 * 5
    if zero_state:
        bias_ref = refs[0]
        seg_refs = refs[1:1 + nseg]
        gt_ref, hn_ref = refs[1 + nseg:3 + nseg]
        rest = refs[3 + nseg:]
    else:
        bias_ref, m0_ref = refs[:2]
        seg_refs = refs[2:2 + nseg]
        gt_ref, hn_ref, c0_ref, n0_ref = refs[2 + nseg:6 + nseg]
        rest = refs[6 + nseg:]
    h_out, c_out, n_out, m_out, c_scr, n_scr, m_scr, pad_scr = rest
    bi = pl.program_id(0)
    c = pl.program_id(1)
    nc = pl.num_programs(1)
    hpseg = SEG // DH_C

    @pl.when(c == 0)
    def _():
        if zero_state:
            c_scr[...] = jnp.zeros(c_scr.shape, F32)
            n_scr[...] = jnp.zeros(n_scr.shape, F32)
            m_scr[...] = jnp.zeros(m_scr.shape, F32)
        else:
            c_scr[...] = c0_ref[...]
            n_scr[...] = n0_ref[...]
            for h in range(H_C):
                m_scr[h] = jnp.full(m_scr.shape[1:], m0_ref[bi, h], F32)

    def load(ref, slot):
        if lv == lp:
            return ref
        pad_scr[slot] = jnp.zeros(pad_scr.shape[1:], F32)
        pad_scr[slot, 0:lv, 0:ref.shape[1]] = ref[...]
        return pad_scr.at[slot]

    srcs = [load(r, i) for i, r in enumerate(seg_refs)]
    gt_src = load(gt_ref, nseg)

    lane = lax.broadcasted_iota(jnp.int32, (1, 128), 1)
    bias_row = jnp.zeros((1, 128), F32)
    for h in range(H_C):
        bias_row = jnp.where(lane == h, bias_ref[0, h], bias_row)
        bias_row = jnp.where(lane == H_C + h, bias_ref[1, h], bias_row)
    gpre = gt_src[:, 0:128] + bias_row
    ig_all = gpre
    lf_all = _log_sigmoid(gpre)
    if lv != lp:
        row1 = lax.broadcasted_iota(jnp.int32, (lp, 1), 0)
        ig_all = jnp.where(row1 < lv, ig_all, NEG_BIG)
        lf_all = jnp.where(row1 < lv, lf_all, 0.0)
    rr = lax.broadcasted_iota(jnp.int32, (lp, lp), 0)
    cc = lax.broadcasted_iota(jnp.int32, (lp, lp), 1)
    tril = rr >= cc
    tril_b = jnp.where(tril, 1.0, 0.0).astype(BF16)
    lf_hi = lf_all.astype(BF16)
    r1 = lf_all - lf_hi.astype(F32)
    lf_mid = r1.astype(BF16)
    lf_lo = (r1 - lf_mid.astype(F32)).astype(BF16)
    b_all = (jnp.dot(tril_b, lf_hi, preferred_element_type=F32)
             + jnp.dot(tril_b, lf_mid, preferred_element_type=F32)
             + jnp.dot(tril_b, lf_lo, preferred_element_type=F32))
    lane_f = lax.broadcasted_iota(jnp.int32, (lp, 128), 1)
    rows_t = jnp.transpose(jnp.where(lane_f < H_C, ig_all, b_all))
    nt = (((1,), (1,)), ((), ()))

    def front(h):
        sl = slice((h % hpseg) * DH_C, (h % hpseg + 1) * DH_C)
        q = srcs[0 + h // hpseg][:, sl]
        k = srcs[2 + h // hpseg][:, sl] * (DH_C ** -0.5)
        v = srcs[4 + h // hpseg][:, sl]
        ig_col = ig_all[:, h:h + 1]
        b_col = b_all[:, H_C + h:H_C + h + 1]
        ig_row = rows_t[h:h + 1, :]
        b_row = rows_t[H_C + h:H_C + h + 1, :]
        dmat = jnp.where(tril, b_col - b_row + ig_row, -jnp.inf)
        m_prev = m_scr[h]
        carry = b_col + m_prev
        mt = jnp.maximum(carry, jnp.max(dmat, axis=1, keepdims=True))
        qb = q.astype(BF16)
        kb = k.astype(BF16)
        qk = lax.dot_general(qb, kb, nt, preferred_element_type=F32)
        cq = lax.dot_general(qb, c_scr[h].astype(BF16), nt, preferred_element_type=F32)
        return dict(sl=sl, q=q, k=k, v=v, kb=kb, ig_col=ig_col, b_col=b_col, m_prev=m_prev, carry=carry,
                    mt=mt, qk=qk, cq=cq, decay_w=jnp.exp(dmat - mt))

    def back(h, f):
        sl, q, k, v, kb, mt, carry = f["sl"], f["q"], f["k"], f["v"], f["kb"], f["mt"], f["carry"]
        s = f["qk"] * f["decay_w"]
        inter = jnp.exp(carry - mt)
        nq = jnp.sum(q * n_scr[h], axis=1, keepdims=True)
        num = jnp.dot(s.astype(BF16), v.astype(BF16), preferred_element_type=F32) + inter * f["cq"]

        m_new = mt[lp - 1:lp, :]
        b_last = f["b_col"][lp - 1:lp, :]
        w_col = jnp.exp(b_last - f["b_col"] + f["ig_col"] - m_new)
        decay = jnp.exp(b_last + f["m_prev"] - m_new)
        vw = (v * w_col).astype(BF16)
        c_scr[h] = decay * c_scr[h] + lax.dot_general(
            vw, kb, (((0,), (0,)), ((), ())), preferred_element_type=F32)
        n_scr[h] = decay * n_scr[h] + jnp.sum(k * w_col, axis=0, keepdims=True)
        m_scr[h] = m_new

        den = jnp.sum(s, axis=1, keepdims=True) + inter * nq
        hc = num / jnp.maximum(jnp.abs(den), jnp.exp(-mt))

        hsl = slice(h * DH_C, (h + 1) * DH_C)
        hc = hc * _sigmoid(srcs[6 + h // hpseg][:, sl])
        hc = hc * lax.rsqrt(jnp.mean(hc * hc, axis=-1, keepdims=True) + EPS) * hn_ref[:, hsl]
        res = (hc * _silu(srcs[8 + h // hpseg][:, sl])).astype(h_out.dtype)
        h_out[:, hsl] = res if lv == lp else res[0:lv, :]

    f_cur = front(0)
    for h in range(H_C):
        f_nxt = front(h + 1) if h + 1 < H_C else None
        back(h, f_cur)
        f_cur = f_nxt

    @pl.when(c == nc - 1)
    def _():
        c_out[...] = c_scr[...]
        n_out[...] = n_scr[...]
        m_out[...] = jnp.broadcast_to(m_scr[...], m_out.shape)


def _mlstm(z4, gates3, bias, head_norm, state, *, chunk, lp):
    nseg, b, t, _ = z4.shape
    nc = t // chunk
    w_c = H_C * DH_C
    zero_state = state is None
    n_pref = 1 if zero_state else 2

    def seg_spec(sidx):
        return pl.BlockSpec((None, None, chunk, SEG), lambda bi, c, *_: (sidx, bi, c, 0))

    def state_spec(*tail):
        return pl.BlockSpec((None, H_C) + tail, lambda bi, c, *_: (bi,) + (0,) * (len(tail) + 1))

    in_specs = [seg_spec(i) for i in range(nseg)]
    in_specs += [pl.BlockSpec((None, chunk, 128), lambda bi, c, *_: (bi, c, 0)),
                 pl.BlockSpec((1, w_c), lambda bi, c, *_: (0, 0))]
    args = [z4] * nseg + [gates3, head_norm.reshape(1, w_c)]
    prefetch = [bias]
    if not zero_state:
        c0, n0, m0 = state
        prefetch.append(m0)
        in_specs += [state_spec(DH_C, DH_C), state_spec(1, DH_C)]
        args += [c0, n0]
    grid_spec = pltpu.PrefetchScalarGridSpec(
        num_scalar_prefetch=n_pref,
        grid=(b, nc),
        in_specs=in_specs,
        out_specs=[
            pl.BlockSpec((None, chunk, w_c), lambda bi, c, *_: (bi, c, 0)),
            state_spec(DH_C, DH_C),
            state_spec(1, DH_C),
            state_spec(1, 128),
        ],
        scratch_shapes=[
            pltpu.VMEM((H_C, DH_C, DH_C), F32),
            pltpu.VMEM((H_C, 1, DH_C), F32),
            pltpu.VMEM((H_C, 1, 1), F32),
            pltpu.VMEM((nseg + 1, lp, SEG), F32),
        ],
    )
    return pl.pallas_call(
        functools.partial(_mlstm_kernel, lv=chunk, lp=lp, zero_state=zero_state),
        grid_spec=grid_spec,
        out_shape=[
            jax.ShapeDtypeStruct((b, t, w_c), BF16),
            jax.ShapeDtypeStruct((b, H_C, DH_C, DH_C), F32),
            jax.ShapeDtypeStruct((b, H_C, 1, DH_C), F32),
            jax.ShapeDtypeStruct((b, H_C, 1, 128), F32),
        ],
        compiler_params=_cparams(("parallel", "arbitrary")),
        name="mlstm",
    )(*prefetch, *args)


PAGES_PER_STEP = 16


def _sample_attn_kernel(pt_ref, slopes_ref, *refs, n_steps, tq, past_len):
    pp = PAGES_PER_STEP
    k_refs = refs[:pp]
    v_refs = refs[pp:2 * pp]
    qkvg_refs = refs[2 * pp:2 * pp + 4]
    o_ref, s_scr, pmax_scr, kmean_scr, acc_scr, l_scr, m_scr, sel_scr = refs[2 * pp + 4:]
    q_ref, kn_ref, vn_ref, ga_ref = qkvg_refs
    s = pl.program_id(1)
    rows = tq * H_A
    cols = PAGE_SIZE * H_A
    ppb = MOBA_BLOCK // PAGE_SIZE
    n_pages = n_steps * pp
    nt = (((1,), (1,)), ((), ()))
    log2e = math.log2(math.e)
    scale2 = DH_A ** -0.5 * log2e

    row = lax.broadcasted_iota(jnp.int32, (rows, 1), 0)
    row_h = row % H_A
    qpos = past_len + row // H_A
    slope2 = jnp.zeros((rows, 1), F32)
    for hh in range(H_A):
        slope2 = jnp.where(row_h == hh, slopes_ref[hh] * log2e, slope2)
    lane = lax.broadcasted_iota(jnp.int32, (1, cols), 1)

    def chunk(g):
        return pl.ds(pl.multiple_of(g * cols, cols), cols)

    def fold_lanes(x, op):
        acc = x[:, 0:128]
        for kk in range(1, cols // 128):
            acc = op(acc, x[:, kk * 128:(kk + 1) * 128])
        return acc

    def picked(sel, g):
        blk = g // ppb
        return (sel[0] == blk) | (sel[1] == blk) | (sel[2] == blk)

    @pl.when(s < n_steps)
    def _():
        qb = q_ref[...].reshape(rows, DH_A).astype(BF16)
        for jj in range(pp // ppb):
            tot = None
            for p in range(ppb):
                part = jnp.sum(k_refs[jj * ppb + p][...], axis=0)
                tot = part if tot is None else tot + part
            kmean_scr[pl.ds(pl.multiple_of((s * (pp // ppb) + jj) * H_A, H_A), H_A), :] = tot * (1.0 / MOBA_BLOCK)
        head_ok = (lane % H_A) == row_h
        qoff = slope2 * qpos.astype(F32)
        for p in range(pp):
            g = s * pp + p
            k2 = k_refs[p][...].reshape(cols, DH_A).astype(BF16)
            st = lax.dot_general(qb, k2, nt, preferred_element_type=F32)
            kpos = (g * PAGE_SIZE + lane // H_A).astype(F32)
            sv = st * scale2 + (slope2 * kpos - qoff)
            sv = jnp.where(head_ok, sv, -jnp.inf)
            s_scr[:, chunk(g)] = sv
            pmax_scr[g] = fold_lanes(sv, jnp.maximum)

    @pl.when(s == n_steps)
    def _():
        qb = q_ref[...].reshape(rows, DH_A).astype(BF16)
        nl = kmean_scr.shape[0]
        gate = lax.dot_general(qb, kmean_scr[...].astype(BF16), nt, preferred_element_type=F32)
        glane = lax.broadcasted_iota(jnp.int32, (rows, nl), 1)
        gate = jnp.where(glane % H_A == row_h, gate, -jnp.inf)
        sel = []
        sel_lane = lax.broadcasted_iota(jnp.int32, sel_scr.shape, 1)
        sel_tile = jnp.zeros(sel_scr.shape, jnp.int32)
        for t in range(MOBA_TOPK):
            mx = jnp.max(gate, axis=1, keepdims=True)
            idx = jnp.min(jnp.where(gate == mx, glane, nl), axis=1, keepdims=True)
            sel.append(idx // H_A)
            sel_tile = jnp.where(sel_lane == t, idx // H_A, sel_tile)
            gate = jnp.where(glane == idx, -jnp.inf, gate)
        sel_scr[...] = sel_tile

        kn2 = kn_ref[...].reshape(rows, DH_A).astype(BF16)
        olane = lax.broadcasted_iota(jnp.int32, (1, rows), 1)
        own_pos = past_len + olane // H_A
        s_own = lax.dot_general(qb, kn2, nt, preferred_element_type=F32) * scale2
        s_own = s_own - slope2 * (qpos - own_pos).astype(F32)
        s_own = jnp.where(((olane % H_A) == row_h) & (own_pos <= qpos), s_own, -jnp.inf)

        def max_body(g, macc):
            return jnp.maximum(macc, jnp.where(picked(sel, g), pmax_scr[g], -jnp.inf))

        macc = lax.fori_loop(0, n_pages, max_body, jnp.full((rows, 128), -jnp.inf, F32), unroll=4)
        m = jnp.maximum(jnp.max(macc, axis=1, keepdims=True), jnp.max(s_own, axis=1, keepdims=True))
        m_scr[...] = m
        p_own = jnp.exp2(s_own - m)
        l_scr[...] = jnp.where(lax.broadcasted_iota(jnp.int32, l_scr.shape, 1) == 0,
                               jnp.sum(p_own, axis=1, keepdims=True), 0.0)
        vn2 = vn_ref[...].reshape(rows, DH_A).astype(BF16)
        acc_scr[...] = jnp.dot(p_own.astype(BF16), vn2, preferred_element_type=F32)

    @pl.when(s >= n_steps)
    def _():
        sel = [sel_scr[:, t:t + 1] for t in range(MOBA_TOPK)]
        m = m_scr[...]
        acc = acc_scr[...]
        lacc = l_scr[...]
        for p in range(pp):
            g = (s - n_steps) * pp + p
            pr = jnp.exp2(s_scr[:, chunk(g)] - jnp.where(picked(sel, g), m, jnp.inf))
            lacc = lacc + fold_lanes(pr, jnp.add)
            v2 = v_refs[p][...].reshape(cols, DH_A).astype(BF16)
            acc = acc + jnp.dot(pr.astype(BF16), v2, preferred_element_type=F32)
        acc_scr[...] = acc
        l_scr[...] = lacc

    @pl.when(s == 2 * n_steps - 1)
    def _():
        o = acc_scr[...] / jnp.sum(l_scr[...], axis=1, keepdims=True)
        o = o * _silu(ga_ref[...].reshape(rows, DH_A))
        o_ref[...] = o.reshape(tq, H_A, DH_A)


def _sample_attn(cache_k4, cache_v4, page_table, slopes, qkvg):
    b, n_pages = page_table.shape
    _, tq, _, _ = qkvg[0].shape
    pp = PAGES_PER_STEP
    n_steps = n_pages // pp
    past_len = n_pages * PAGE_SIZE
    nbp = past_len // MOBA_BLOCK
    rows = tq * H_A

    def k_spec(p):
        return pl.BlockSpec((None, PAGE_SIZE, H_A, DH_A),
                            lambda bi, s, pt, sl: (pt[bi, jnp.minimum(s, n_steps - 1) * pp + p], 0, 0, 0))

    def v_spec(p):
        return pl.BlockSpec((None, PAGE_SIZE, H_A, DH_A),
                            lambda bi, s, pt, sl: (pt[bi, jnp.maximum(s - n_steps, 0) * pp + p], 0, 0, 0))

    new_spec = pl.BlockSpec((None, tq, H_A, DH_A), lambda bi, s, pt, sl: (bi, 0, 0, 0))

    grid_spec = pltpu.PrefetchScalarGridSpec(
        num_scalar_prefetch=2,
        grid=(b, 2 * n_steps),
        in_specs=[k_spec(p) for p in range(pp)] + [v_spec(p) for p in range(pp)] + [new_spec] * 4,
        out_specs=pl.BlockSpec((None, tq, H_A, DH_A), lambda bi, s, pt, sl: (bi, 0, 0, 0)),
        scratch_shapes=[
            pltpu.VMEM((rows, n_pages * PAGE_SIZE * H_A), F32),
            pltpu.VMEM((n_pages, rows, 128), F32),
            pltpu.VMEM((nbp * H_A, DH_A), F32),
            pltpu.VMEM((rows, DH_A), F32),
            pltpu.VMEM((rows, 128), F32),
            pltpu.VMEM((rows, 1), F32),
            pltpu.VMEM((rows, 128), jnp.int32),
        ],
    )
    return pl.pallas_call(
        functools.partial(_sample_attn_kernel, n_steps=n_steps, tq=tq, past_len=past_len),
        grid_spec=grid_spec,
        out_shape=jax.ShapeDtypeStruct((b, tq, H_A, DH_A), F32),
        compiler_params=_cparams(("parallel", "arbitrary")),
        name="sample_attn",
    )(page_table, slopes, *([cache_k4] * pp), *([cache_v4] * pp), *qkvg)


def _alibi_slopes(n_heads):
    return 2.0 ** (-8.0 * jnp.arange(1, n_heads + 1, dtype=F32) / n_heads)


def kernel(x_prompt, x_sample, cache_k, cache_v, state_pool, state_C, state_n, state_m, page_table,
           c_prompt, c_sample, ada_w, ada_b, norm_pre, norm_post, w_in_even, pool_w, pool_scale,
           w_out_even, w_in_odd, b_igate, b_fgate, head_norm, w_out_odd):
    bp, tp, d = x_prompt.shape
    bs, ts, _ = x_sample.shape
    n_pages = page_table.shape[1]
    past_len = n_pages * PAGE_SIZE
    w_a = H_A * DH_A
    w_c = H_C * DH_C
    rp, rs = bp * tp, bs * ts
    tm_p = 1024

    mods = _adaln(jnp.concatenate([c_prompt, c_sample], axis=0), ada_w, ada_b)

    def mod_parts(l):
        m = mods[l]
        shift, scale, gate = m[:, :d], m[:, d:2 * d], m[:, 2 * d:]
        prompt = tuple(a[:bp].reshape(bp, 1, d) for a in (shift, scale, gate))
        sample = tuple(jnp.repeat(a[bp:], ts, axis=0) for a in (shift, scale, gate))
        return prompt, sample

    slopes = _alibi_slopes(H_A)
    xp = x_prompt.reshape(rp, d)
    xs = x_sample.reshape(rs, d)

    (sh_p, sc_p, gt_p), (sh_s, sc_s, gt_s) = mod_parts(0)
    w_in0 = w_in_even[0].astype(BF16)
    zp, _ = _inproj(xp, norm_pre[0], sc_p, sh_p, w_in0, None, rows_per_group=tp, tm=tm_p)
    zs, qkvg_s = _inproj(xs, norm_pre[0], sc_s, sh_s, w_in0, None, rows_per_group=ts, tm=rs, hm=(0, 4, H_A))
    qkvg_s = [a.reshape(bs, ts, H_A, DH_A) for a in qkvg_s]
    kv_s = qkvg_s[1:3]
    z4p = zp.reshape(zp.shape[0], bp, tp, SEG)
    z4s = zs.reshape(zs.shape[0], bs, ts, SEG)

    pool_w_bf = pool_w[0].astype(BF16)
    att_p = _moba_prompt(z4p, slopes)
    pool_p, pstate_p = _pool(z4p, jnp.zeros((bp, POOL_HIST, SEG), F32), pool_w_bf, pool_scale[0],
                             pos0=0, tt=512, out_dtype=BF16)

    att_s = _sample_attn(cache_k[0], cache_v[0], page_table, slopes, qkvg_s)
    pool_s, pstate_s = _pool(z4s, state_pool[0], pool_w_bf, pool_scale[0],
                             pos0=past_len, tt=ts, out_dtype=F32)

    w_out0 = w_out_even[0].astype(BF16)
    w_list0 = [w_out0[:w_a], w_out0[w_a:]]
    xp1, kv_p = _outproj([att_p.reshape(rp, w_a), pool_p.reshape(rp, SEG)], w_list0, xp, gt_p, norm_post[0],
                         rows_per_group=tp, tm=512, hm=(zp, (1, 2), H_A))
    xs1 = _outproj([att_s.reshape(rs, w_a), pool_s.reshape(rs, SEG)], w_list0, xs, gt_s, norm_post[0],
                   rows_per_group=ts, tm=rs)

    (sh_p, sc_p, gt_p), (sh_s, sc_s, gt_s) = mod_parts(1)
    n_main = 5 * w_c
    w_in1 = w_in_odd[0].astype(BF16)
    wg1 = jnp.pad(w_in_odd[0][:, n_main:], ((0, 0), (0, 128 - 2 * H_C))).astype(BF16)
    zp, gates_p = _inproj(xp1, norm_pre[1], sc_p, sh_p, w_in1, wg1, rows_per_group=tp, tm=tm_p)
    zs, gates_s = _inproj(xs1, norm_pre[1], sc_s, sh_s, w_in1, wg1, rows_per_group=ts, tm=rs)
    bias = jnp.stack([b_igate[0], b_fgate[0]]).astype(F32)
    hc_p, c_p, n_p, m_p = _mlstm(zp.reshape(zp.shape[0], bp, tp, SEG), gates_p.reshape(bp, tp, 128),
                                 bias, head_norm[0], None, chunk=256, lp=256)
    state = (state_C[0], state_n[0].reshape(bs, H_C, 1, DH_C), state_m[0])
    hc_s, c_s, n_s, m_s = _mlstm(zs.reshape(zs.shape[0], bs, ts, SEG), gates_s.reshape(bs, ts, 128),
                                 bias, head_norm[0], state, chunk=ts, lp=128)
    w_out1 = w_out_odd[0].astype(BF16)
    xp2 = _outproj([hc_p.reshape(rp, w_c)], [w_out1], xp1, gt_p, norm_post[1], rows_per_group=tp, tm=512)
    xs2 = _outproj([hc_s.reshape(rs, w_c)], [w_out1], xs1, gt_s, norm_post[1], rows_per_group=ts, tm=rs)

    kv_shape_p = (1, bp, tp, H_A, DH_A)
    kv_shape_s = (1, bs, ts, H_A, DH_A)
    return (xp2.reshape(bp, tp, d), xs2.reshape(bs, ts, d),
            kv_p[0].reshape(kv_shape_p), kv_p[1].reshape(kv_shape_p),
            kv_s[0].reshape(kv_shape_s), kv_s[1].reshape(kv_shape_s),
            pstate_p[None], pstate_s[None],
            c_p[None], n_p.reshape(1, bp, H_C, DH_C), m_p[:, :, 0, 0][None],
            c_s[None], n_s.reshape(1, bs, H_C, DH_C), m_s[:, :, 0, 0][None])
```

```python
import functools
import math

import jax
import jax.numpy as jnp
from jax import lax
from jax.experimental import pallas as pl
from jax.experimental.pallas import tpu as pltpu

F32 = jnp.float32
BF16 = jnp.bfloat16

H_A = 8
DH_A = 128
MOBA_BLOCK = 256
MOBA_TOPK = 3
POOL_WINDOWS = (2, 4, 8, 16)
GW_B = 256
POOL_HIST = max(POOL_WINDOWS) - 1
H_C = 8
DH_C = 256
EPS = 1e-6
PAGE_SIZE = 128

SEG = 1024
NEG_BIG = -1e30

VMEM_LIMIT_V7X = 52 * 1024 * 1024


def _sigmoid(x):
    return 0.5 * jnp.tanh(0.5 * x) + 0.5


def _silu(x):
    return x * _sigmoid(x)


def _log_sigmoid(x):
    return -(jnp.maximum(-x, 0.0) + jnp.log(1.0 + jnp.exp(-jnp.abs(x))))


def _cparams(sem):
    return pltpu.CompilerParams(dimension_semantics=sem, vmem_limit_bytes=VMEM_LIMIT_V7X)


def _adaln_kernel(c_ref, w_ref, b_ref, o_ref):
    s = _silu(c_ref[...]).astype(BF16)
    o_ref[...] = jnp.dot(s, w_ref[...].astype(BF16), preferred_element_type=F32) + b_ref[...]


def _adaln(c_all, ada_w, ada_b, tn=512):
    depth, d, n = ada_w.shape
    r = c_all.shape[0]
    return pl.pallas_call(
        _adaln_kernel,
        grid=(depth, n // tn),
        in_specs=[
            pl.BlockSpec((r, d), lambda l, j: (0, 0)),
            pl.BlockSpec((None, d, tn), lambda l, j: (l, 0, j)),
            pl.BlockSpec((None, 1, tn), lambda l, j: (l, 0, j)),
        ],
        out_specs=pl.BlockSpec((None, r, tn), lambda l, j: (l, 0, j)),
        out_shape=jax.ShapeDtypeStruct((depth, r, n), F32),
        compiler_params=_cparams(("parallel", "parallel")),
        name="adaln",
    )(c_all, ada_w, ada_b.reshape(depth, 1, n))


def _inproj_kernel(*refs, has_gate, hm, nch):
    hm_refs = ()
    gt_ref = wg_ref = None
    if has_gate:
        x_ref, g_ref, sc_ref, sh_ref, w_ref, wg_ref, z_ref, gt_ref, h_even, h_odd = refs
    elif hm:
        x_ref, g_ref, sc_ref, sh_ref, w_ref, z_ref = refs[:6]
        hm_refs = refs[6:-2]
        h_even, h_odd = refs[-2:]
    else:
        x_ref, g_ref, sc_ref, sh_ref, w_ref, z_ref, h_even, h_odd = refs
    i = pl.program_id(0)
    j = pl.program_id(1)
    tm = x_ref.shape[0]
    rc = tm // nch

    def normed(rows):
        x = x_ref[rows, :]
        sc = sc_ref[...] if sc_ref.shape[0] == 1 else sc_ref[rows, :]
        sh = sh_ref[...] if sh_ref.shape[0] == 1 else sh_ref[rows, :]
        r = x * lax.rsqrt(jnp.mean(x * x, axis=-1, keepdims=True) + EPS)
        return ((r * g_ref[...]) * (1.0 + sc) + sh).astype(BF16)

    @pl.when((i == 0) & (j == 0))
    def _():
        h_even[...] = normed(slice(None))

    def step(h_cur, h_nxt):
        if has_gate:
            @pl.when(j == 0)
            def _():
                gt_ref[...] = jnp.dot(h_cur[...], wg_ref[...], preferred_element_type=F32)

        c = jnp.clip(j - 1, 0, nch - 1)
        rows = pl.ds(pl.multiple_of(c * rc, rc), rc)
        h_nxt[rows, :] = normed(rows)
        res = jnp.dot(h_cur[...], w_ref[...], preferred_element_type=F32)
        z_ref[...] = res

        if hm:
            first, count, heads = hm
            dh = SEG // heads
            for idx, hm_ref in enumerate(hm_refs):
                @pl.when(j == first + idx)
                def _(hm_ref=hm_ref):
                    for hh in range(heads):
                        hm_ref[:, hh, :] = res[:, hh * dh:(hh + 1) * dh]

    pl.when(i % 2 == 0)(functools.partial(step, h_even, h_odd))
    pl.when(i % 2 == 1)(functools.partial(step, h_odd, h_even))


def _inproj(x2, g_pre, scale, shift, w_bf, wg_bf, *, rows_per_group, tm, hm=None):
    r, d = x2.shape
    n = w_bf.shape[1]
    nseg = n // SEG
    n_tiles = r // tm
    has_gate = wg_bf is not None
    nch = 1 << ((nseg - 1).bit_length() - 1)

    def ahead(i, j):
        return jnp.where((i == 0) & (j == 0), 0, jnp.minimum(i + 1, n_tiles - 1))

    if scale.ndim == 3:
        per = rows_per_group // tm
        mod_spec = pl.BlockSpec((None, 1, d), lambda i, j: (ahead(i, j) // per, 0, 0))
    else:
        mod_spec = pl.BlockSpec((tm, d), lambda i, j: (ahead(i, j), 0))
    x_mode = {"pipeline_mode": pl.Buffered(1)} if hm else {}
    in_specs = [
        pl.BlockSpec((tm, d), lambda i, j: (ahead(i, j), 0), **x_mode),
        pl.BlockSpec((1, d), lambda i, j: (0, 0)),
        mod_spec,
        mod_spec,
        pl.BlockSpec((d, SEG), lambda i, j: (0, j)),
    ]
    args = [x2, g_pre.reshape(1, d), scale, shift, w_bf]
    out_specs = [pl.BlockSpec((None, tm, SEG), lambda i, j: (j, i, 0))]
    out_shape = [jax.ShapeDtypeStruct((nseg, r, SEG), F32)]
    if has_gate:
        in_specs.append(pl.BlockSpec((d, 128), lambda i, j: (0, 0)))
        args.append(wg_bf)
        out_specs.append(pl.BlockSpec((tm, 128), lambda i, j: (i, 0)))
        out_shape.append(jax.ShapeDtypeStruct((r, 128), F32))
    if hm:
        first, count, heads = hm
        for _ in range(count):
            out_specs.append(pl.BlockSpec((tm, heads, SEG // heads), lambda i, j: (i, 0, 0),
                                          pipeline_mode=pl.Buffered(1)))
            out_shape.append(jax.ShapeDtypeStruct((r, heads, SEG // heads), F32))
    res = pl.pallas_call(
        functools.partial(_inproj_kernel, has_gate=has_gate, hm=hm, nch=nch),
        grid=(n_tiles, nseg),
        in_specs=in_specs,
        out_specs=out_specs,
        out_shape=out_shape,
        scratch_shapes=[pltpu.VMEM((tm, d), BF16), pltpu.VMEM((tm, d), BF16)],
        compiler_params=_cparams(("arbitrary", "arbitrary")),
        name="inproj_gate" if has_gate else "inproj",
    )(*args)
    if hm:
        return res[0], list(res[1:])
    return (res[0], res[1]) if has_gate else (res[0], None)


def _outproj_kernel(*refs, n_a, n_hm, heads):
    a_refs = refs[:n_a]
    w_refs = refs[n_a:2 * n_a]
    x_ref, gate_ref, gp_ref = refs[2 * n_a:2 * n_a + 3]
    seg_refs = refs[2 * n_a + 3:2 * n_a + 3 + n_hm]
    o_ref = refs[2 * n_a + 3 + n_hm]
    hm_refs = refs[2 * n_a + 4 + n_hm:]
    for seg_ref, hm_ref in zip(seg_refs, hm_refs):
        dh = seg_ref.shape[1] // heads
        for hh in range(heads):
            hm_ref[pl.ds(hh, seg_ref.shape[0], stride=heads), :] = seg_ref[:, hh * dh:(hh + 1) * dh]
    y = None
    for a_ref, w_ref in zip(a_refs, w_refs):
        t = jnp.dot(a_ref[...].astype(BF16), w_ref[...], preferred_element_type=F32)
        y = t if y is None else y + t
    r = y * lax.rsqrt(jnp.mean(y * y, axis=-1, keepdims=True) + EPS)
    o_ref[...] = x_ref[...] + gate_ref[...] * (r * gp_ref[...])


def _outproj(a_list, w_list, x2, gate, g_post, *, rows_per_group, tm, hm=None):
    r, d = x2.shape
    n_a = len(a_list)
    if gate.ndim == 3:
        per = rows_per_group // tm
        gate_spec = pl.BlockSpec((None, 1, d), lambda i: (i // per, 0, 0))
    else:
        gate_spec = pl.BlockSpec((tm, d), lambda i: (i, 0))
    in_specs = [pl.BlockSpec((tm, a.shape[1]), lambda i: (i, 0)) for a in a_list]
    in_specs += [pl.BlockSpec(w.shape, lambda i: (0, 0), pipeline_mode=pl.Buffered(1)) for w in w_list]
    in_specs += [pl.BlockSpec((tm, d), lambda i: (i, 0)), gate_spec, pl.BlockSpec((1, d), lambda i: (0, 0))]
    args = [*a_list, *w_list, x2, gate, g_post.reshape(1, d)]
    out_specs = [pl.BlockSpec((tm, d), lambda i: (i, 0))]
    out_shape = [jax.ShapeDtypeStruct((r, d), F32)]
    n_hm, heads = 0, 1
    if hm:
        z3, seg_ids, heads = hm
        n_hm = len(seg_ids)
        for sidx in seg_ids:
            in_specs.append(pl.BlockSpec((None, tm, SEG), lambda i, sidx=sidx: (sidx, i, 0)))
            args.append(z3)
            out_specs.append(pl.BlockSpec((tm * heads, SEG // heads), lambda i: (i, 0)))
            out_shape.append(jax.ShapeDtypeStruct((r * heads, SEG // heads), F32))
    res = pl.pallas_call(
        functools.partial(_outproj_kernel, n_a=n_a, n_hm=n_hm, heads=heads),
        grid=(r // tm,),
        in_specs=in_specs,
        out_specs=out_specs,
        out_shape=out_shape,
        compiler_params=_cparams(("parallel",)),
        name="outproj",
    )(*args)
    return (res[0], list(res[1:])) if hm else res[0]


MOBA_AUX = 128
ALIBI_COL = 8
MOBA_HEADS_PER_STEP = 2


def _moba_kernel(slopes_ref, q_ref, k_ref, v_ref, ga_ref, *rest, nb, hps, cast_rider):
    if cast_rider:
        win_ref, o_ref, wout_ref, kaug_scr, vt_scr = rest
        wout_ref[...] = win_ref[...].astype(wout_ref.dtype)
    else:
        o_ref, kaug_scr, vt_scr = rest
    blk = MOBA_BLOCK
    t = nb * blk
    log2e = math.log2(math.e)
    scale2 = DH_A ** -0.5 * log2e
    nt = (((1,), (1,)), ((), ()))
    pos = lax.broadcasted_iota(jnp.int32, (t, MOBA_AUX), 0)
    lane = lax.broadcasted_iota(jnp.int32, (t, MOBA_AUX), 1)
    posf = pos.astype(F32)
    pos_hi = posf.astype(BF16).astype(F32)
    pos_lo = posf - pos_hi
    aux_b = jnp.where(lane < ALIBI_COL, jnp.where(lane == pos // blk, 1.0, 0.0),
                      jnp.where(lane < ALIBI_COL + 3, pos_hi,
                                jnp.where(lane < ALIBI_COL + 6, pos_lo, 0.0))).astype(BF16)
    srow = lax.broadcasted_iota(jnp.int32, (8, blk), 0)
    rowb = lax.broadcasted_iota(jnp.int32, (nb, blk), 0)
    krow = lax.broadcasted_iota(jnp.int32, (blk, blk), 0)
    qcol = lax.broadcasted_iota(jnp.int32, (blk, blk), 1)

    def head(hh):
        hsl = slice(hh * DH_A, (hh + 1) * DH_A)
        k = k_ref[:, hsl]
        kmean_b = jnp.concatenate(
            [jnp.mean(k[j * blk:(j + 1) * blk, :], axis=0, keepdims=True) for j in range(nb)],
            axis=0).astype(BF16)
        sl2 = jnp.full((8, blk), slopes_ref[pl.program_id(1) * hps + hh] * log2e, F32)
        sl_hi = sl2.astype(BF16).astype(F32)
        sl_mid = (sl2 - sl_hi).astype(BF16).astype(F32)
        sl_lo = sl2 - sl_hi - sl_mid
        slope_rows = jnp.where(srow >= 6, 0.0,
                               jnp.where(srow % 3 == 0, sl_hi, jnp.where(srow % 3 == 1, sl_mid, sl_lo)))
        kaug_scr[hh] = jnp.concatenate([k.astype(BF16), aux_b], axis=1)
        vt_scr[hh] = jnp.transpose(v_ref[:, hsl]).astype(BF16)

        def scores(n):
            nk = (n + 1) * blk
            q = q_ref[n * blk:nk, hsl]
            if n > MOBA_TOPK:
                gate = lax.dot_general(kmean_b, q.astype(BF16), nt, preferred_element_type=F32)
                gm = jnp.where(rowb < n, gate, -jnp.inf)
                cnt = jnp.zeros((nb, blk), F32)
                for kk in range(n):
                    gk = gm[kk:kk + 1, :]
                    beats = (gk > gm) | ((gk == gm) & (rowb > kk))
                    cnt = cnt + jnp.where(beats, 1.0, 0.0)
                keep = ((rowb < n) & (cnt < MOBA_TOPK)) | (rowb == n)
                bias = jnp.where(keep, 0.0, NEG_BIG)
            else:
                bias = jnp.zeros((nb, blk), F32)
            pieces = [bias, slope_rows]
            if nb < ALIBI_COL:
                pieces.insert(1, jnp.zeros((ALIBI_COL - nb, blk), F32))
            pieces.append(jnp.zeros((MOBA_AUX - ALIBI_COL - 8, blk), F32))
            rhs = jnp.concatenate([(jnp.transpose(q) * scale2).astype(BF16),
                                   jnp.concatenate(pieces, axis=0).astype(BF16)], axis=0)
            return jnp.dot(kaug_scr[hh, 0:nk, :], rhs, preferred_element_type=F32)

        def finish(n, s):
            nk = (n + 1) * blk
            s_own = jnp.where(krow <= qcol, s[n * blk:nk, :], -jnp.inf)
            m = jnp.max(s_own, axis=0, keepdims=True)
            if n > 0:
                m = jnp.maximum(m, jnp.max(s[0:n * blk, :], axis=0, keepdims=True))
            p_own = jnp.exp2(s_own - m)
            l = jnp.sum(p_own, axis=0, keepdims=True)
            ot = jnp.dot(vt_scr[hh, :, n * blk:nk], p_own.astype(BF16), preferred_element_type=F32)
            if n > 0:
                p = jnp.exp2(s[0:n * blk, :] - m)
                l = l + jnp.sum(p, axis=0, keepdims=True)
                ot = ot + jnp.dot(vt_scr[hh, :, 0:n * blk], p.astype(BF16), preferred_element_type=F32)
            o = jnp.transpose(ot / l)
            o_ref[n * blk:nk, hsl] = (o * _silu(ga_ref[n * blk:nk, hsl])).astype(o_ref.dtype)

        s_cur = scores(0)
        for n in range(nb):
            s_nxt = scores(n + 1) if n + 1 < nb else None
            finish(n, s_cur)
            s_cur = s_nxt

    for hh in range(hps):
        head(hh)


def _moba_prompt(z4, slopes, w_next=None):
    _, b, t, w = z4.shape
    nb = t // MOBA_BLOCK
    assert nb <= ALIBI_COL
    hps = MOBA_HEADS_PER_STEP
    ngrp = H_A // hps

    def seg_spec(sidx):
        return pl.BlockSpec((None, None, t, hps * DH_A), lambda bi, hg, s: (sidx, bi, 0, hg))

    in_specs = [seg_spec(0), seg_spec(1), seg_spec(2), seg_spec(3)]
    args = [z4, z4, z4, z4]
    out_specs = [pl.BlockSpec((None, t, hps * DH_A), lambda bi, hg, s: (bi, 0, hg))]
    out_shape = [jax.ShapeDtypeStruct((b, t, w), BF16)]
    if w_next is not None:
        wr, wc = w_next.shape
        slab = wr // (b * ngrp)
        assert slab * b * ngrp == wr and slab % 16 == 0
        in_specs.append(pl.BlockSpec((slab, wc), lambda bi, hg, s: (bi * ngrp + hg, 0)))
        args.append(w_next)
        out_specs.append(pl.BlockSpec((slab, wc), lambda bi, hg, s: (bi * ngrp + hg, 0)))
        out_shape.append(jax.ShapeDtypeStruct((wr, wc), BF16))
    grid_spec = pltpu.PrefetchScalarGridSpec(
        num_scalar_prefetch=1,
        grid=(b, ngrp),
        in_specs=in_specs,
        out_specs=out_specs,
        scratch_shapes=[
            pltpu.VMEM((hps, t, DH_A + MOBA_AUX), BF16),
            pltpu.VMEM((hps, DH_A, t), BF16),
        ],
    )
    res = pl.pallas_call(
        functools.partial(_moba_kernel, nb=nb, hps=hps, cast_rider=w_next is not None),
        grid_spec=grid_spec,
        out_shape=out_shape,
        compiler_params=_cparams(("parallel", "parallel")),
        name="moba_prompt",
    )(slopes, *args)
    return res if w_next is not None else res[0]


def _pool_kernel(u_ref, gb_ref, prev_ref, pw_ref, ps_ref, o_ref, pn_ref, ext_scr, *, tt, pos0, nt):
    t = pl.program_id(1)
    hist = POOL_HIST + 1

    @pl.when(t == 0)
    def _():
        ext_scr[0:1, :] = jnp.zeros((1, ext_scr.shape[1]), F32)
        ext_scr[1:hist, :] = prev_ref[...]

    ext_scr[hist:hist + tt, :] = u_ref[...]
    pos = pos0 + t * tt + lax.broadcasted_iota(jnp.int32, (tt, 1), 0)
    for g, w in enumerate(POOL_WINDOWS):
        sl = slice(g * GW_B, (g + 1) * GW_B)
        x = u_ref[:, sl]
        acc = x
        for jj in range(1, w):
            acc = acc + ext_scr[hist - jj:hist - jj + tt, sl]
        cnt = jnp.minimum(w, pos + 1).astype(F32)
        d = acc / cnt - x
        y = jnp.dot(d.astype(BF16), pw_ref[g], preferred_element_type=F32) * ps_ref[:, sl]
        o_ref[:, sl] = (y * _silu(gb_ref[:, sl])).astype(o_ref.dtype)

    tail = ext_scr[tt:tt + hist, :]
    ext_scr[0:hist, :] = tail

    @pl.when(t == nt - 1)
    def _():
        pn_ref[...] = ext_scr[1:hist, :]


def _pool(z4, prev, pool_w_bf, pool_scale, *, pos0, tt, out_dtype):
    _, b, t, w = z4.shape
    nt = t // tt
    return pl.pallas_call(
        functools.partial(_pool_kernel, tt=tt, pos0=pos0, nt=nt),
        grid=(b, nt),
        in_specs=[
            pl.BlockSpec((None, None, tt, w), lambda bi, ti: (4, bi, ti, 0)),
            pl.BlockSpec((None, None, tt, w), lambda bi, ti: (5, bi, ti, 0)),
            pl.BlockSpec((None, POOL_HIST, w), lambda bi, ti: (bi, 0, 0)),
            pl.BlockSpec(pool_w_bf.shape, lambda bi, ti: (0, 0, 0)),
            pl.BlockSpec((1, w), lambda bi, ti: (0, 0)),
        ],
        out_specs=[
            pl.BlockSpec((None, tt, w), lambda bi, ti: (bi, ti, 0)),
            pl.BlockSpec((None, POOL_HIST, w), lambda bi, ti: (bi, 0, 0)),
        ],
        out_shape=[
            jax.ShapeDtypeStruct((b, t, w), out_dtype),
            jax.ShapeDtypeStruct((b, POOL_HIST, w), F32),
        ],
        scratch_shapes=[pltpu.VMEM((POOL_HIST + 1 + tt, w), F32)],
        compiler_params=_cparams(("parallel", "arbitrary")),
        name="pool",
    )(z4, z4, prev, pool_w_bf, pool_scale.reshape(1, w))


def _mlstm_kernel(*refs, lv, lp, zero_state):
    nseg = 2 * 5
    if zero_state:
        bias_ref = refs[0]
        seg_refs = refs[1:1 + nseg]
        gt_ref, hn_ref = refs[1 + nseg:3 + nseg]
        rest = refs[3 + nseg:]
    else:
        bias_ref, m0_ref = refs[:2]
        seg_refs = refs[2:2 + nseg]
        gt_ref, hn_ref, c0_ref, n0_ref = refs[2 + nseg:6 + nseg]
        rest = refs[6 + nseg:]
    h_out, c_out, n_out, m_out, c_scr, n_scr, m_scr, pad_scr = rest
    bi = pl.program_id(0)
    c = pl.program_id(1)
    nc = pl.num_programs(1)
    hpseg = SEG // DH_C

    @pl.when(c == 0)
    def _():
        if zero_state:
            c_scr[...] = jnp.zeros(c_scr.shape, F32)
            n_scr[...] = jnp.zeros(n_scr.shape, F32)
            m_scr[...] = jnp.zeros(m_scr.shape, F32)
        else:
            c_scr[...] = c0_ref[...]
            n_scr[...] = n0_ref[...]
            for h in range(H_C):
                m_scr[h] = jnp.full(m_scr.shape[1:], m0_ref[bi, h], F32)

    def load(ref, slot):
        if lv == lp:
            return ref
        pad_scr[slot] = jnp.zeros(pad_scr.shape[1:], F32)
        pad_scr[slot, 0:lv, 0:ref.shape[1]] = ref[...]
        return pad_scr.at[slot]

    srcs = [load(r, i) for i, r in enumerate(seg_refs)]
    gt_src = load(gt_ref, nseg)

    lane = lax.broadcasted_iota(jnp.int32, (1, 128), 1)
    bias_row = jnp.zeros((1, 128), F32)
    for h in range(H_C):
        bias_row = jnp.where(lane == h, bias_ref[0, h], bias_row)
        bias_row = jnp.where(lane == H_C + h, bias_ref[1, h], bias_row)
    gpre = gt_src[:, 0:128] + bias_row
    ig_all = gpre
    lf_all = _log_sigmoid(gpre)
    if lv != lp:
        row1 = lax.broadcasted_iota(jnp.int32, (lp, 1), 0)
        ig_all = jnp.where(row1 < lv, ig_all, NEG_BIG)
        lf_all = jnp.where(row1 < lv, lf_all, 0.0)
    rr = lax.broadcasted_iota(jnp.int32, (lp, lp), 0)
    cc = lax.broadcasted_iota(jnp.int32, (lp, lp), 1)
    tril = rr >= cc
    tril_b = jnp.where(tril, 1.0, 0.0).astype(BF16)
    lf_hi = lf_all.astype(BF16)
    r1 = lf_all - lf_hi.astype(F32)
    lf_mid = r1.astype(BF16)
    lf_lo = (r1 - lf_mid.astype(F32)).astype(BF16)
    b_all = (jnp.dot(tril_b, lf_hi, preferred_element_type=F32)
             + jnp.dot(tril_b, lf_mid, preferred_element_type=F32)
             + jnp.dot(tril_b, lf_lo, preferred_element_type=F32))
    lane_f = lax.broadcasted_iota(jnp.int32, (lp, 128), 1)
    rows_t = jnp.transpose(jnp.where(lane_f < H_C, ig_all, b_all))
    nt = (((1,), (1,)), ((), ()))

    def front(h):
        sl = slice((h % hpseg) * DH_C, (h % hpseg + 1) * DH_C)
        q = srcs[0 + h // hpseg][:, sl]
        k = srcs[2 + h // hpseg][:, sl] * (DH_C ** -0.5)
        v = srcs[4 + h // hpseg][:, sl]
        ig_col = ig_all[:, h:h + 1]
        b_col = b_all[:, H_C + h:H_C + h + 1]
        ig_row = rows_t[h:h + 1, :]
        b_row = rows_t[H_C + h:H_C + h + 1, :]
        dmat = jnp.where(tril, b_col - b_row + ig_row, -jnp.inf)
        m_prev = m_scr[h]
        carry = b_col + m_prev
        mt = jnp.maximum(carry, jnp.max(dmat, axis=1, keepdims=True))
        qb = q.astype(BF16)
        kb = k.astype(BF16)
        qk = lax.dot_general(qb, kb, nt, preferred_element_type=F32)
        cq = lax.dot_general(qb, c_scr[h].astype(BF16), nt, preferred_element_type=F32)
        return dict(sl=sl, q=q, k=k, v=v, kb=kb, ig_col=ig_col, b_col=b_col, m_prev=m_prev, carry=carry,
                    mt=mt, qk=qk, cq=cq, decay_w=jnp.exp(dmat - mt))

    def back(h, f):
        sl, q, k, v, kb, mt, carry = f["sl"], f["q"], f["k"], f["v"], f["kb"], f["mt"], f["carry"]
        s = f["qk"] * f["decay_w"]
        inter = jnp.exp(carry - mt)
        nq = jnp.sum(q * n_scr[h], axis=1, keepdims=True)
        num = jnp.dot(s.astype(BF16), v.astype(BF16), preferred_element_type=F32) + inter * f["cq"]

        m_new = mt[lp - 1:lp, :]
        b_last = f["b_col"][lp - 1:lp, :]
        w_col = jnp.exp(b_last - f["b_col"] + f["ig_col"] - m_new)
        decay = jnp.exp(b_last + f["m_prev"] - m_new)
        vw = (v * w_col).astype(BF16)
        c_scr[h] = decay * c_scr[h] + lax.dot_general(
            vw, kb, (((0,), (0,)), ((), ())), preferred_element_type=F32)
        n_scr[h] = decay * n_scr[h] + jnp.sum(k * w_col, axis=0, keepdims=True)
        m_scr[h] = m_new

        den = jnp.sum(s, axis=1, keepdims=True) + inter * nq
        hc = num / jnp.maximum(jnp.abs(den), jnp.exp(-mt))

        hsl = slice(h * DH_C, (h + 1) * DH_C)
        hc = hc * _sigmoid(srcs[6 + h // hpseg][:, sl])
        hc = hc * lax.rsqrt(jnp.mean(hc * hc, axis=-1, keepdims=True) + EPS) * hn_ref[:, hsl]
        res = (hc * _silu(srcs[8 + h // hpseg][:, sl])).astype(h_out.dtype)
        h_out[:, hsl] = res if lv == lp else res[0:lv, :]

    f_cur = front(0)
    for h in range(H_C):
        f_nxt = front(h + 1) if h + 1 < H_C else None
        back(h, f_cur)
        f_cur = f_nxt

    @pl.when(c == nc - 1)
    def _():
        c_out[...] = c_scr[...]
        n_out[...] = n_scr[...]
        m_out[...] = jnp.broadcast_to(m_scr[...], m_out.shape)


def _mlstm(z4, gates3, bias, head_norm, state, *, chunk, lp):
    nseg, b, t, _ = z4.shape
    nc = t // chunk
    w_c = H_C * DH_C
    zero_state = state is None
    n_pref = 1 if zero_state else 2

    def seg_spec(sidx):
        return pl.BlockSpec((None, None, chunk, SEG), lambda bi, c, *_: (sidx, bi, c, 0))

    def state_spec(*tail):
        return pl.BlockSpec((None, H_C) + tail, lambda bi, c, *_: (bi,) + (0,) * (len(tail) + 1))

    in_specs = [seg_spec(i) for i in range(nseg)]
    in_specs += [pl.BlockSpec((None, chunk, 128), lambda bi, c, *_: (bi, c, 0)),
                 pl.BlockSpec((1, w_c), lambda bi, c, *_: (0, 0))]
    args = [z4] * nseg + [gates3, head_norm.reshape(1, w_c)]
    prefetch = [bias]
    if not zero_state:
        c0, n0, m0 = state
        prefetch.append(m0)
        in_specs += [state_spec(DH_C, DH_C), state_spec(1, DH_C)]
        args += [c0, n0]
    grid_spec = pltpu.PrefetchScalarGridSpec(
        num_scalar_prefetch=n_pref,
        grid=(b, nc),
        in_specs=in_specs,
        out_specs=[
            pl.BlockSpec((None, chunk, w_c), lambda bi, c, *_: (bi, c, 0)),
            state_spec(DH_C, DH_C),
            state_spec(1, DH_C),
            state_spec(1, 128),
        ],
        scratch_shapes=[
            pltpu.VMEM((H_C, DH_C, DH_C), F32),
            pltpu.VMEM((H_C, 1, DH_C), F32),
            pltpu.VMEM((H_C, 1, 1), F32),
            pltpu.VMEM((nseg + 1, lp, SEG), F32),
        ],
    )
    return pl.pallas_call(
        functools.partial(_mlstm_kernel, lv=chunk, lp=lp, zero_state=zero_state),
        grid_spec=grid_spec,
        out_shape=[
            jax.ShapeDtypeStruct((b, t, w_c), BF16),
            jax.ShapeDtypeStruct((b, H_C, DH_C, DH_C), F32),
            jax.ShapeDtypeStruct((b, H_C, 1, DH_C), F32),
            jax.ShapeDtypeStruct((b, H_C, 1, 128), F32),
        ],
        compiler_params=_cparams(("parallel", "arbitrary")),
        name="mlstm",
    )(*prefetch, *args)


PAGES_PER_STEP = 16


def _sample_attn_kernel(pt_ref, slopes_ref, *refs, n_steps, tq, past_len):
    pp = PAGES_PER_STEP
    k_refs = refs[:pp]
    v_refs = refs[pp:2 * pp]
    qkvg_refs = refs[2 * pp:2 * pp + 4]
    o_ref, s_scr, pmax_scr, kmean_scr, acc_scr, l_scr, m_scr, sel_scr = refs[2 * pp + 4:]
    q_ref, kn_ref, vn_ref, ga_ref = qkvg_refs
    s = pl.program_id(1)
    rows = tq * H_A
    cols = PAGE_SIZE * H_A
    ppb = MOBA_BLOCK // PAGE_SIZE
    n_pages = n_steps * pp
    nt = (((1,), (1,)), ((), ()))
    log2e = math.log2(math.e)
    scale2 = DH_A ** -0.5 * log2e

    row = lax.broadcasted_iota(jnp.int32, (rows, 1), 0)
    row_h = row % H_A
    qpos = past_len + row // H_A
    slope2 = jnp.zeros((rows, 1), F32)
    for hh in range(H_A):
        slope2 = jnp.where(row_h == hh, slopes_ref[hh] * log2e, slope2)
    lane = lax.broadcasted_iota(jnp.int32, (1, cols), 1)

    def chunk(g):
        return pl.ds(pl.multiple_of(g * cols, cols), cols)

    def fold_lanes(x, op):
        acc = x[:, 0:128]
        for kk in range(1, cols // 128):
            acc = op(acc, x[:, kk * 128:(kk + 1) * 128])
        return acc

    def picked(sel, g):
        blk = g // ppb
        return (sel[0] == blk) | (sel[1] == blk) | (sel[2] == blk)

    @pl.when(s < n_steps)
    def _():
        qb = q_ref[...].reshape(rows, DH_A).astype(BF16)
        for jj in range(pp // ppb):
            tot = None
            for p in range(ppb):
                part = jnp.sum(k_refs[jj * ppb + p][...], axis=0)
                tot = part if tot is None else tot + part
            kmean_scr[pl.ds(pl.multiple_of((s * (pp // ppb) + jj) * H_A, H_A), H_A), :] = tot * (1.0 / MOBA_BLOCK)
        head_ok = (lane % H_A) == row_h
        qoff = slope2 * qpos.astype(F32)
        for p in range(pp):
            g = s * pp + p
            k2 = k_refs[p][...].reshape(cols, DH_A).astype(BF16)
            st = lax.dot_general(qb, k2, nt, preferred_element_type=F32)
            kpos = (g * PAGE_SIZE + lane // H_A).astype(F32)
            sv = st * scale2 + (slope2 * kpos - qoff)
            sv = jnp.where(head_ok, sv, -jnp.inf)
            s_scr[:, chunk(g)] = sv
            pmax_scr[g] = fold_lanes(sv, jnp.maximum)

    @pl.when(s == n_steps)
    def _():
        qb = q_ref[...].reshape(rows, DH_A).astype(BF16)
        nl = kmean_scr.shape[0]
        gate = lax.dot_general(qb, kmean_scr[...].astype(BF16), nt, preferred_element_type=F32)
        glane = lax.broadcasted_iota(jnp.int32, (rows, nl), 1)
        gate = jnp.where(glane % H_A == row_h, gate, -jnp.inf)
        sel = []
        sel_lane = lax.broadcasted_iota(jnp.int32, sel_scr.shape, 1)
        sel_tile = jnp.zeros(sel_scr.shape, jnp.int32)
        for t in range(MOBA_TOPK):
            mx = jnp.max(gate, axis=1, keepdims=True)
            idx = jnp.min(jnp.where(gate == mx, glane, nl), axis=1, keepdims=True)
            sel.append(idx // H_A)
            sel_tile = jnp.where(sel_lane == t, idx // H_A, sel_tile)
            gate = jnp.where(glane == idx, -jnp.inf, gate)
        sel_scr[...] = sel_tile

        kn2 = kn_ref[...].reshape(rows, DH_A).astype(BF16)
        olane = lax.broadcasted_iota(jnp.int32, (1, rows), 1)
        own_pos = past_len + olane // H_A
        s_own = lax.dot_general(qb, kn2, nt, preferred_element_type=F32) * scale2
        s_own = s_own - slope2 * (qpos - own_pos).astype(F32)
        s_own = jnp.where(((olane % H_A) == row_h) & (own_pos <= qpos), s_own, -jnp.inf)

        def max_body(g, macc):
            return jnp.maximum(macc, jnp.where(picked(sel, g), pmax_scr[g], -jnp.inf))

        macc = lax.fori_loop(0, n_pages, max_body, jnp.full((rows, 128), -jnp.inf, F32), unroll=4)
        m = jnp.maximum(jnp.max(macc, axis=1, keepdims=True), jnp.max(s_own, axis=1, keepdims=True))
        m_scr[...] = m
        p_own = jnp.exp2(s_own - m)
        l_scr[...] = jnp.where(lax.broadcasted_iota(jnp.int32, l_scr.shape, 1) == 0,
                               jnp.sum(p_own, axis=1, keepdims=True), 0.0)
        vn2 = vn_ref[...].reshape(rows, DH_A).astype(BF16)
        acc_scr[...] = jnp.dot(p_own.astype(BF16), vn2, preferred_element_type=F32)

    @pl.when(s >= n_steps)
    def _():
        sel = [sel_scr[:, t:t + 1] for t in range(MOBA_TOPK)]
        m = m_scr[...]
        acc = acc_scr[...]
        lacc = l_scr[...]
        for p in range(pp):
            g = (s - n_steps) * pp + p
            pr = jnp.exp2(s_scr[:, chunk(g)] - jnp.where(picked(sel, g), m, jnp.inf))
            lacc = lacc + fold_lanes(pr, jnp.add)
            v2 = v_refs[p][...].reshape(cols, DH_A).astype(BF16)
            acc = acc + jnp.dot(pr.astype(BF16), v2, preferred_element_type=F32)
        acc_scr[...] = acc
        l_scr[...] = lacc

    @pl.when(s == 2 * n_steps - 1)
    def _():
        o = acc_scr[...] / jnp.sum(l_scr[...], axis=1, keepdims=True)
        o = o * _silu(ga_ref[...].reshape(rows, DH_A))
        o_ref[...] = o.reshape(tq, H_A, DH_A)


def _sample_attn(cache_k4, cache_v4, page_table, slopes, qkvg):
    b, n_pages = page_table.shape
    _, tq, _, _ = qkvg[0].shape
    pp = PAGES_PER_STEP
    n_steps = n_pages // pp
    past_len = n_pages * PAGE_SIZE
    nbp = past_len // MOBA_BLOCK
    rows = tq * H_A

    def k_spec(p):
        return pl.BlockSpec((None, PAGE_SIZE, H_A, DH_A),
                            lambda bi, s, pt, sl: (pt[bi, jnp.minimum(s, n_steps - 1) * pp + p], 0, 0, 0))

    def v_spec(p):
        return pl.BlockSpec((None, PAGE_SIZE, H_A, DH_A),
                            lambda bi, s, pt, sl: (pt[bi, jnp.maximum(s - n_steps, 0) * pp + p], 0, 0, 0))

    new_spec = pl.BlockSpec((None, tq, H_A, DH_A), lambda bi, s, pt, sl: (bi, 0, 0, 0))

    grid_spec = pltpu.PrefetchScalarGridSpec(
        num_scalar_prefetch=2,
        grid=(b, 2 * n_steps),
        in_specs=[k_spec(p) for p in range(pp)] + [v_spec(p) for p in range(pp)] + [new_spec] * 4,
        out_specs=pl.BlockSpec((None, tq, H_A, DH_A), lambda bi, s, pt, sl: (bi, 0, 0, 0)),
        scratch_shapes=[
            pltpu.VMEM((rows, n_pages * PAGE_SIZE * H_A), F32),
            pltpu.VMEM((n_pages, rows, 128), F32),
            pltpu.VMEM((nbp * H_A, DH_A), F32),
            pltpu.VMEM((rows, DH_A), F32),
            pltpu.VMEM((rows, 128), F32),
            pltpu.VMEM((rows, 1), F32),
            pltpu.VMEM((rows, 128), jnp.int32),
        ],
    )
    return pl.pallas_call(
        functools.partial(_sample_attn_kernel, n_steps=n_steps, tq=tq, past_len=past_len),
        grid_spec=grid_spec,
        out_shape=jax.ShapeDtypeStruct((b, tq, H_A, DH_A), F32),
        compiler_params=_cparams(("parallel", "arbitrary")),
        name="sample_attn",
    )(page_table, slopes, *([cache_k4] * pp), *([cache_v4] * pp), *qkvg)


def _alibi_slopes(n_heads):
    return 2.0 ** (-8.0 * jnp.arange(1, n_heads + 1, dtype=F32) / n_heads)


def kernel(x_prompt, x_sample, cache_k, cache_v, state_pool, state_C, state_n, state_m, page_table,
           c_prompt, c_sample, ada_w, ada_b, norm_pre, norm_post, w_in_even, pool_w, pool_scale,
           w_out_even, w_in_odd, b_igate, b_fgate, head_norm, w_out_odd):
    bp, tp, d = x_prompt.shape
    bs, ts, _ = x_sample.shape
    n_pages = page_table.shape[1]
    past_len = n_pages * PAGE_SIZE
    w_a = H_A * DH_A
    w_c = H_C * DH_C
    rp, rs = bp * tp, bs * ts
    tm_p = 1024

    mods = _adaln(jnp.concatenate([c_prompt, c_sample], axis=0), ada_w, ada_b)

    def mod_parts(l):
        m = mods[l]
        shift, scale, gate = m[:, :d], m[:, d:2 * d], m[:, 2 * d:]
        prompt = tuple(a[:bp].reshape(bp, 1, d) for a in (shift, scale, gate))
        sample = tuple(jnp.repeat(a[bp:], ts, axis=0) for a in (shift, scale, gate))
        return prompt, sample

    slopes = _alibi_slopes(H_A)
    xp = x_prompt.reshape(rp, d)
    xs = x_sample.reshape(rs, d)

    (sh_p, sc_p, gt_p), (sh_s, sc_s, gt_s) = mod_parts(0)
    w_in0 = w_in_even[0].astype(BF16)
    zp, _ = _inproj(xp, norm_pre[0], sc_p, sh_p, w_in0, None, rows_per_group=tp, tm=tm_p)
    zs, qkvg_s = _inproj(xs, norm_pre[0], sc_s, sh_s, w_in0, None, rows_per_group=ts, tm=rs, hm=(0, 4, H_A))
    qkvg_s = [a.reshape(bs, ts, H_A, DH_A) for a in qkvg_s]
    kv_s = qkvg_s[1:3]
    z4p = zp.reshape(zp.shape[0], bp, tp, SEG)
    z4s = zs.reshape(zs.shape[0], bs, ts, SEG)

    pool_w_bf = pool_w[0].astype(BF16)
    att_p, w_in1 = _moba_prompt(z4p, slopes, w_next=w_in_odd[0])
    pool_p, pstate_p = _pool(z4p, jnp.zeros((bp, POOL_HIST, SEG), F32), pool_w_bf, pool_scale[0],
                             pos0=0, tt=512, out_dtype=BF16)

    att_s = _sample_attn(cache_k[0], cache_v[0], page_table, slopes, qkvg_s)
    pool_s, pstate_s = _pool(z4s, state_pool[0], pool_w_bf, pool_scale[0],
                             pos0=past_len, tt=ts, out_dtype=F32)

    w_out0 = w_out_even[0].astype(BF16)
    w_list0 = [w_out0[:w_a], w_out0[w_a:]]
    xp1, kv_p = _outproj([att_p.reshape(rp, w_a), pool_p.reshape(rp, SEG)], w_list0, xp, gt_p, norm_post[0],
                         rows_per_group=tp, tm=512, hm=(zp, (1, 2), H_A))
    xs1 = _outproj([att_s.reshape(rs, w_a), pool_s.reshape(rs, SEG)], w_list0, xs, gt_s, norm_post[0],
                   rows_per_group=ts, tm=rs)

    (sh_p, sc_p, gt_p), (sh_s, sc_s, gt_s) = mod_parts(1)
    n_main = 5 * w_c
    wg1 = jnp.pad(w_in_odd[0][:, n_main:], ((0, 0), (0, 128 - 2 * H_C))).astype(BF16)
    zp, gates_p = _inproj(xp1, norm_pre[1], sc_p, sh_p, w_in1, wg1, rows_per_group=tp, tm=tm_p)
    zs, gates_s = _inproj(xs1, norm_pre[1], sc_s, sh_s, w_in1, wg1, rows_per_group=ts, tm=rs)
    bias = jnp.stack([b_igate[0], b_fgate[0]]).astype(F32)
    hc_p, c_p, n_p, m_p = _mlstm(zp.reshape(zp.shape[0], bp, tp, SEG), gates_p.reshape(bp, tp, 128),
                                 bias, head_norm[0], None, chunk=256, lp=256)
    state = (state_C[0], state_n[0].reshape(bs, H_C, 1, DH_C), state_m[0])
    hc_s, c_s, n_s, m_s = _mlstm(zs.reshape(zs.shape[0], bs, ts, SEG), gates_s.reshape(bs, ts, 128),
                                 bias, head_norm[0], state, chunk=ts, lp=128)
    w_out1 = w_out_odd[0].astype(BF16)
    xp2 = _outproj([hc_p.reshape(rp, w_c)], [w_out1], xp1, gt_p, norm_post[1], rows_per_group=tp, tm=512)
    xs2 = _outproj([hc_s.reshape(rs, w_c)], [w_out1], xs1, gt_s, norm_post[1], rows_per_group=ts, tm=rs)

    kv_shape_p = (1, bp, tp, H_A, DH_A)
    kv_shape_s = (1, bs, ts, H_A, DH_A)
    return (xp2.reshape(bp, tp, d), xs2.reshape(bs, ts, d),
            kv_p[0].reshape(kv_shape_p), kv_p[1].reshape(kv_shape_p),
            kv_s[0].reshape(kv_shape_s), kv_s[1].reshape(kv_shape_s),
            pstate_p[None], pstate_s[None],
            c_p[None], n_p.reshape(1, bp, H_C, DH_C), m_p[:, :, 0, 0][None],
            c_s[None], n_s.reshape(1, bs, H_C, DH_C), m_s[:, :, 0, 0][None])
```
